```python
import math
import jax, jax.numpy as jnp
from jax import lax
import numpy as np

D_MODEL = 2048
BATCH = 4
SEQ = 2048
DEPTH = 1
DEC_BATCH = 128
DEC_SEQ = 8
PAST_LEN = 16384
PAGE_SIZE = 128

EXPAND = 2
D_MIX = EXPAND * D_MODEL
D_SSM = D_MIX // 2
D_SHORT = D_MIX - D_SSM
SSD_HEAD_DIM = 64
SSD_HEADS = D_SSM // SSD_HEAD_DIM
SSD_D_STATE = 128
SSD_GROUPS = 4
SSD_HPG = SSD_HEADS // SSD_GROUPS
SSD_GN = SSD_GROUPS * SSD_D_STATE
SSD_CONV_DIM = D_SSM + 2 * SSD_GN
SSD_CONV_K = 4
SSD_CHUNK = 128
SHORT_CONV_K = 3
SHORT_CHANNEL_GROUPS = 32
EPS = 1e-6
D_IN_PROJ = D_SSM + SSD_CONV_DIM + SSD_HEADS + 4 * D_SHORT
SPLIT_POINTS = (D_SSM,
                D_SSM + SSD_CONV_DIM,
                D_SSM + SSD_CONV_DIM + SSD_HEADS,
                D_SSM + SSD_CONV_DIM + SSD_HEADS + D_SHORT,
                D_SSM + SSD_CONV_DIM + SSD_HEADS + 2 * D_SHORT,
                D_SSM + SSD_CONV_DIM + SSD_HEADS + 3 * D_SHORT)

kernel_name = "hymba_ssd_shortconv_decode_step"


def rmsnorm(x, w):
    xf = x.astype(jnp.float32)
    var = jnp.mean(xf * xf, axis=-1, keepdims=True)
    return (xf * lax.rsqrt(var + EPS) * w.astype(jnp.float32)).astype(x.dtype)


def grouped_rmsnorm(y, w, groups):
    b, L, d = y.shape
    yf = y.astype(jnp.float32).reshape(b, L, groups, d // groups)
    yf = yf * lax.rsqrt(jnp.mean(yf * yf, axis=-1, keepdims=True) + EPS)
    return (yf.reshape(b, L, d) * w.astype(jnp.float32)).astype(y.dtype)


def causal_dwconv(u, prev, w):
    k = w.shape[0]
    L = u.shape[1]
    full = jnp.concatenate([prev.astype(u.dtype), u], axis=1)
    out = full[:, 0:L] * w[0]
    for i in range(1, k):
        out = out + full[:, i:i + L] * w[i]
    return out, full[:, L:]


def ssd_scan(x, dt, a, bmat, cmat, d_skip, h0):
    bsz, L = x.shape[0], x.shape[1]
    chunk = min(SSD_CHUNK, L)
    pad = (-L) % chunk
    xf = x.astype(jnp.float32)
    bf = bmat.astype(jnp.float32)
    cf = cmat.astype(jnp.float32)
    dtf = dt
    if pad:
        pw = ((0, 0), (0, pad))
        xf_p = jnp.pad(xf, pw + ((0, 0), (0, 0)))
        bf = jnp.pad(bf, pw + ((0, 0), (0, 0)))
        cf = jnp.pad(cf, pw + ((0, 0), (0, 0)))
        dtf = jnp.pad(dtf, pw + ((0, 0),))
    else:
        xf_p = xf
    nc = (L + pad) // chunk
    xdt = (xf_p * dtf[..., None]).reshape(bsz, nc, chunk, SSD_GROUPS, SSD_HPG, SSD_HEAD_DIM)
    la = (dtf * a.astype(jnp.float32)).reshape(bsz, nc, chunk, SSD_GROUPS, SSD_HPG)
    bc = bf.reshape(bsz, nc, chunk, SSD_GROUPS, SSD_D_STATE)
    cc = cf.reshape(bsz, nc, chunk, SSD_GROUPS, SSD_D_STATE)
    a_cum = jnp.cumsum(la, axis=2)
    seg = a_cum[:, :, :, None] - a_cum[:, :, None, :]
    causal = jnp.tril(jnp.ones((chunk, chunk), dtype=bool))[:, :, None, None]
    decay = jnp.exp(jnp.where(causal, seg, -jnp.inf))
    cb = jnp.einsum('bclgn,bcsgn->bclsg', cc, bc)
    y_diag = jnp.einsum('bclsg,bclsgh,bcsghp->bclghp', cb, decay, xdt)
    decay_end = jnp.exp(a_cum[:, :, -1:] - a_cum)
    states = jnp.einsum('bclgn,bclgh,bclghp->bcghpn', bc, decay_end, xdt)
    chunk_decay = jnp.exp(a_cum[:, :, -1])

    def step(h, inp):
        s, dcy = inp
        h_next = h * dcy[..., None, None] + s
        return h_next, h

    h_init = h0.astype(jnp.float32).reshape(bsz, SSD_GROUPS, SSD_HPG, SSD_HEAD_DIM, SSD_D_STATE)
    h_fin, h_prev = lax.scan(step, h_init,
                             (jnp.moveaxis(states, 1, 0), jnp.moveaxis(chunk_decay, 1, 0)))
    h_prev = jnp.moveaxis(h_prev, 0, 1)
    y_off = jnp.einsum('bclgn,bcghpn,bclgh->bclghp', cc, h_prev, jnp.exp(a_cum))
    y = (y_diag + y_off).reshape(bsz, nc * chunk, SSD_HEADS, SSD_HEAD_DIM)[:, :L]
    y = y + xf * d_skip.astype(jnp.float32)[:, None]
    return y.astype(x.dtype), h_fin.reshape(bsz, SSD_HEADS, SSD_HEAD_DIM, SSD_D_STATE)


def mixer_layer(x, h_ssm, buf_ssd, buf_short, norm_w, w_in, conv_ssd_w, conv_ssd_b,
                dt_bias, a_log, d_skip, ssd_norm_w, conv_short_w, w_out):
    bsz, L, _ = x.shape
    h = rmsnorm(x, norm_w)
    proj = jnp.einsum('bld,de->ble', h, w_in)
    z_s, xbc, dt_raw, z_c, b_c, c_c, v_c = jnp.split(proj, SPLIT_POINTS, axis=-1)
    xbc, buf_ssd_new = causal_dwconv(xbc, buf_ssd, conv_ssd_w)
    xbc = jax.nn.silu(xbc + conv_ssd_b)
    xs, bm, cm = jnp.split(xbc, (D_SSM, D_SSM + SSD_GN), axis=-1)
    xs = xs.reshape(bsz, L, SSD_HEADS, SSD_HEAD_DIM)
    bm = bm.reshape(bsz, L, SSD_GROUPS, SSD_D_STATE)
    cm = cm.reshape(bsz, L, SSD_GROUPS, SSD_D_STATE)
    dt = jax.nn.softplus(dt_raw.astype(jnp.float32) + dt_bias.astype(jnp.float32))
    a = -jnp.exp(a_log.astype(jnp.float32))
    y_s, h_new = ssd_scan(xs, dt, a, bm, cm, d_skip, h_ssm)
    y_s = grouped_rmsnorm(y_s.reshape(bsz, L, D_SSM) * jax.nn.silu(z_s), ssd_norm_w, SSD_GROUPS)
    conv_out, buf_short_new = causal_dwconv(c_c * v_c, buf_short, conv_short_w)
    y_c = b_c * conv_out * jax.nn.silu(z_c)
    y = jnp.einsum('ble,ed->bld', jnp.concatenate([y_s, y_c], axis=-1), w_out)
    return x + y, h_new.astype(x.dtype), buf_ssd_new, buf_short_new


def setup_inputs(seed: int = 0) -> dict:
    key = jax.random.key(seed)
    ks = jax.random.split(key, 20)
    f32 = jnp.float32
    x_prompt = jax.random.normal(ks[0], (BATCH, SEQ, D_MODEL), f32)
    x_sample = jax.random.normal(ks[1], (DEC_BATCH, DEC_SEQ, D_MODEL), f32)
    state_ssm = 0.1 * jax.random.normal(ks[2], (DEPTH, DEC_BATCH, SSD_HEADS, SSD_HEAD_DIM, SSD_D_STATE), f32)
    state_conv_ssd = jax.random.normal(ks[3], (DEPTH, DEC_BATCH, SSD_CONV_K - 1, SSD_CONV_DIM), f32)
    state_conv_short = jax.random.normal(ks[4], (DEPTH, DEC_BATCH, SHORT_CONV_K - 1, D_SHORT), f32)
    norm_w = 1.0 + 0.02 * jax.random.normal(ks[5], (DEPTH, D_MODEL), f32)
    w_in = jax.random.normal(ks[6], (DEPTH, D_MODEL, D_IN_PROJ), f32) * D_MODEL ** -0.5
    conv_ssd_w = jax.random.normal(ks[7], (DEPTH, SSD_CONV_K, SSD_CONV_DIM), f32) * SSD_CONV_K ** -0.5
    conv_ssd_b = 0.02 * jax.random.normal(ks[8], (DEPTH, SSD_CONV_DIM), f32)
    dt0 = jnp.exp(jax.random.uniform(ks[9], (DEPTH, SSD_HEADS), f32,
                                     math.log(1e-3), math.log(1e-1)))
    dt_bias = dt0 + jnp.log(-jnp.expm1(-dt0))
    a_log = jnp.log(jax.random.uniform(ks[10], (DEPTH, SSD_HEADS), f32, 1.0, 16.0))
    d_skip = 1.0 + 0.1 * jax.random.normal(ks[11], (DEPTH, SSD_HEADS), f32)
    ssd_norm_w = 1.0 + 0.02 * jax.random.normal(ks[12], (DEPTH, D_SSM), f32)
    conv_short_w = jax.random.normal(ks[13], (DEPTH, SHORT_CONV_K, D_SHORT), f32) * SHORT_CONV_K ** -0.5
    w_out = jax.random.normal(ks[14], (DEPTH, D_MIX, D_MODEL), f32) * D_MIX ** -0.5
    final_norm_w = 1.0 + 0.02 * jax.random.normal(ks[15], (D_MODEL,), f32)
    return {"x_prompt": x_prompt, "x_sample": x_sample,
            "state_ssm": state_ssm, "state_conv_ssd": state_conv_ssd,
            "state_conv_short": state_conv_short,
            "norm_w": norm_w, "w_in": w_in, "conv_ssd_w": conv_ssd_w, "conv_ssd_b": conv_ssd_b,
            "dt_bias": dt_bias, "a_log": a_log, "d_skip": d_skip, "ssd_norm_w": ssd_norm_w,
            "conv_short_w": conv_short_w, "w_out": w_out, "final_norm_w": final_norm_w}


def reference(x_prompt, x_sample, state_ssm, state_conv_ssd, state_conv_short,
              norm_w, w_in, conv_ssd_w, conv_ssd_b, dt_bias, a_log, d_skip, ssd_norm_w,
              conv_short_w, w_out, final_norm_w):
    hp = x_prompt
    hs = x_sample
    ssm_p, cs_p, csh_p, ssm_s, cs_s, csh_s = [], [], [], [], [], []
    for layer in range(DEPTH):
        params = (norm_w[layer], w_in[layer], conv_ssd_w[layer], conv_ssd_b[layer],
                  dt_bias[layer], a_log[layer], d_skip[layer], ssd_norm_w[layer],
                  conv_short_w[layer], w_out[layer])
        h0 = jnp.zeros((BATCH, SSD_HEADS, SSD_HEAD_DIM, SSD_D_STATE), hp.dtype)
        b0 = jnp.zeros((BATCH, SSD_CONV_K - 1, SSD_CONV_DIM), hp.dtype)
        c0 = jnp.zeros((BATCH, SHORT_CONV_K - 1, D_SHORT), hp.dtype)
        hp, a1, a2, a3 = mixer_layer(hp, h0, b0, c0, *params)
        hs, s1, s2, s3 = mixer_layer(hs, state_ssm[layer], state_conv_ssd[layer],
                                     state_conv_short[layer], *params)
        ssm_p.append(a1); cs_p.append(a2); csh_p.append(a3)
        ssm_s.append(s1); cs_s.append(s2); csh_s.append(s3)
    y_prompt = rmsnorm(hp, final_norm_w)
    y_sample = rmsnorm(hs, final_norm_w)
    return (y_prompt, y_sample,
            jnp.stack(ssm_p), jnp.stack(cs_p), jnp.stack(csh_p),
            jnp.stack(ssm_s), jnp.stack(cs_s), jnp.stack(csh_s))
```

```python
import functools

import jax
import jax.numpy as jnp
from jax import lax
from jax.experimental import pallas as pl
from jax.experimental.pallas import tpu as pltpu

F32 = jnp.float32
BF16 = jnp.bfloat16

D_MODEL = 2048
D_SSM = 2048
D_SHORT = 2048
HEADS = 32
HEAD_DIM = 64
D_STATE = 128
GROUPS = 4
HPG = HEADS // GROUPS
GROUP_W = HPG * HEAD_DIM
GN = GROUPS * D_STATE
CONV_DIM = D_SSM + 2 * GN
CONV_K = 4
SHORT_K = 3
SSD_CHUNK = 128
EPS = 1e-6
LANES = 128
SUBLANES = 8
VMEM_LIMIT = 56 * 1024 * 1024

NT_DIMS = (((1,), (1,)), ((), ()))


def _silu(x):
    return x * (1.0 / (1.0 + jnp.exp(-x)))


def _softplus(x):
    return jnp.maximum(x, 0.0) + jnp.log1p(jnp.exp(-jnp.abs(x)))


def _split3(x):
    hi = x.astype(BF16)
    r = x - hi.astype(F32)
    mid = r.astype(BF16)
    lo = (r - mid.astype(F32)).astype(BF16)
    return hi, mid, lo


def _norm_rows_to(x_ref, nw_ref, h_ref, rows):
    blk = 32
    nw = nw_ref[...]

    def body(i, carry):
        r = pl.ds(pl.multiple_of(i * blk, blk), blk)
        xf = x_ref[r, :]
        var = jnp.mean(xf * xf, axis=-1, keepdims=True)
        h_ref[r, :] = (xf * lax.rsqrt(var + EPS) * nw).astype(BF16)
        return carry

    lax.fori_loop(0, rows // blk, body, 0)


def _proj_body(x_ref, nw_ref, w_ref, o_ref, h_ref, *, tm):
    @pl.when(pl.program_id(1) == 0)
    def _():
        _norm_rows_to(x_ref, nw_ref, h_ref, tm)

    o_ref[...] = jnp.dot(h_ref[...], w_ref[...], preferred_element_type=F32).astype(o_ref.dtype)


def _proj(x, nw, w, out_dtype, *, tm, tn):
    t, n = x.shape[0], w.shape[1]
    return pl.pallas_call(
        functools.partial(_proj_body, tm=tm),
        grid=(t // tm, n // tn),
        in_specs=[
            pl.BlockSpec((tm, D_MODEL), lambda m, j: (m, 0)),
            pl.BlockSpec((1, D_MODEL), lambda m, j: (0, 0)),
            pl.BlockSpec((D_MODEL, tn), lambda m, j: (0, j)),
        ],
        out_specs=pl.BlockSpec((tm, tn), lambda m, j: (m, j)),
        out_shape=jax.ShapeDtypeStruct((t, n), out_dtype),
        scratch_shapes=[pltpu.VMEM((tm, D_MODEL), BF16)],
        compiler_params=pltpu.CompilerParams(
            dimension_semantics=("arbitrary", "arbitrary"), vmem_limit_bytes=VMEM_LIMIT),
        name="in_proj",
    )(x, nw, w)


def _short_body(x_ref, nw_ref, wz_ref, wb_ref, wc_ref, wv_ref, cw_ref, st_ref,
                y_ref, ns_ref, h_ref, pad_ref, *carry, tm, tn, nseq, tps):
    m = pl.program_id(0)
    j = pl.program_id(1)
    lt = tm // nseq

    @pl.when(j == 0)
    def _():
        _norm_rows_to(x_ref, nw_ref, h_ref, tm)

    h = h_ref[...]
    c = jnp.dot(h, wc_ref[...], preferred_element_type=F32)
    v = jnp.dot(h, wv_ref[...], preferred_element_type=F32)
    pad_ref[:, SUBLANES:SUBLANES + lt, :] = (c * v).reshape(nseq, lt, tn)
    if tps > 1:
        (carry_ref,) = carry
        first = (m % tps) == 0

        @pl.when(first)
        def _():
            pad_ref[:, SUBLANES - 2:SUBLANES, :] = st_ref[...]

        @pl.when(jnp.logical_not(first))
        def _():
            pad_ref[:, 0:SUBLANES, :] = carry_ref[j]
    else:
        pad_ref[:, SUBLANES - 2:SUBLANES, :] = st_ref[...]

    conv = pad_ref[:, 6:6 + lt, :] * cw_ref[0:1, :]
    conv = conv + pad_ref[:, 7:7 + lt, :] * cw_ref[1:2, :]
    conv = conv + pad_ref[:, 8:8 + lt, :] * cw_ref[2:3, :]
    b = jnp.dot(h, wb_ref[...], preferred_element_type=F32)
    z = jnp.dot(h, wz_ref[...], preferred_element_type=F32)
    y = b * conv.reshape(tm, tn) * _silu(z)
    y_ref[...] = y.astype(y_ref.dtype)
    ns_ref[...] = pad_ref[:, lt + 6:lt + 8, :]
    if tps > 1:
        carry_ref[j] = pad_ref[:, lt:lt + SUBLANES, :]


def _short(x, nw, w_sh, conv_w, state, out_dtype, *, tm, tn, nseq, tps):
    t = x.shape[0]
    nj = D_SHORT // tn
    lt = tm // nseq
    scratch = [pltpu.VMEM((tm, D_MODEL), BF16), pltpu.VMEM((nseq, SUBLANES + lt, tn), F32)]
    if tps > 1:
        scratch.append(pltpu.VMEM((nj, nseq, SUBLANES, tn), F32))

    def wspec(k):
        return pl.BlockSpec((D_MODEL, tn), lambda m, j, k=k: (0, k * nj + j))

    sspec = pl.BlockSpec((nseq, SHORT_K - 1, tn), lambda m, j: (m // tps, 0, j))
    nspec = pl.BlockSpec((nseq, SHORT_K - 1, tn), lambda m, j: (m, 0, j))
    y, tails = pl.pallas_call(
        functools.partial(_short_body, tm=tm, tn=tn, nseq=nseq, tps=tps),
        grid=(t // tm, nj),
        in_specs=[
            pl.BlockSpec((tm, D_MODEL), lambda m, j: (m, 0)),
            pl.BlockSpec((1, D_MODEL), lambda m, j: (0, 0)),
            wspec(0), wspec(1), wspec(2), wspec(3),
            pl.BlockSpec((SHORT_K, tn), lambda m, j: (0, j)),
            sspec,
        ],
        out_specs=[pl.BlockSpec((tm, tn), lambda m, j: (m, j)), nspec],
        out_shape=[jax.ShapeDtypeStruct((t, D_SHORT), out_dtype),
                   jax.ShapeDtypeStruct(((t // tm) * nseq, SHORT_K - 1, D_SHORT), F32)],
        scratch_shapes=scratch,
        compiler_params=pltpu.CompilerParams(
            dimension_semantics=("arbitrary", "arbitrary"), vmem_limit_bytes=VMEM_LIMIT),
        name="short_conv",
    )(x, nw, w_sh, w_sh, w_sh, w_sh, conv_w, state)
    tails = tails.reshape(t // (tm * tps), tps, nseq, SHORT_K - 1, D_SHORT)[:, -1]
    return y, tails.reshape(state.shape)


def _ssd_body(z_ref, xbc_ref, dt_ref, cs_ref, h0_ref, cw_ref, cb_ref, dtb_ref, alr_ref, alc_ref,
              dexp_ref, gnw_ref, y_ref, hout_ref, cso_ref, hst_ref, xpad_ref, xc_ref, ysc_ref,
              *, lc, nc):
    c = pl.program_id(1)

    @pl.when(c == 0)
    def _():
        hst_ref[...] = h0_ref[0]
        xpad_ref[5:8, :] = cs_ref[0]

    if nc > 1:
        @pl.when(c > 0)
        def _():
            xpad_ref[0:SUBLANES, :] = xpad_ref[lc:lc + SUBLANES, :]

    xpad_ref[SUBLANES:SUBLANES + lc, :] = xbc_ref[...]

    cblk = 512
    for k in range(CONV_DIM // cblk):
        cs = slice(k * cblk, (k + 1) * cblk)
        acc = xpad_ref[5:5 + lc, cs] * cw_ref[0:1, cs]
        for i in range(1, CONV_K):
            acc = acc + xpad_ref[5 + i:5 + i + lc, cs] * cw_ref[i:i + 1, cs]
        xc_ref[:, cs] = _silu(acc + cb_ref[:, cs])

    row_i = lax.broadcasted_iota(jnp.int32, (lc, lc), 0)
    col_i = lax.broadcasted_iota(jnp.int32, (lc, lc), 1)
    causal = row_i >= col_i
    tri = causal.astype(BF16)
    upper = (row_i <= col_i).astype(BF16)
    ident = (lax.broadcasted_iota(jnp.int32, (LANES, LANES), 0)
             == lax.broadcasted_iota(jnp.int32, (LANES, LANES), 1)).astype(BF16)

    dt = _softplus(dt_ref[...] + dtb_ref[...])
    la = dt * (-jnp.exp(alr_ref[...]))
    a_cum = sum(jnp.dot(tri, p, preferred_element_type=F32) for p in _split3(la))
    dt_t = sum(lax.dot_general(ident, p, NT_DIMS, preferred_element_type=F32)
               for p in _split3(dt))
    la_t = dt_t * (-jnp.exp(alc_ref[...]))
    a_cum_t = sum(jnp.dot(p, upper, preferred_element_type=F32) for p in _split3(la_t))
    a_last_t = a_cum_t[:, lc - 1:lc]
    w_t = jnp.exp(a_last_t - a_cum_t) * dt_t
    chunk_decay_t = jnp.exp(a_last_t)
    exp_a_cum = jnp.exp(a_cum)

    for g in range(GROUPS):
        bg = xc_ref[:, D_SSM + g * D_STATE:D_SSM + (g + 1) * D_STATE].astype(BF16)
        cg = xc_ref[:, D_SSM + GN + g * D_STATE:D_SSM + GN + (g + 1) * D_STATE].astype(BF16)
        rows = slice(g * GROUP_W, (g + 1) * GROUP_W)
        h_prev = hst_ref[rows, :]
        cb = lax.dot_general(cg, bg, NT_DIMS, preferred_element_type=F32)
        y_off = lax.dot_general(cg, h_prev.astype(BF16), NT_DIMS,
                                preferred_element_type=F32)
        xs_g = xc_ref[:, rows]
        w_rows = []
        cd_rows = []
        for jh in range(HPG):
            hh = g * HPG + jh
            seg = a_cum[:, hh:hh + 1] - a_cum_t[hh:hh + 1, :]
            decay = jnp.exp(jnp.where(causal, seg, -jnp.inf))
            mh = cb * decay * dt_t[hh:hh + 1, :]
            xs_h = xs_g[:, jh * HEAD_DIM:(jh + 1) * HEAD_DIM]
            y_diag = jnp.dot(mh.astype(BF16), xs_h.astype(BF16), preferred_element_type=F32)
            y_h = y_diag + y_off[:, jh * HEAD_DIM:(jh + 1) * HEAD_DIM] * exp_a_cum[:, hh:hh + 1]
            ysc_ref[:, hh * HEAD_DIM:(hh + 1) * HEAD_DIM] = y_h
            w_rows.append(jnp.broadcast_to(w_t[hh:hh + 1, :], (HEAD_DIM, lc)))
            cd_rows.append(jnp.broadcast_to(chunk_decay_t[hh:hh + 1, :], (HEAD_DIM, D_STATE)))
        xw_t = xs_g.T * jnp.concatenate(w_rows, axis=0)
        s_g = jnp.dot(xw_t.astype(BF16), bg, preferred_element_type=F32)
        hst_ref[rows, :] = h_prev * jnp.concatenate(cd_rows, axis=0) + s_g

    y = ysc_ref[...] + xc_ref[:, 0:D_SSM] * dexp_ref[...]
    y = y * _silu(z_ref[...].astype(F32))
    for g in range(GROUPS):
        cs = slice(g * GROUP_W, (g + 1) * GROUP_W)
        blk = y[:, cs]
        ms = jnp.mean(blk * blk, axis=-1, keepdims=True)
        y_ref[:, cs] = (blk * lax.rsqrt(ms + EPS) * gnw_ref[:, cs]).astype(y_ref.dtype)

    @pl.when(c == nc - 1)
    def _():
        hout_ref[0] = hst_ref[...]
        cso_ref[0] = xpad_ref[lc + 5:lc + 8, :]


def _ssd(z, xbc, dt_raw, conv_state, h0, conv_w, conv_b, dtb, alr, alc, dexp, gnw, out_dtype,
         *, nb, nc, lc):
    t = z.shape[0]
    tok = lambda width: pl.BlockSpec((lc, width), lambda b, c: (b * nc + c, 0))
    par = lambda r, width: pl.BlockSpec((r, width), lambda b, c: (0, 0))
    cs_spec = pl.BlockSpec((1, CONV_K - 1, CONV_DIM), lambda b, c: (b, 0, 0))
    h_spec = pl.BlockSpec((1, D_SSM, D_STATE), lambda b, c: (b, 0, 0))
    return pl.pallas_call(
        functools.partial(_ssd_body, lc=lc, nc=nc),
        grid=(nb, nc),
        in_specs=[tok(D_SSM), tok(CONV_DIM), tok(LANES), cs_spec, h_spec,
                  par(CONV_K, CONV_DIM), par(1, CONV_DIM), par(1, LANES), par(1, LANES),
                  par(LANES, 1), par(1, D_SSM), par(1, D_SSM)],
        out_specs=[tok(D_SSM), h_spec, cs_spec],
        out_shape=[jax.ShapeDtypeStruct((t, D_SSM), out_dtype),
                   jax.ShapeDtypeStruct((nb, D_SSM, D_STATE), F32),
                   jax.ShapeDtypeStruct((nb, CONV_K - 1, CONV_DIM), F32)],
        scratch_shapes=[pltpu.VMEM((D_SSM, D_STATE), F32),
                        pltpu.VMEM((SUBLANES + lc, CONV_DIM), F32),
                        pltpu.VMEM((lc, CONV_DIM), F32),
                        pltpu.VMEM((lc, D_SSM), F32)],
        compiler_params=pltpu.CompilerParams(
            dimension_semantics=("arbitrary", "arbitrary"), vmem_limit_bytes=VMEM_LIMIT),
        name="ssd_scan",
    )(z, xbc, dt_raw, conv_state, h0, conv_w, conv_b, dtb, alr, alc, dexp, gnw)


def _out_body(ys_ref, yc_ref, x_ref, w1_ref, w2_ref, fnw_ref, o_ref):
    acc = jnp.dot(ys_ref[...].astype(BF16), w1_ref[...], preferred_element_type=F32)
    acc = acc + jnp.dot(yc_ref[...].astype(BF16), w2_ref[...], preferred_element_type=F32)
    r = x_ref[...] + acc
    var = jnp.mean(r * r, axis=-1, keepdims=True)
    o_ref[...] = r * lax.rsqrt(var + EPS) * fnw_ref[...]


def _out(ys, yc, x, w_o, fnw, *, tm):
    t = x.shape[0]
    row = lambda width: pl.BlockSpec((tm, width), lambda m: (m, 0))
    return pl.pallas_call(
        _out_body,
        grid=(t // tm,),
        in_specs=[row(D_SSM), row(D_SHORT), row(D_MODEL),
                  pl.BlockSpec((D_SSM, D_MODEL), lambda m: (0, 0)),
                  pl.BlockSpec((D_SHORT, D_MODEL), lambda m: (1, 0)),
                  pl.BlockSpec((1, D_MODEL), lambda m: (0, 0))],
        out_specs=row(D_MODEL),
        out_shape=jax.ShapeDtypeStruct((t, D_MODEL), F32),
        compiler_params=pltpu.CompilerParams(
            dimension_semantics=("arbitrary",), vmem_limit_bytes=VMEM_LIMIT),
        name="out_proj",
    )(ys, yc, x, w_o, w_o, fnw)


def _mixer(x2d, h0, conv_state, short_state, p, *, nb, seqlen, act_dtype, tm, short_nseq, short_tps):
    lc = min(SSD_CHUNK, seqlen)
    nc = seqlen // lc
    z = _proj(x2d, p["nw"], p["w_z"], act_dtype, tm=tm, tn=512)
    xbc = _proj(x2d, p["nw"], p["w_x"], F32, tm=tm, tn=512)
    dt_raw = _proj(x2d, p["nw"], p["w_dt"], F32, tm=tm, tn=LANES)
    y_c, short_new = _short(x2d, p["nw"], p["w_sh"], p["conv_short_w"], short_state, act_dtype,
                            tm=tm, tn=256, nseq=short_nseq, tps=short_tps)
    y_s, h_new, conv_new = _ssd(z, xbc, dt_raw, conv_state, h0, p["conv_w"], p["conv_b"], p["dtb"],
                                p["alr"], p["alc"], p["dexp"], p["gnw"], act_dtype,
                                nb=nb, nc=nc, lc=lc)
    return y_s, y_c, h_new, conv_new, short_new


def kernel(x_prompt, x_sample, state_ssm, state_conv_ssd, state_conv_short, norm_w, w_in, conv_ssd_w,
           conv_ssd_b, dt_bias, a_log, d_skip, ssd_norm_w, conv_short_w, w_out, final_norm_w):
    depth = norm_w.shape[0]
    assert depth == 1, "the output projection fuses the final rmsnorm, valid for a single layer"
    bp, lp, _ = x_prompt.shape
    bs, ls, _ = x_sample.shape
    hp = x_prompt.reshape(bp * lp, D_MODEL)
    hs = x_sample.reshape(bs * ls, D_MODEL)
    o1 = D_SSM
    o2 = o1 + CONV_DIM
    o3 = o2 + HEADS
    pad_h = LANES - HEADS
    outs = [[] for _ in range(6)]
    for layer in range(depth):
        w = w_in[layer]
        al = jnp.pad(a_log[layer], (0, pad_h))
        p = dict(
            nw=norm_w[layer][None, :],
            w_z=w[:, :o1].astype(BF16),
            w_x=w[:, o1:o2].astype(BF16),
            w_dt=jnp.pad(w[:, o2:o3], ((0, 0), (0, pad_h))).astype(BF16),
            w_sh=w[:, o3:].astype(BF16),
            conv_short_w=conv_short_w[layer],
            conv_w=conv_ssd_w[layer],
            conv_b=conv_ssd_b[layer][None, :],
            dtb=jnp.pad(dt_bias[layer], (0, pad_h))[None, :],
            alr=al[None, :],
            alc=al[:, None],
            dexp=jnp.repeat(d_skip[layer], HEAD_DIM)[None, :],
            gnw=ssd_norm_w[layer][None, :],
        )
        w_o = w_out[layer].astype(BF16)

        tm_p = 1024
        ys, yc, a1, a2, a3 = _mixer(
            hp, jnp.zeros((bp, D_SSM, D_STATE), F32), jnp.zeros((bp, CONV_K - 1, CONV_DIM), F32),
            jnp.zeros((bp, SHORT_K - 1, D_SHORT), F32), p,
            nb=bp, seqlen=lp, act_dtype=BF16, tm=tm_p, short_nseq=1, short_tps=lp // tm_p)
        tm_s = bs * ls
        ys2, yc2, s1, s2, s3 = _mixer(
            hs, state_ssm[layer].reshape(bs, D_SSM, D_STATE), state_conv_ssd[layer],
            state_conv_short[layer], p,
            nb=bs, seqlen=ls, act_dtype=F32, tm=tm_s, short_nseq=bs, short_tps=1)
        fw = final_norm_w[None, :]
        hp = _out(ys, yc, hp, w_o, fw, tm=256)
        hs = _out(ys2, yc2, hs, w_o, fw, tm=256)
        for lst, val in zip(outs, (a1.reshape(bp, HEADS, HEAD_DIM, D_STATE), a2, a3,
                                   s1.reshape(bs, HEADS, HEAD_DIM, D_STATE), s2, s3)):
            lst.append(val)
    return (hp.reshape(bp, lp, D_MODEL), hs.reshape(bs, ls, D_MODEL),
            *(jnp.stack(v) for v in outs))
```

```python
import functools

import jax
import jax.numpy as jnp
from jax import lax
from jax.experimental import pallas as pl
from jax.experimental.pallas import tpu as pltpu

F32 = jnp.float32
BF16 = jnp.bfloat16

D_MODEL = 2048
D_SSM = 2048
D_SHORT = 2048
HEADS = 32
HEAD_DIM = 64
D_STATE = 128
GROUPS = 4
HPG = HEADS // GROUPS
GROUP_W = HPG * HEAD_DIM
GN = GROUPS * D_STATE
CONV_DIM = D_SSM + 2 * GN
CONV_K = 4
SHORT_K = 3
SSD_CHUNK = 128
EPS = 1e-6
LANES = 128
SUBLANES = 8
VMEM_LIMIT = 56 * 1024 * 1024

TN = 512
Z_TILES = D_SSM // TN
XD_W = CONV_DIM + TN
XD_TILES = XD_W // TN
MAIN_W = D_SSM + XD_W
MAIN_VALID = D_SSM + CONV_DIM + HEADS
DT_BLOCK = CONV_DIM // LANES

NT_DIMS = (((1,), (1,)), ((), ()))


def _silu(x):
    return x * (1.0 / (1.0 + jnp.exp(-x)))


def _softplus(x):
    return jnp.maximum(x, 0.0) + jnp.log1p(jnp.exp(-jnp.abs(x)))


def _split3(x):
    hi = x.astype(BF16)
    r = x - hi.astype(F32)
    mid = r.astype(BF16)
    lo = (r - mid.astype(F32)).astype(BF16)
    return hi, mid, lo


def _wprep_body(a_ref, b_ref, o_ref, *, shift, valid):
    if shift:
        w = jnp.concatenate([a_ref[:, shift:], b_ref[:, :shift]], axis=1)
    else:
        w = a_ref[...]
    col = pl.program_id(0) * TN + lax.broadcasted_iota(jnp.int32, w.shape, 1)
    o_ref[...] = jnp.where(col < valid, w, 0.0).astype(BF16)


def _wprep(w, *, start, n_out, valid):
    base = (start // TN) * TN
    shift = start - base
    assert shift < LANES and n_out % TN == 0
    return pl.pallas_call(
        functools.partial(_wprep_body, shift=shift, valid=valid),
        grid=(n_out // TN,),
        in_specs=[pl.BlockSpec((D_MODEL, TN), lambda j: (0, base // TN + j)),
                  pl.BlockSpec((D_MODEL, LANES), lambda j: (0, (base + TN * (j + 1)) // LANES))],
        out_specs=pl.BlockSpec((D_MODEL, TN), lambda j: (0, j)),
        out_shape=jax.ShapeDtypeStruct((D_MODEL, n_out), BF16),
        compiler_params=pltpu.CompilerParams(dimension_semantics=("arbitrary",)),
        name="wprep",
    )(w, w)


def _proj_body(x_ref, nw_ref, w_ref, h_ref, z_ref, xd_ref, *, tm):
    j = pl.program_id(1)

    @pl.when(j == 0)
    def _():
        blk = 32
        nw = nw_ref[...]

        def body(i, carry):
            r = pl.ds(pl.multiple_of(i * blk, blk), blk)
            xf = x_ref[r, :]
            var = jnp.mean(xf * xf, axis=-1, keepdims=True)
            h_ref[r, :] = (xf * lax.rsqrt(var + EPS) * nw).astype(BF16)
            return carry

        lax.fori_loop(0, tm // blk, body, 0, unroll=2)

    acc = jnp.dot(h_ref[...], w_ref[...], preferred_element_type=F32)

    @pl.when(j < Z_TILES)
    def _():
        z_ref[...] = acc.astype(z_ref.dtype)

    @pl.when(j >= Z_TILES)
    def _():
        xd_ref[...] = acc


def _proj(x, nw, w_main, z_dtype, *, tm):
    t = x.shape[0]
    return pl.pallas_call(
        functools.partial(_proj_body, tm=tm),
        grid=(t // tm, Z_TILES + XD_TILES),
        in_specs=[
            pl.BlockSpec((tm, D_MODEL), lambda m, j: (m, 0)),
            pl.BlockSpec((1, D_MODEL), lambda m, j: (0, 0)),
            pl.BlockSpec((D_MODEL, TN), lambda m, j: (0, j)),
        ],
        out_specs=[pl.BlockSpec((tm, D_MODEL), lambda m, j: (m, 0)),
                   pl.BlockSpec((tm, TN), lambda m, j: (m, jnp.minimum(j, Z_TILES - 1))),
                   pl.BlockSpec((tm, TN), lambda m, j: (m, jnp.maximum(j - Z_TILES, 0)))],
        out_shape=[jax.ShapeDtypeStruct((t, D_MODEL), BF16),
                   jax.ShapeDtypeStruct((t, D_SSM), z_dtype),
                   jax.ShapeDtypeStruct((t, XD_W), F32)],
        compiler_params=pltpu.CompilerParams(
            dimension_semantics=("arbitrary", "arbitrary"), vmem_limit_bytes=VMEM_LIMIT),
        name="in_proj",
    )(x, nw, w_main)


def _short_body(h_ref, wz_ref, wb_ref, wc_ref, wv_ref, cw_ref, st_ref,
                y_ref, ns_ref, pad_ref, *carry, tm, tn, nseq, tps):
    m = pl.program_id(0)
    j = pl.program_id(1)
    lt = tm // nseq

    h = h_ref[...]
    c = jnp.dot(h, wc_ref[...], preferred_element_type=F32)
    v = jnp.dot(h, wv_ref[...], preferred_element_type=F32)
    pad_ref[:, SUBLANES:SUBLANES + lt, :] = (c * v).reshape(nseq, lt, tn)
    if tps > 1:
        (carry_ref,) = carry
        first = (m % tps) == 0

        @pl.when(first)
        def _():
            pad_ref[:, SUBLANES - 2:SUBLANES, :] = st_ref[...]

        @pl.when(jnp.logical_not(first))
        def _():
            pad_ref[:, 0:SUBLANES, :] = carry_ref[j]
    else:
        pad_ref[:, SUBLANES - 2:SUBLANES, :] = st_ref[...]

    conv = pad_ref[:, 6:6 + lt, :] * cw_ref[0:1, :]
    conv = conv + pad_ref[:, 7:7 + lt, :] * cw_ref[1:2, :]
    conv = conv + pad_ref[:, 8:8 + lt, :] * cw_ref[2:3, :]
    b = jnp.dot(h, wb_ref[...], preferred_element_type=F32)
    z = jnp.dot(h, wz_ref[...], preferred_element_type=F32)
    y = b * conv.reshape(tm, tn) * _silu(z)
    y_ref[...] = y.astype(y_ref.dtype)
    ns_ref[...] = pad_ref[:, lt + 6:lt + 8, :]
    if tps > 1:
        carry_ref[j] = pad_ref[:, lt:lt + SUBLANES, :]


def _short(h, w_sh, conv_w, state, out_dtype, *, tm, tn, nseq, tps):
    t = h.shape[0]
    nj = D_SHORT // tn
    lt = tm // nseq
    scratch = [pltpu.VMEM((nseq, SUBLANES + lt, tn), F32)]
    if tps > 1:
        scratch.append(pltpu.VMEM((nj, nseq, SUBLANES, tn), F32))

    def wspec(k):
        return pl.BlockSpec((D_MODEL, tn), lambda m, j, k=k: (0, k * nj + j))

    sspec = pl.BlockSpec((nseq, SHORT_K - 1, tn), lambda m, j: (m // tps, 0, j))
    nspec = pl.BlockSpec((nseq, SHORT_K - 1, tn), lambda m, j: (m, 0, j))
    y, tails = pl.pallas_call(
        functools.partial(_short_body, tm=tm, tn=tn, nseq=nseq, tps=tps),
        grid=(t // tm, nj),
        in_specs=[
            pl.BlockSpec((tm, D_MODEL), lambda m, j: (m, 0)),
            wspec(0), wspec(1), wspec(2), wspec(3),
            pl.BlockSpec((SHORT_K, tn), lambda m, j: (0, j)),
            sspec,
        ],
        out_specs=[pl.BlockSpec((tm, tn), lambda m, j: (m, j)), nspec],
        out_shape=[jax.ShapeDtypeStruct((t, D_SHORT), out_dtype),
                   jax.ShapeDtypeStruct(((t // tm) * nseq, SHORT_K - 1, D_SHORT), F32)],
        scratch_shapes=scratch,
        compiler_params=pltpu.CompilerParams(
            dimension_semantics=("arbitrary", "arbitrary"), vmem_limit_bytes=VMEM_LIMIT),
        name="short_conv",
    )(h, w_sh, w_sh, w_sh, w_sh, conv_w, state)
    tails = tails.reshape(t // (tm * tps), tps, nseq, SHORT_K - 1, D_SHORT)[:, -1]
    return y, tails.reshape(state.shape)


def _ssd_body(z_ref, xbc_ref, dt_ref, cs_ref, h0_ref, cw_ref, cb_ref, dtb_ref, alr_ref, alc_ref,
              dexp_ref, gnw_ref, y_ref, hout_ref, cso_ref, hst_ref, xpad_ref, xc_ref, ysc_ref,
              *, lc, nc):
    c = pl.program_id(1)

    @pl.when(c == 0)
    def _():
        hst_ref[...] = h0_ref[0]
        xpad_ref[5:8, :] = cs_ref[0]

    if nc > 1:
        @pl.when(c > 0)
        def _():
            xpad_ref[0:SUBLANES, :] = xpad_ref[lc:lc + SUBLANES, :]

    xpad_ref[SUBLANES:SUBLANES + lc, :] = xbc_ref[...]

    cblk = 512
    for k in range(CONV_DIM // cblk):
        cs = slice(k * cblk, (k + 1) * cblk)
        acc = xpad_ref[5:5 + lc, cs] * cw_ref[0:1, cs]
        for i in range(1, CONV_K):
            acc = acc + xpad_ref[5 + i:5 + i + lc, cs] * cw_ref[i:i + 1, cs]
        xc_ref[:, cs] = _silu(acc + cb_ref[:, cs])

    row_i = lax.broadcasted_iota(jnp.int32, (lc, lc), 0)
    col_i = lax.broadcasted_iota(jnp.int32, (lc, lc), 1)
    causal = row_i >= col_i
    tri = causal.astype(BF16)
    upper = (row_i <= col_i).astype(BF16)
    ident = (lax.broadcasted_iota(jnp.int32, (LANES, LANES), 0)
             == lax.broadcasted_iota(jnp.int32, (LANES, LANES), 1)).astype(BF16)

    dt = _softplus(dt_ref[...] + dtb_ref[...])
    la = dt * (-jnp.exp(alr_ref[...]))
    a_cum = sum(jnp.dot(tri, p, preferred_element_type=F32) for p in _split3(la))
    dt_t = sum(lax.dot_general(ident, p, NT_DIMS, preferred_element_type=F32)
               for p in _split3(dt))
    la_t = dt_t * (-jnp.exp(alc_ref[...]))
    a_cum_t = sum(jnp.dot(p, upper, preferred_element_type=F32) for p in _split3(la_t))
    a_last_t = a_cum_t[:, lc - 1:lc]
    w_t = jnp.exp(a_last_t - a_cum_t) * dt_t
    chunk_decay_t = jnp.exp(a_last_t)
    exp_a_cum = jnp.exp(a_cum)

    for g in range(GROUPS):
        bg = xc_ref[:, D_SSM + g * D_STATE:D_SSM + (g + 1) * D_STATE].astype(BF16)
        cg = xc_ref[:, D_SSM + GN + g * D_STATE:D_SSM + GN + (g + 1) * D_STATE].astype(BF16)
        rows = slice(g * GROUP_W, (g + 1) * GROUP_W)
        h_prev = hst_ref[rows, :]
        cb = lax.dot_general(cg, bg, NT_DIMS, preferred_element_type=F32)
        y_off = lax.dot_general(cg, h_prev.astype(BF16), NT_DIMS,
                                preferred_element_type=F32)
        xs_g = xc_ref[:, rows]
        w_rows = []
        cd_rows = []
        for jh in range(HPG):
            hh = g * HPG + jh
            seg = a_cum[:, hh:hh + 1] - a_cum_t[hh:hh + 1, :]
            decay = jnp.exp(jnp.where(causal, seg, -jnp.inf))
            mh = cb * decay * dt_t[hh:hh + 1, :]
            xs_h = xs_g[:, jh * HEAD_DIM:(jh + 1) * HEAD_DIM]
            y_diag = jnp.dot(mh.astype(BF16), xs_h.astype(BF16), preferred_element_type=F32)
            y_h = y_diag + y_off[:, jh * HEAD_DIM:(jh + 1) * HEAD_DIM] * exp_a_cum[:, hh:hh + 1]
            ysc_ref[:, hh * HEAD_DIM:(hh + 1) * HEAD_DIM] = y_h
            w_rows.append(jnp.broadcast_to(w_t[hh:hh + 1, :], (HEAD_DIM, lc)))
            cd_rows.append(jnp.broadcast_to(chunk_decay_t[hh:hh + 1, :], (HEAD_DIM, D_STATE)))
        xw_t = xs_g.T * jnp.concatenate(w_rows, axis=0)
        s_g = jnp.dot(xw_t.astype(BF16), bg, preferred_element_type=F32)
        hst_ref[rows, :] = h_prev * jnp.concatenate(cd_rows, axis=0) + s_g

    y = ysc_ref[...] + xc_ref[:, 0:D_SSM] * dexp_ref[...]
    y = y * _silu(z_ref[...].astype(F32))
    for g in range(GROUPS):
        cs = slice(g * GROUP_W, (g + 1) * GROUP_W)
        blk = y[:, cs]
        ms = jnp.mean(blk * blk, axis=-1, keepdims=True)
        y_ref[:, cs] = (blk * lax.rsqrt(ms + EPS) * gnw_ref[:, cs]).astype(y_ref.dtype)

    @pl.when(c == nc - 1)
    def _():
        hout_ref[0] = hst_ref[...]
        cso_ref[0] = xpad_ref[lc + 5:lc + 8, :]


def _ssd(z, xd, conv_state, h0, conv_w, conv_b, dtb, alr, alc, dexp, gnw, out_dtype,
         *, nb, nc, lc):
    t = z.shape[0]
    tok = lambda width, blk=0: pl.BlockSpec((lc, width), lambda b, c: (b * nc + c, blk))
    par = lambda r, width: pl.BlockSpec((r, width), lambda b, c: (0, 0))
    cs_spec = pl.BlockSpec((1, CONV_K - 1, CONV_DIM), lambda b, c: (b, 0, 0))
    h_spec = pl.BlockSpec((1, D_SSM, D_STATE), lambda b, c: (b, 0, 0))
    return pl.pallas_call(
        functools.partial(_ssd_body, lc=lc, nc=nc),
        grid=(nb, nc),
        in_specs=[tok(D_SSM), tok(CONV_DIM), tok(LANES, DT_BLOCK), cs_spec, h_spec,
                  par(CONV_K, CONV_DIM), par(1, CONV_DIM), par(1, LANES), par(1, LANES),
                  par(LANES, 1), par(1, D_SSM), par(1, D_SSM)],
        out_specs=[tok(D_SSM), h_spec, cs_spec],
        out_shape=[jax.ShapeDtypeStruct((t, D_SSM), out_dtype),
                   jax.ShapeDtypeStruct((nb, D_SSM, D_STATE), F32),
                   jax.ShapeDtypeStruct((nb, CONV_K - 1, CONV_DIM), F32)],
        scratch_shapes=[pltpu.VMEM((D_SSM, D_STATE), F32),
                        pltpu.VMEM((SUBLANES + lc, CONV_DIM), F32),
                        pltpu.VMEM((lc, CONV_DIM), F32),
                        pltpu.VMEM((lc, D_SSM), F32)],
        compiler_params=pltpu.CompilerParams(
            dimension_semantics=("arbitrary", "arbitrary"), vmem_limit_bytes=VMEM_LIMIT),
        name="ssd_scan",
    )(z, xd, xd, conv_state, h0, conv_w, conv_b, dtb, alr, alc, dexp, gnw)


def _out_body(ys_ref, yc_ref, x_ref, w1_ref, w2_ref, fnw_ref, o_ref):
    acc = jnp.dot(ys_ref[...].astype(BF16), w1_ref[...], preferred_element_type=F32)
    acc = acc + jnp.dot(yc_ref[...].astype(BF16), w2_ref[...], preferred_element_type=F32)
    r = x_ref[...] + acc
    var = jnp.mean(r * r, axis=-1, keepdims=True)
    o_ref[...] = r * lax.rsqrt(var + EPS) * fnw_ref[...]


def _out(ys, yc, x, w_o, fnw, *, tm):
    t = x.shape[0]
    row = lambda width: pl.BlockSpec((tm, width), lambda m: (m, 0))
    return pl.pallas_call(
        _out_body,
        grid=(t // tm,),
        in_specs=[row(D_SSM), row(D_SHORT), row(D_MODEL),
                  pl.BlockSpec((D_SSM, D_MODEL), lambda m: (0, 0)),
                  pl.BlockSpec((D_SHORT, D_MODEL), lambda m: (1, 0)),
                  pl.BlockSpec((1, D_MODEL), lambda m: (0, 0))],
        out_specs=row(D_MODEL),
        out_shape=jax.ShapeDtypeStruct((t, D_MODEL), F32),
        compiler_params=pltpu.CompilerParams(
            dimension_semantics=("arbitrary",), vmem_limit_bytes=VMEM_LIMIT),
        name="out_proj",
    )(ys, yc, x, w_o, w_o, fnw)


def _mixer(x2d, h0, conv_state, short_state, p, *, nb, seqlen, act_dtype, tm, short_nseq, short_tps):
    lc = min(SSD_CHUNK, seqlen)
    nc = seqlen // lc
    h, z, xd = _proj(x2d, p["nw"], p["w_main"], act_dtype, tm=tm)
    y_c, short_new = _short(h, p["w_sh"], p["conv_short_w"], short_state, act_dtype,
                            tm=tm, tn=256, nseq=short_nseq, tps=short_tps)
    y_s, h_new, conv_new = _ssd(z, xd, conv_state, h0, p["conv_w"], p["conv_b"], p["dtb"],
                                p["alr"], p["alc"], p["dexp"], p["gnw"], act_dtype,
                                nb=nb, nc=nc, lc=lc)
    return y_s, y_c, h_new, conv_new, short_new


def kernel(x_prompt, x_sample, state_ssm, state_conv_ssd, state_conv_short, norm_w, w_in, conv_ssd_w,
           conv_ssd_b, dt_bias, a_log, d_skip, ssd_norm_w, conv_short_w, w_out, final_norm_w):
    depth = norm_w.shape[0]
    assert depth == 1, "the output projection fuses the final rmsnorm, valid for a single layer"
    bp, lp, _ = x_prompt.shape
    bs, ls, _ = x_sample.shape
    hp = x_prompt.reshape(bp * lp, D_MODEL)
    hs = x_sample.reshape(bs * ls, D_MODEL)
    pad_h = LANES - HEADS
    outs = [[] for _ in range(6)]
    for layer in range(depth):
        w = w_in[layer]
        al = jnp.pad(a_log[layer], (0, pad_h))
        p = dict(
            nw=norm_w[layer][None, :],
            w_main=_wprep(w, start=0, n_out=MAIN_W, valid=MAIN_VALID),
            w_sh=_wprep(w, start=MAIN_VALID, n_out=4 * D_SHORT, valid=4 * D_SHORT),
            conv_short_w=conv_short_w[layer],
            conv_w=conv_ssd_w[layer],
            conv_b=conv_ssd_b[layer][None, :],
            dtb=jnp.pad(dt_bias[layer], (0, pad_h))[None, :],
            alr=al[None, :],
            alc=al[:, None],
            dexp=jnp.repeat(d_skip[layer], HEAD_DIM)[None, :],
            gnw=ssd_norm_w[layer][None, :],
        )
        w_o = w_out[layer].astype(BF16)

        tm_p = 1024
        ys, yc, a1, a2, a3 = _mixer(
            hp, jnp.zeros((bp, D_SSM, D_STATE), F32), jnp.zeros((bp, CONV_K - 1, CONV_DIM), F32),
            jnp.zeros((bp, SHORT_K - 1, D_SHORT), F32), p,
            nb=bp, seqlen=lp, act_dtype=BF16, tm=tm_p, short_nseq=1, short_tps=lp // tm_p)
        tm_s = bs * ls
        ys2, yc2, s1, s2, s3 = _mixer(
            hs, state_ssm[layer].reshape(bs, D_SSM, D_STATE), state_conv_ssd[layer],
            state_conv_short[layer], p,
            nb=bs, seqlen=ls, act_dtype=F32, tm=tm_s, short_nseq=bs, short_tps=1)
        fw = final_norm_w[None, :]
        hp = _out(ys, yc, hp, w_o, fw, tm=256)
        hs = _out(ys2, yc2, hs, w_o, fw, tm=256)
        for lst, val in zip(outs, (a1.reshape(bp, HEADS, HEAD_DIM, D_STATE), a2, a3,
                                   s1.reshape(bs, HEADS, HEAD_DIM, D_STATE), s2, s3)):
            lst.append(val)
    return (hp.reshape(bp, lp, D_MODEL), hs.reshape(bs, ls, D_MODEL),
            *(jnp.stack(v) for v in outs))
```

```python
import functools

import jax
import jax.numpy as jnp
from jax import lax
from jax.experimental import pallas as pl
from jax.experimental.pallas import tpu as pltpu

F32 = jnp.float32
BF16 = jnp.bfloat16

D_MODEL = 2048
D_SSM = 2048
D_SHORT = 2048
HEADS = 32
HEAD_DIM = 64
D_STATE = 128
GROUPS = 4
HPG = HEADS // GROUPS
GROUP_W = HPG * HEAD_DIM
GN = GROUPS * D_STATE
CONV_DIM = D_SSM + 2 * GN
CONV_K = 4
SHORT_K = 3
SSD_CHUNK = 128
EPS = 1e-6
LANES = 128
SUBLANES = 8
VMEM_LIMIT = 56 * 1024 * 1024

TN = 512
Z_TILES = D_SSM // TN
XD_W = CONV_DIM + TN
XD_TILES = XD_W // TN
MAIN_W = D_SSM + XD_W
MAIN_VALID = D_SSM + CONV_DIM + HEADS
DT_BLOCK = CONV_DIM // LANES

NT_DIMS = (((1,), (1,)), ((), ()))


def _silu(x):
    return x * (1.0 / (1.0 + jnp.exp(-x)))


def _softplus(x):
    return jnp.maximum(x, 0.0) + jnp.log1p(jnp.exp(-jnp.abs(x)))


def _split3(x):
    hi = x.astype(BF16)
    r = x - hi.astype(F32)
    mid = r.astype(BF16)
    lo = (r - mid.astype(F32)).astype(BF16)
    return hi, mid, lo


def _wprep_body(*refs, shift, valid):
    if shift:
        a_ref, b_ref, o_ref = refs
        w = jnp.concatenate([a_ref[shift:, :], b_ref[...]], axis=0)
    else:
        a_ref, o_ref = refs
        w = a_ref[...]
    row = pl.program_id(0) * TN + lax.broadcasted_iota(jnp.int32, w.shape, 0)
    o_ref[...] = jnp.where(row < valid, w, 0.0).T.astype(BF16)


def _wprep(w_t, *, start, n_out, valid):
    base = (start // TN) * TN
    shift = start - base
    assert shift % SUBLANES == 0 and n_out % TN == 0
    in_specs = [pl.BlockSpec((TN, D_MODEL), lambda j: (base // TN + j, 0))]
    operands = [w_t]
    if shift:
        assert TN % shift == 0 and base % shift == 0
        in_specs.append(pl.BlockSpec((shift, D_MODEL), lambda j: ((base + TN * (j + 1)) // shift, 0)))
        operands.append(w_t)
    return pl.pallas_call(
        functools.partial(_wprep_body, shift=shift, valid=valid),
        grid=(n_out // TN,),
        in_specs=in_specs,
        out_specs=pl.BlockSpec((D_MODEL, TN), lambda j: (0, j)),
        out_shape=jax.ShapeDtypeStruct((D_MODEL, n_out), BF16),
        compiler_params=pltpu.CompilerParams(dimension_semantics=("arbitrary",)),
        name="wprep",
    )(*operands)


def _proj_body(x_ref, nw_ref, w_ref, h_ref, z_ref, xd_ref, *, tm):
    j = pl.program_id(1)

    @pl.when(j == 0)
    def _():
        blk = 32
        nw = nw_ref[...]

        def body(i, carry):
            r = pl.ds(pl.multiple_of(i * blk, blk), blk)
            xf = x_ref[r, :]
            var = jnp.mean(xf * xf, axis=-1, keepdims=True)
            h_ref[r, :] = (xf * lax.rsqrt(var + EPS) * nw).astype(BF16)
            return carry

        lax.fori_loop(0, tm // blk, body, 0, unroll=2)

    acc = jnp.dot(h_ref[...], w_ref[...], preferred_element_type=F32)

    @pl.when(j < Z_TILES)
    def _():
        z_ref[...] = acc.astype(z_ref.dtype)

    @pl.when(j >= Z_TILES)
    def _():
        xd_ref[...] = acc


def _proj(x, nw, w_main, z_dtype, *, tm):
    t = x.shape[0]
    return pl.pallas_call(
        functools.partial(_proj_body, tm=tm),
        grid=(t // tm, Z_TILES + XD_TILES),
        in_specs=[
            pl.BlockSpec((tm, D_MODEL), lambda m, j: (m, 0)),
            pl.BlockSpec((1, D_MODEL), lambda m, j: (0, 0)),
            pl.BlockSpec((D_MODEL, TN), lambda m, j: (0, j)),
        ],
        out_specs=[pl.BlockSpec((tm, D_MODEL), lambda m, j: (m, 0)),
                   pl.BlockSpec((tm, TN), lambda m, j: (m, jnp.minimum(j, Z_TILES - 1))),
                   pl.BlockSpec((tm, TN), lambda m, j: (m, jnp.maximum(j - Z_TILES, 0)))],
        out_shape=[jax.ShapeDtypeStruct((t, D_MODEL), BF16),
                   jax.ShapeDtypeStruct((t, D_SSM), z_dtype),
                   jax.ShapeDtypeStruct((t, XD_W), F32)],
        compiler_params=pltpu.CompilerParams(
            dimension_semantics=("arbitrary", "arbitrary"), vmem_limit_bytes=VMEM_LIMIT),
        name="in_proj",
    )(x, nw, w_main)


def _short_body(h_ref, wz_ref, wb_ref, wc_ref, wv_ref, cw_ref, st_ref,
                y_ref, ns_ref, pad_ref, *carry, tm, tn, nseq, tps):
    m = pl.program_id(0)
    j = pl.program_id(1)
    lt = tm // nseq

    h = h_ref[...]
    c = jnp.dot(h, wc_ref[...], preferred_element_type=F32)
    v = jnp.dot(h, wv_ref[...], preferred_element_type=F32)
    pad_ref[:, SUBLANES:SUBLANES + lt, :] = (c * v).reshape(nseq, lt, tn)
    if tps > 1:
        (carry_ref,) = carry
        first = (m % tps) == 0

        @pl.when(first)
        def _():
            pad_ref[:, SUBLANES - 2:SUBLANES, :] = st_ref[...]

        @pl.when(jnp.logical_not(first))
        def _():
            pad_ref[:, 0:SUBLANES, :] = carry_ref[j]
    else:
        pad_ref[:, SUBLANES - 2:SUBLANES, :] = st_ref[...]

    conv = pad_ref[:, 6:6 + lt, :] * cw_ref[0:1, :]
    conv = conv + pad_ref[:, 7:7 + lt, :] * cw_ref[1:2, :]
    conv = conv + pad_ref[:, 8:8 + lt, :] * cw_ref[2:3, :]
    b = jnp.dot(h, wb_ref[...], preferred_element_type=F32)
    z = jnp.dot(h, wz_ref[...], preferred_element_type=F32)
    y = b * conv.reshape(tm, tn) * _silu(z)
    y_ref[...] = y.astype(y_ref.dtype)
    ns_ref[...] = pad_ref[:, lt + 6:lt + 8, :]
    if tps > 1:
        carry_ref[j] = pad_ref[:, lt:lt + SUBLANES, :]


def _short(h, w_sh, conv_w, state, out_dtype, *, tm, tn, nseq, tps):
    t = h.shape[0]
    nj = D_SHORT // tn
    lt = tm // nseq
    scratch = [pltpu.VMEM((nseq, SUBLANES + lt, tn), F32)]
    if tps > 1:
        scratch.append(pltpu.VMEM((nj, nseq, SUBLANES, tn), F32))

    def wspec(k):
        return pl.BlockSpec((D_MODEL, tn), lambda m, j, k=k: (0, k * nj + j))

    sspec = pl.BlockSpec((nseq, SHORT_K - 1, tn), lambda m, j: (m // tps, 0, j))
    nspec = pl.BlockSpec((nseq, SHORT_K - 1, tn), lambda m, j: (m, 0, j))
    y, tails = pl.pallas_call(
        functools.partial(_short_body, tm=tm, tn=tn, nseq=nseq, tps=tps),
        grid=(t // tm, nj),
        in_specs=[
            pl.BlockSpec((tm, D_MODEL), lambda m, j: (m, 0)),
            wspec(0), wspec(1), wspec(2), wspec(3),
            pl.BlockSpec((SHORT_K, tn), lambda m, j: (0, j)),
            sspec,
        ],
        out_specs=[pl.BlockSpec((tm, tn), lambda m, j: (m, j)), nspec],
        out_shape=[jax.ShapeDtypeStruct((t, D_SHORT), out_dtype),
                   jax.ShapeDtypeStruct(((t // tm) * nseq, SHORT_K - 1, D_SHORT), F32)],
        scratch_shapes=scratch,
        compiler_params=pltpu.CompilerParams(
            dimension_semantics=("arbitrary", "arbitrary"), vmem_limit_bytes=VMEM_LIMIT),
        name="short_conv",
    )(h, w_sh, w_sh, w_sh, w_sh, conv_w, state)
    tails = tails.reshape(t // (tm * tps), tps, nseq, SHORT_K - 1, D_SHORT)[:, -1]
    return y, tails.reshape(state.shape)


def _ssd_body(z_ref, xbc_ref, dt_ref, cs_ref, h0_ref, cw_ref, cb_ref, dtb_ref, alr_ref, alc_ref,
              dexp_ref, gnw_ref, y_ref, hout_ref, cso_ref, hst_ref, xpad_ref, xc_ref, ysc_ref,
              *, lc, nc):
    c = pl.program_id(1)

    @pl.when(c == 0)
    def _():
        hst_ref[...] = h0_ref[0]
        xpad_ref[5:8, :] = cs_ref[0]

    if nc > 1:
        @pl.when(c > 0)
        def _():
            xpad_ref[0:SUBLANES, :] = xpad_ref[lc:lc + SUBLANES, :]

    xpad_ref[SUBLANES:SUBLANES + lc, :] = xbc_ref[...]

    cblk = 512
    for k in range(CONV_DIM // cblk):
        cs = slice(k * cblk, (k + 1) * cblk)
        acc = xpad_ref[5:5 + lc, cs] * cw_ref[0:1, cs]
        for i in range(1, CONV_K):
            acc = acc + xpad_ref[5 + i:5 + i + lc, cs] * cw_ref[i:i + 1, cs]
        xc_ref[:, cs] = _silu(acc + cb_ref[:, cs])

    row_i = lax.broadcasted_iota(jnp.int32, (lc, lc), 0)
    col_i = lax.broadcasted_iota(jnp.int32, (lc, lc), 1)
    causal = row_i >= col_i
    tri = causal.astype(BF16)
    upper = (row_i <= col_i).astype(BF16)
    ident = (lax.broadcasted_iota(jnp.int32, (LANES, LANES), 0)
             == lax.broadcasted_iota(jnp.int32, (LANES, LANES), 1)).astype(BF16)

    dt = _softplus(dt_ref[...] + dtb_ref[...])
    la = dt * (-jnp.exp(alr_ref[...]))
    a_cum = sum(jnp.dot(tri, p, preferred_element_type=F32) for p in _split3(la))
    dt_t = sum(lax.dot_general(ident, p, NT_DIMS, preferred_element_type=F32)
               for p in _split3(dt))
    la_t = dt_t * (-jnp.exp(alc_ref[...]))
    a_cum_t = sum(jnp.dot(p, upper, preferred_element_type=F32) for p in _split3(la_t))
    a_last_t = a_cum_t[:, lc - 1:lc]
    w_t = jnp.exp(a_last_t - a_cum_t) * dt_t
    chunk_decay_t = jnp.exp(a_last_t)
    exp_a_cum = jnp.exp(a_cum)

    for g in range(GROUPS):
        bg = xc_ref[:, D_SSM + g * D_STATE:D_SSM + (g + 1) * D_STATE].astype(BF16)
        cg = xc_ref[:, D_SSM + GN + g * D_STATE:D_SSM + GN + (g + 1) * D_STATE].astype(BF16)
        rows = slice(g * GROUP_W, (g + 1) * GROUP_W)
        h_prev = hst_ref[rows, :]
        cb = lax.dot_general(cg, bg, NT_DIMS, preferred_element_type=F32)
        y_off = lax.dot_general(cg, h_prev.astype(BF16), NT_DIMS,
                                preferred_element_type=F32)
        xs_g = xc_ref[:, rows]
        w_rows = []
        cd_rows = []
        for jh in range(HPG):
            hh = g * HPG + jh
            seg = a_cum[:, hh:hh + 1] - a_cum_t[hh:hh + 1, :]
            decay = jnp.exp(jnp.where(causal, seg, -jnp.inf))
            mh = cb * decay * dt_t[hh:hh + 1, :]
            xs_h = xs_g[:, jh * HEAD_DIM:(jh + 1) * HEAD_DIM]
            y_diag = jnp.dot(mh.astype(BF16), xs_h.astype(BF16), preferred_element_type=F32)
            y_h = y_diag + y_off[:, jh * HEAD_DIM:(jh + 1) * HEAD_DIM] * exp_a_cum[:, hh:hh + 1]
            ysc_ref[:, hh * HEAD_DIM:(hh + 1) * HEAD_DIM] = y_h
            w_rows.append(jnp.broadcast_to(w_t[hh:hh + 1, :], (HEAD_DIM, lc)))
            cd_rows.append(jnp.broadcast_to(chunk_decay_t[hh:hh + 1, :], (HEAD_DIM, D_STATE)))
        xw_t = xs_g.T * jnp.concatenate(w_rows, axis=0)
        s_g = jnp.dot(xw_t.astype(BF16), bg, preferred_element_type=F32)
        hst_ref[rows, :] = h_prev * jnp.concatenate(cd_rows, axis=0) + s_g

    y = ysc_ref[...] + xc_ref[:, 0:D_SSM] * dexp_ref[...]
    y = y * _silu(z_ref[...].astype(F32))
    for g in range(GROUPS):
        cs = slice(g * GROUP_W, (g + 1) * GROUP_W)
        blk = y[:, cs]
        ms = jnp.mean(blk * blk, axis=-1, keepdims=True)
        y_ref[:, cs] = (blk * lax.rsqrt(ms + EPS) * gnw_ref[:, cs]).astype(y_ref.dtype)

    @pl.when(c == nc - 1)
    def _():
        hout_ref[0] = hst_ref[...]
        cso_ref[0] = xpad_ref[lc + 5:lc + 8, :]


def _ssd(z, xd, conv_state, h0, conv_w, conv_b, dtb, alr, alc, dexp, gnw, out_dtype,
         *, nb, nc, lc):
    t = z.shape[0]
    tok = lambda width, blk=0: pl.BlockSpec((lc, width), lambda b, c: (b * nc + c, blk))
    par = lambda r, width: pl.BlockSpec((r, width), lambda b, c: (0, 0))
    cs_spec = pl.BlockSpec((1, CONV_K - 1, CONV_DIM), lambda b, c: (b, 0, 0))
    h_spec = pl.BlockSpec((1, D_SSM, D_STATE), lambda b, c: (b, 0, 0))
    return pl.pallas_call(
        functools.partial(_ssd_body, lc=lc, nc=nc),
        grid=(nb, nc),
        in_specs=[tok(D_SSM), tok(CONV_DIM), tok(LANES, DT_BLOCK), cs_spec, h_spec,
                  par(CONV_K, CONV_DIM), par(1, CONV_DIM), par(1, LANES), par(1, LANES),
                  par(LANES, 1), par(1, D_SSM), par(1, D_SSM)],
        out_specs=[tok(D_SSM), h_spec, cs_spec],
        out_shape=[jax.ShapeDtypeStruct((t, D_SSM), out_dtype),
                   jax.ShapeDtypeStruct((nb, D_SSM, D_STATE), F32),
                   jax.ShapeDtypeStruct((nb, CONV_K - 1, CONV_DIM), F32)],
        scratch_shapes=[pltpu.VMEM((D_SSM, D_STATE), F32),
                        pltpu.VMEM((SUBLANES + lc, CONV_DIM), F32),
                        pltpu.VMEM((lc, CONV_DIM), F32),
                        pltpu.VMEM((lc, D_SSM), F32)],
        compiler_params=pltpu.CompilerParams(
            dimension_semantics=("arbitrary", "arbitrary"), vmem_limit_bytes=VMEM_LIMIT),
        name="ssd_scan",
    )(z, xd, xd, conv_state, h0, conv_w, conv_b, dtb, alr, alc, dexp, gnw)


def _out_body(ys_ref, yc_ref, x_ref, w1_ref, w2_ref, fnw_ref, o_ref):
    acc = jnp.dot(ys_ref[...].astype(BF16), w1_ref[...], preferred_element_type=F32)
    acc = acc + jnp.dot(yc_ref[...].astype(BF16), w2_ref[...], preferred_element_type=F32)
    r = x_ref[...] + acc
    var = jnp.mean(r * r, axis=-1, keepdims=True)
    o_ref[...] = r * lax.rsqrt(var + EPS) * fnw_ref[...]


def _out(ys, yc, x, w_o, fnw, *, tm):
    t = x.shape[0]
    row = lambda width: pl.BlockSpec((tm, width), lambda m: (m, 0))
    return pl.pallas_call(
        _out_body,
        grid=(t // tm,),
        in_specs=[row(D_SSM), row(D_SHORT), row(D_MODEL),
                  pl.BlockSpec((D_SSM, D_MODEL), lambda m: (0, 0)),
                  pl.BlockSpec((D_SHORT, D_MODEL), lambda m: (1, 0)),
                  pl.BlockSpec((1, D_MODEL), lambda m: (0, 0))],
        out_specs=row(D_MODEL),
        out_shape=jax.ShapeDtypeStruct((t, D_MODEL), F32),
        compiler_params=pltpu.CompilerParams(
            dimension_semantics=("arbitrary",), vmem_limit_bytes=VMEM_LIMIT),
        name="out_proj",
    )(ys, yc, x, w_o, w_o, fnw)


def _mixer(x2d, h0, conv_state, short_state, p, *, nb, seqlen, act_dtype, tm, short_nseq, short_tps):
    lc = min(SSD_CHUNK, seqlen)
    nc = seqlen // lc
    h, z, xd = _proj(x2d, p["nw"], p["w_main"], act_dtype, tm=tm)
    y_c, short_new = _short(h, p["w_sh"], p["conv_short_w"], short_state, act_dtype,
                            tm=tm, tn=256, nseq=short_nseq, tps=short_tps)
    y_s, h_new, conv_new = _ssd(z, xd, conv_state, h0, p["conv_w"], p["conv_b"], p["dtb"],
                                p["alr"], p["alc"], p["dexp"], p["gnw"], act_dtype,
                                nb=nb, nc=nc, lc=lc)
    return y_s, y_c, h_new, conv_new, short_new


def kernel(x_prompt, x_sample, state_ssm, state_conv_ssd, state_conv_short, norm_w, w_in, conv_ssd_w,
           conv_ssd_b, dt_bias, a_log, d_skip, ssd_norm_w, conv_short_w, w_out, final_norm_w):
    depth = norm_w.shape[0]
    assert depth == 1, "the output projection fuses the final rmsnorm, valid for a single layer"
    bp, lp, _ = x_prompt.shape
    bs, ls, _ = x_sample.shape
    hp = x_prompt.reshape(bp * lp, D_MODEL)
    hs = x_sample.reshape(bs * ls, D_MODEL)
    pad_h = LANES - HEADS
    outs = [[] for _ in range(6)]
    for layer in range(depth):
        w_t = w_in[layer].T
        al = jnp.pad(a_log[layer], (0, pad_h))
        p = dict(
            nw=norm_w[layer][None, :],
            w_main=_wprep(w_t, start=0, n_out=MAIN_W, valid=MAIN_VALID),
            w_sh=_wprep(w_t, start=MAIN_VALID, n_out=4 * D_SHORT, valid=4 * D_SHORT),
            conv_short_w=conv_short_w[layer],
            conv_w=conv_ssd_w[layer],
            conv_b=conv_ssd_b[layer][None, :],
            dtb=jnp.pad(dt_bias[layer], (0, pad_h))[None, :],
            alr=al[None, :],
            alc=al[:, None],
            dexp=jnp.repeat(d_skip[layer], HEAD_DIM)[None, :],
            gnw=ssd_norm_w[layer][None, :],
        )
        w_o = w_out[layer].astype(BF16)

        tm_p = 1024
        ys, yc, a1, a2, a3 = _mixer(
            hp, jnp.zeros((bp, D_SSM, D_STATE), F32), jnp.zeros((bp, CONV_K - 1, CONV_DIM), F32),
            jnp.zeros((bp, SHORT_K - 1, D_SHORT), F32), p,
            nb=bp, seqlen=lp, act_dtype=BF16, tm=tm_p, short_nseq=1, short_tps=lp // tm_p)
        tm_s = bs * ls
        ys2, yc2, s1, s2, s3 = _mixer(
            hs, state_ssm[layer].reshape(bs, D_SSM, D_STATE), state_conv_ssd[layer],
            state_conv_short[layer], p,
            nb=bs, seqlen=ls, act_dtype=F32, tm=tm_s, short_nseq=bs, short_tps=1)
        fw = final_norm_w[None, :]
        hp = _out(ys, yc, hp, w_o, fw, tm=256)
        hs = _out(ys2, yc2, hs, w_o, fw, tm=256)
        for lst, val in zip(outs, (a1.reshape(bp, HEADS, HEAD_DIM, D_STATE), a2, a3,
                                   s1.reshape(bs, HEADS, HEAD_DIM, D_STATE), s2, s3)):
            lst.append(val)
    return (hp.reshape(bp, lp, D_MODEL), hs.reshape(bs, ls, D_MODEL),
            *(jnp.stack(v) for v in outs))
```

```python
import functools

import jax
import jax.numpy as jnp
from jax import lax
from jax.experimental import pallas as pl
from jax.experimental.pallas import tpu as pltpu

F32 = jnp.float32
BF16 = jnp.bfloat16

D_MODEL = 2048
D_SSM = 2048
D_SHORT = 2048
HEADS = 32
HEAD_DIM = 64
D_STATE = 128
GROUPS = 4
HPG = HEADS // GROUPS
GROUP_W = HPG * HEAD_DIM
GN = GROUPS * D_STATE
CONV_DIM = D_SSM + 2 * GN
CONV_K = 4
SHORT_K = 3
SSD_CHUNK = 128
EPS = 1e-6
LANES = 128
SUBLANES = 8
VMEM_LIMIT = 56 * 1024 * 1024

TN = 512
Z_TILES = D_SSM // TN
XC_TILES = CONV_DIM // TN
MAIN_VALID = D_SSM + CONV_DIM + HEADS
MAIN_W = D_SSM + CONV_DIM + TN
DT_BLOCK = (D_SSM + CONV_DIM) // LANES

NT_DIMS = (((1,), (1,)), ((), ()))
LOG2E = 1.4426950408889634


def _silu(x):
    return x * (1.0 / (1.0 + jnp.exp(-x)))


def _softplus(x):
    return jnp.maximum(x, 0.0) + jnp.log1p(jnp.exp(-jnp.abs(x)))


def _split3(x):
    hi = x.astype(BF16)
    r = x - hi.astype(F32)
    mid = r.astype(BF16)
    lo = (r - mid.astype(F32)).astype(BF16)
    return hi, mid, lo


def _wprep_body(*refs, shift, valid):
    if shift:
        a_ref, b_ref, o_ref = refs
        w = jnp.concatenate([a_ref[shift:, :], b_ref[...]], axis=0)
    else:
        a_ref, o_ref = refs
        w = a_ref[...]
    row = pl.program_id(0) * TN + lax.broadcasted_iota(jnp.int32, w.shape, 0)
    o_ref[...] = jnp.where(row < valid, w, 0.0).T.astype(BF16)


def _wprep(w_t, *, start, n_out, valid):
    base = (start // TN) * TN
    shift = start - base
    assert shift % SUBLANES == 0 and n_out % TN == 0
    in_specs = [pl.BlockSpec((TN, D_MODEL), lambda j: (base // TN + j, 0))]
    operands = [w_t]
    if shift:
        assert TN % shift == 0 and base % shift == 0
        in_specs.append(pl.BlockSpec((shift, D_MODEL), lambda j: ((base + TN * (j + 1)) // shift, 0)))
        operands.append(w_t)
    return pl.pallas_call(
        functools.partial(_wprep_body, shift=shift, valid=valid),
        grid=(n_out // TN,),
        in_specs=in_specs,
        out_specs=pl.BlockSpec((D_MODEL, TN), lambda j: (0, j)),
        out_shape=jax.ShapeDtypeStruct((D_MODEL, n_out), BF16),
        compiler_params=pltpu.CompilerParams(dimension_semantics=("arbitrary",)),
        name="wprep",
    )(*operands)


def _proj_z_body(x_ref, nw_ref, w_ref, wdt_ref, dtb_ref, h_ref, zg_ref, dt_ref, *, tm):
    @pl.when(pl.program_id(1) == 0)
    def _():
        blk = 32
        nw = nw_ref[...]

        def body(i, carry):
            r = pl.ds(pl.multiple_of(i * blk, blk), blk)
            xf = x_ref[r, :]
            var = jnp.mean(xf * xf, axis=-1, keepdims=True)
            h_ref[r, :] = (xf * lax.rsqrt(var + EPS) * nw).astype(BF16)
            return carry

        lax.fori_loop(0, tm // blk, body, 0, unroll=2)
        dt_raw = jnp.dot(h_ref[...], wdt_ref[...], preferred_element_type=F32)
        dt_ref[...] = _softplus(dt_raw + dtb_ref[...])

    acc = jnp.dot(h_ref[...], w_ref[...], preferred_element_type=F32)
    zg_ref[...] = _silu(acc).astype(zg_ref.dtype)


def _proj_z(x, nw, w_main, dtb, zg_dtype, *, tm):
    t = x.shape[0]
    return pl.pallas_call(
        functools.partial(_proj_z_body, tm=tm),
        grid=(t // tm, Z_TILES),
        in_specs=[
            pl.BlockSpec((tm, D_MODEL), lambda m, j: (m, 0)),
            pl.BlockSpec((1, D_MODEL), lambda m, j: (0, 0)),
            pl.BlockSpec((D_MODEL, TN), lambda m, j: (0, j)),
            pl.BlockSpec((D_MODEL, LANES), lambda m, j: (0, DT_BLOCK)),
            pl.BlockSpec((1, LANES), lambda m, j: (0, 0)),
        ],
        out_specs=[pl.BlockSpec((tm, D_MODEL), lambda m, j: (m, 0)),
                   pl.BlockSpec((tm, TN), lambda m, j: (m, j)),
                   pl.BlockSpec((tm, LANES), lambda m, j: (m, 0))],
        out_shape=[jax.ShapeDtypeStruct((t, D_MODEL), BF16),
                   jax.ShapeDtypeStruct((t, D_SSM), zg_dtype),
                   jax.ShapeDtypeStruct((t, LANES), F32)],
        compiler_params=pltpu.CompilerParams(
            dimension_semantics=("arbitrary", "arbitrary"), vmem_limit_bytes=VMEM_LIMIT),
        name="in_proj_z",
    )(x, nw, w_main, w_main, dtb)


def _load_history(pad_ref, st_ref, carry, j, *, hist, tps):
    def from_state():
        pad_ref[:, 0:SUBLANES, :] = jnp.zeros((pad_ref.shape[0], SUBLANES, pad_ref.shape[2]), F32)
        pad_ref[:, SUBLANES - hist:SUBLANES, :] = st_ref[...]

    if tps > 1:
        (carry_ref,) = carry
        first = (pl.program_id(0) % tps) == 0
        pl.when(first)(from_state)

        @pl.when(jnp.logical_not(first))
        def _():
            pad_ref[:, 0:SUBLANES, :] = carry_ref[j]
    else:
        from_state()


def _causal_conv(pad_ref, cw_ref, taps):
    xe = pad_ref[...]
    conv = xe[:, SUBLANES:, :] * cw_ref[taps - 1:taps, :]
    for s in range(1, taps):
        shifted = pltpu.roll(xe, s, axis=1)[:, SUBLANES:, :]
        conv = conv + shifted * cw_ref[taps - 1 - s:taps - s, :]
    return conv


def _proj_xc_body(h_ref, w_ref, cw_ref, cb_ref, st_ref, xc_ref, tails_ref, pad_ref, *carry,
                  tm, nseq, tps):
    j = pl.program_id(1)
    lt = tm // nseq
    _load_history(pad_ref, st_ref, carry, j, hist=CONV_K - 1, tps=tps)
    acc = jnp.dot(h_ref[...], w_ref[...], preferred_element_type=F32)
    pad_ref[:, SUBLANES:SUBLANES + lt, :] = acc.reshape(nseq, lt, TN)
    conv = _causal_conv(pad_ref, cw_ref, CONV_K)
    xc_ref[...] = _silu(conv + cb_ref[...]).reshape(tm, TN)
    tails_ref[...] = pad_ref[:, lt + SUBLANES - (CONV_K - 1):lt + SUBLANES, :]
    if tps > 1:
        carry[0][j] = pad_ref[:, lt:lt + SUBLANES, :]


def _proj_xc(h, w_main, conv_w, conv_b, conv_state, *, tm, nseq, tps):
    t = h.shape[0]
    lt = tm // nseq
    scratch = [pltpu.VMEM((nseq, SUBLANES + lt, TN), F32)]
    if tps > 1:
        scratch.append(pltpu.VMEM((XC_TILES, nseq, SUBLANES, TN), F32))
    xc, tails = pl.pallas_call(
        functools.partial(_proj_xc_body, tm=tm, nseq=nseq, tps=tps),
        grid=(t // tm, XC_TILES),
        in_specs=[
            pl.BlockSpec((tm, D_MODEL), lambda m, j: (m, 0)),
            pl.BlockSpec((D_MODEL, TN), lambda m, j: (0, Z_TILES + j)),
            pl.BlockSpec((CONV_K, TN), lambda m, j: (0, j)),
            pl.BlockSpec((1, TN), lambda m, j: (0, j)),
            pl.BlockSpec((nseq, CONV_K - 1, TN), lambda m, j: (m // tps, 0, j)),
        ],
        out_specs=[pl.BlockSpec((tm, TN), lambda m, j: (m, j)),
                   pl.BlockSpec((nseq, CONV_K - 1, TN), lambda m, j: (m, 0, j))],
        out_shape=[jax.ShapeDtypeStruct((t, CONV_DIM), F32),
                   jax.ShapeDtypeStruct(((t // tm) * nseq, CONV_K - 1, CONV_DIM), F32)],
        scratch_shapes=scratch,
        compiler_params=pltpu.CompilerParams(
            dimension_semantics=("arbitrary", "arbitrary"), vmem_limit_bytes=VMEM_LIMIT),
        name="in_proj_xc",
    )(h, w_main, conv_w, conv_b, conv_state)
    tails = tails.reshape(t // (tm * tps), tps, nseq, CONV_K - 1, CONV_DIM)[:, -1]
    return xc, tails.reshape(conv_state.shape)


def _short_body(h_ref, wz_ref, wb_ref, wc_ref, wv_ref, cw_ref, st_ref,
                y_ref, ns_ref, pad_ref, *carry, tm, tn, nseq, tps):
    j = pl.program_id(1)
    lt = tm // nseq
    _load_history(pad_ref, st_ref, carry, j, hist=SHORT_K - 1, tps=tps)
    h = h_ref[...]
    c = jnp.dot(h, wc_ref[...], preferred_element_type=F32)
    v = jnp.dot(h, wv_ref[...], preferred_element_type=F32)
    pad_ref[:, SUBLANES:SUBLANES + lt, :] = (c * v).reshape(nseq, lt, tn)
    conv = _causal_conv(pad_ref, cw_ref, SHORT_K)
    b = jnp.dot(h, wb_ref[...], preferred_element_type=F32)
    z = jnp.dot(h, wz_ref[...], preferred_element_type=F32)
    y = b * conv.reshape(tm, tn) * _silu(z)
    y_ref[...] = y.astype(y_ref.dtype)
    ns_ref[...] = pad_ref[:, lt + SUBLANES - (SHORT_K - 1):lt + SUBLANES, :]
    if tps > 1:
        carry[0][j] = pad_ref[:, lt:lt + SUBLANES, :]


def _short(h, w_sh, conv_w, state, out_dtype, *, tm, tn, nseq, tps):
    t = h.shape[0]
    nj = D_SHORT // tn
    lt = tm // nseq
    scratch = [pltpu.VMEM((nseq, SUBLANES + lt, tn), F32)]
    if tps > 1:
        scratch.append(pltpu.VMEM((nj, nseq, SUBLANES, tn), F32))

    def wspec(k):
        return pl.BlockSpec((D_MODEL, tn), lambda m, j, k=k: (0, k * nj + j))

    y, tails = pl.pallas_call(
        functools.partial(_short_body, tm=tm, tn=tn, nseq=nseq, tps=tps),
        grid=(t // tm, nj),
        in_specs=[
            pl.BlockSpec((tm, D_MODEL), lambda m, j: (m, 0)),
            wspec(0), wspec(1), wspec(2), wspec(3),
            pl.BlockSpec((SHORT_K, tn), lambda m, j: (0, j)),
            pl.BlockSpec((nseq, SHORT_K - 1, tn), lambda m, j: (m // tps, 0, j)),
        ],
        out_specs=[pl.BlockSpec((tm, tn), lambda m, j: (m, j)),
                   pl.BlockSpec((nseq, SHORT_K - 1, tn), lambda m, j: (m, 0, j))],
        out_shape=[jax.ShapeDtypeStruct((t, D_SHORT), out_dtype),
                   jax.ShapeDtypeStruct(((t // tm) * nseq, SHORT_K - 1, D_SHORT), F32)],
        scratch_shapes=scratch,
        compiler_params=pltpu.CompilerParams(
            dimension_semantics=("arbitrary", "arbitrary"), vmem_limit_bytes=VMEM_LIMIT),
        name="short_conv",
    )(h, w_sh, w_sh, w_sh, w_sh, conv_w, state)
    tails = tails.reshape(t // (tm * tps), tps, nseq, SHORT_K - 1, D_SHORT)[:, -1]
    return y, tails.reshape(state.shape)


def _dot_split(x, w, pieces):
    out = None
    r = x
    for _ in range(pieces):
        p = r.astype(BF16)
        d = jnp.dot(p, w, preferred_element_type=F32)
        out = d if out is None else out + d
        r = r - p.astype(F32)
    return out


def _cumsum_rows(x, causal):
    tri = causal.astype(BF16)
    return sum(jnp.dot(tri, p, preferred_element_type=F32) for p in _split3(x))


def _gated_norm_store(y, zg_ref, gnw_ref, y_ref):
    y = y * zg_ref[...].astype(F32)
    for g in range(GROUPS):
        cs = slice(g * GROUP_W, (g + 1) * GROUP_W)
        blk = y[:, cs]
        ms = jnp.mean(blk * blk, axis=-1, keepdims=True)
        y_ref[:, cs] = (blk * lax.rsqrt(ms + EPS) * gnw_ref[:, cs]).astype(y_ref.dtype)


def _ssd_chunk_body(zg_ref, xc_ref, dt_ref, h0_ref, alr_ref, expand_ref, dexp_ref, gnw_ref,
                    y_ref, hout_ref, hst_ref, ysc_ref, *, lc, nc):
    c = pl.program_id(1)

    @pl.when(c == 0)
    def _():
        hst_ref[...] = h0_ref[0].T

    row_i = lax.broadcasted_iota(jnp.int32, (lc, lc), 0)
    col_i = lax.broadcasted_iota(jnp.int32, (lc, lc), 1)
    causal = row_i >= col_i
    lane_lo = lax.broadcasted_iota(jnp.int32, (lc, LANES), 1) < HEAD_DIM

    dt = dt_ref[...]
    la = dt * (-jnp.exp(alr_ref[...]))
    a_cum = _cumsum_rows(la, causal)
    a_last = a_cum[lc - 1:lc, :]
    expand = expand_ref[...]
    e_exp = _dot_split(jnp.exp(a_cum), expand, 2)
    w_exp = _dot_split(jnp.exp(a_last - a_cum) * dt, expand, 2)
    a2 = a_cum * LOG2E
    r_t = (a2 - jnp.log2(dt)).T

    for g in range(GROUPS):
        bg = xc_ref[:, D_SSM + g * D_STATE:D_SSM + (g + 1) * D_STATE]
        cg = xc_ref[:, D_SSM + GN + g * D_STATE:D_SSM + GN + (g + 1) * D_STATE].astype(BF16)
        cols = slice(g * GROUP_W, (g + 1) * GROUP_W)
        h_prev = hst_ref[:, cols]
        cb = lax.dot_general(cg, bg.astype(BF16), NT_DIMS, preferred_element_type=F32)
        y_off = jnp.dot(cg, h_prev.astype(BF16), preferred_element_type=F32)
        xs_g = xc_ref[:, cols]
        for pr in range(HPG // 2):
            mats = []
            for hh in (g * HPG + 2 * pr, g * HPG + 2 * pr + 1):
                seg = a2[:, hh:hh + 1] - r_t[hh:hh + 1, :]
                dec = jnp.exp2(jnp.where(causal, seg, -jnp.inf))
                mats.append((cb * dec).astype(BF16))
            xp = xs_g[:, pr * LANES:(pr + 1) * LANES].astype(BF16)
            rhs = jnp.concatenate([jnp.where(lane_lo, xp, 0), jnp.where(lane_lo, 0, xp)], axis=0)
            y_diag = jnp.dot(jnp.concatenate(mats, axis=1), rhs, preferred_element_type=F32)
            pc = slice(g * GROUP_W + pr * LANES, g * GROUP_W + (pr + 1) * LANES)
            ysc_ref[:, pc] = y_diag + y_off[:, pr * LANES:(pr + 1) * LANES] * e_exp[:, pc]
        xw = (xs_g * w_exp[:, cols]).astype(BF16)
        s_t = jnp.dot(bg.T.astype(BF16), xw, preferred_element_type=F32)
        hst_ref[:, cols] = h_prev * e_exp[lc - 1:lc, cols] + s_t

    y = ysc_ref[...] + xc_ref[:, 0:D_SSM] * dexp_ref[...]
    _gated_norm_store(y, zg_ref, gnw_ref, y_ref)

    @pl.when(c == nc - 1)
    def _():
        hout_ref[0] = hst_ref[...].T


def _ssd_short_body(zg_ref, xc_ref, dt_ref, h0_ref, alr_ref, expand_ref, dexp_ref, gnw_ref,
                    y_ref, hout_ref, hst_ref, ysc_ref, *, lc, nc):
    del expand_ref, hst_ref
    assert nc == 1
    row_i = lax.broadcasted_iota(jnp.int32, (lc, lc), 0)
    col_i = lax.broadcasted_iota(jnp.int32, (lc, lc), 1)
    causal = row_i >= col_i

    dt = dt_ref[...]
    la = dt * (-jnp.exp(alr_ref[...]))
    a_cum = _cumsum_rows(la, causal)
    a_cum_t = a_cum.T
    dt_t = dt.T
    a_last_t = a_cum_t[:, lc - 1:lc]
    w_t = jnp.exp(a_last_t - a_cum_t) * dt_t
    chunk_decay_t = jnp.exp(a_last_t)
    exp_a_cum = jnp.exp(a_cum)

    for g in range(GROUPS):
        bg = xc_ref[:, D_SSM + g * D_STATE:D_SSM + (g + 1) * D_STATE].astype(BF16)
        cg = xc_ref[:, D_SSM + GN + g * D_STATE:D_SSM + GN + (g + 1) * D_STATE].astype(BF16)
        rows = slice(g * GROUP_W, (g + 1) * GROUP_W)
        h_prev = h0_ref[0, rows, :]
        cb = lax.dot_general(cg, bg, NT_DIMS, preferred_element_type=F32)
        y_off = lax.dot_general(cg, h_prev.astype(BF16), NT_DIMS, preferred_element_type=F32)
        xs_g = xc_ref[:, rows]
        w_rows = []
        cd_rows = []
        for jh in range(HPG):
            hh = g * HPG + jh
            seg = a_cum[:, hh:hh + 1] - a_cum_t[hh:hh + 1, :]
            decay = jnp.exp(jnp.where(causal, seg, -jnp.inf))
            mh = cb * decay * dt_t[hh:hh + 1, :]
            xs_h = xs_g[:, jh * HEAD_DIM:(jh + 1) * HEAD_DIM]
            y_diag = jnp.dot(mh.astype(BF16), xs_h.astype(BF16), preferred_element_type=F32)
            y_h = y_diag + y_off[:, jh * HEAD_DIM:(jh + 1) * HEAD_DIM] * exp_a_cum[:, hh:hh + 1]
            ysc_ref[:, hh * HEAD_DIM:(hh + 1) * HEAD_DIM] = y_h
            w_rows.append(jnp.broadcast_to(w_t[hh:hh + 1, :], (HEAD_DIM, lc)))
            cd_rows.append(jnp.broadcast_to(chunk_decay_t[hh:hh + 1, :], (HEAD_DIM, D_STATE)))
        xw_t = xs_g.T * jnp.concatenate(w_rows, axis=0)
        s_g = jnp.dot(xw_t.astype(BF16), bg, preferred_element_type=F32)
        hout_ref[0, rows, :] = h_prev * jnp.concatenate(cd_rows, axis=0) + s_g

    y = ysc_ref[...] + xc_ref[:, 0:D_SSM] * dexp_ref[...]
    _gated_norm_store(y, zg_ref, gnw_ref, y_ref)


def _ssd(zg, xc, dt, h0, alr, expand, dexp, gnw, out_dtype, *, nb, nc, lc):
    t = zg.shape[0]
    tok = lambda width: pl.BlockSpec((lc, width), lambda b, c: (b * nc + c, 0))
    par = lambda r, width: pl.BlockSpec((r, width), lambda b, c: (0, 0))
    h_spec = pl.BlockSpec((1, D_SSM, D_STATE), lambda b, c: (b, 0, 0))
    if lc == SSD_CHUNK:
        body, hst_shape = _ssd_chunk_body, (D_STATE, D_SSM)
    else:
        body, hst_shape = _ssd_short_body, (SUBLANES, LANES)
    return pl.pallas_call(
        functools.partial(body, lc=lc, nc=nc),
        grid=(nb, nc),
        in_specs=[tok(D_SSM), tok(CONV_DIM), tok(LANES), h_spec,
                  par(1, LANES), par(LANES, D_SSM), par(1, D_SSM), par(1, D_SSM)],
        out_specs=[tok(D_SSM), h_spec],
        out_shape=[jax.ShapeDtypeStruct((t, D_SSM), out_dtype),
                   jax.ShapeDtypeStruct((nb, D_SSM, D_STATE), F32)],
        scratch_shapes=[pltpu.VMEM(hst_shape, F32), pltpu.VMEM((lc, D_SSM), F32)],
        compiler_params=pltpu.CompilerParams(
            dimension_semantics=("arbitrary", "arbitrary"), vmem_limit_bytes=VMEM_LIMIT),
        name="ssd_scan",
    )(zg, xc, dt, h0, alr, expand, dexp, gnw)


def _out_body(ys_ref, yc_ref, x_ref, w1_ref, w2_ref, fnw_ref, o_ref):
    acc = jnp.dot(ys_ref[...].astype(BF16), w1_ref[...], preferred_element_type=F32)
    acc = acc + jnp.dot(yc_ref[...].astype(BF16), w2_ref[...], preferred_element_type=F32)
    r = x_ref[...] + acc
    var = jnp.mean(r * r, axis=-1, keepdims=True)
    o_ref[...] = r * lax.rsqrt(var + EPS) * fnw_ref[...]


def _out(ys, yc, x, w_o, fnw, *, tm):
    t = x.shape[0]
    row = lambda width: pl.BlockSpec((tm, width), lambda m: (m, 0))
    return pl.pallas_call(
        _out_body,
        grid=(t // tm,),
        in_specs=[row(D_SSM), row(D_SHORT), row(D_MODEL),
                  pl.BlockSpec((D_SSM, D_MODEL), lambda m: (0, 0)),
                  pl.BlockSpec((D_SHORT, D_MODEL), lambda m: (1, 0)),
                  pl.BlockSpec((1, D_MODEL), lambda m: (0, 0))],
        out_specs=row(D_MODEL),
        out_shape=jax.ShapeDtypeStruct((t, D_MODEL), F32),
        compiler_params=pltpu.CompilerParams(
            dimension_semantics=("arbitrary",), vmem_limit_bytes=VMEM_LIMIT),
        name="out_proj",
    )(ys, yc, x, w_o, w_o, fnw)


def _mixer(x2d, h0, conv_state, short_state, p, *, nb, seqlen, act_dtype, tm, nseq, tps):
    lc = min(SSD_CHUNK, seqlen)
    nc = seqlen // lc
    h, zg, dt = _proj_z(x2d, p["nw"], p["w_main"], p["dtb"], act_dtype, tm=tm)
    xc, conv_new = _proj_xc(h, p["w_main"], p["conv_w"], p["conv_b"], conv_state,
                            tm=tm, nseq=nseq, tps=tps)
    y_c, short_new = _short(h, p["w_sh"], p["conv_short_w"], short_state, act_dtype,
                            tm=tm, tn=256, nseq=nseq, tps=tps)
    y_s, h_new = _ssd(zg, xc, dt, h0, p["alr"], p["expand"], p["dexp"], p["gnw"], act_dtype,
                      nb=nb, nc=nc, lc=lc)
    return y_s, y_c, h_new, conv_new, short_new


def kernel(x_prompt, x_sample, state_ssm, state_conv_ssd, state_conv_short, norm_w, w_in, conv_ssd_w,
           conv_ssd_b, dt_bias, a_log, d_skip, ssd_norm_w, conv_short_w, w_out, final_norm_w):
    depth = norm_w.shape[0]
    assert depth == 1, "the output projection fuses the final rmsnorm, valid for a single layer"
    bp, lp, _ = x_prompt.shape
    bs, ls, _ = x_sample.shape
    hp = x_prompt.reshape(bp * lp, D_MODEL)
    hs = x_sample.reshape(bs * ls, D_MODEL)
    pad_h = LANES - HEADS
    expand = (jnp.arange(LANES)[:, None] == jnp.arange(D_SSM)[None, :] // HEAD_DIM).astype(BF16)
    outs = [[] for _ in range(6)]
    for layer in range(depth):
        w_t = w_in[layer].T
        p = dict(
            nw=norm_w[layer][None, :],
            w_main=_wprep(w_t, start=0, n_out=MAIN_W, valid=MAIN_VALID),
            w_sh=_wprep(w_t, start=MAIN_VALID, n_out=4 * D_SHORT, valid=4 * D_SHORT),
            conv_short_w=conv_short_w[layer],
            conv_w=conv_ssd_w[layer],
            conv_b=conv_ssd_b[layer][None, :],
            dtb=jnp.pad(dt_bias[layer], (0, pad_h))[None, :],
            alr=jnp.pad(a_log[layer], (0, pad_h))[None, :],
            expand=expand,
            dexp=jnp.repeat(d_skip[layer], HEAD_DIM)[None, :],
            gnw=ssd_norm_w[layer][None, :],
        )
        w_o = w_out[layer].astype(BF16)

        tm_p = 1024
        ys, yc, a1, a2, a3 = _mixer(
            hp, jnp.zeros((bp, D_SSM, D_STATE), F32), jnp.zeros((bp, CONV_K - 1, CONV_DIM), F32),
            jnp.zeros((bp, SHORT_K - 1, D_SHORT), F32), p,
            nb=bp, seqlen=lp, act_dtype=BF16, tm=tm_p, nseq=1, tps=lp // tm_p)
        ys2, yc2, s1, s2, s3 = _mixer(
            hs, state_ssm[layer].reshape(bs, D_SSM, D_STATE), state_conv_ssd[layer],
            state_conv_short[layer], p,
            nb=bs, seqlen=ls, act_dtype=F32, tm=bs * ls, nseq=bs, tps=1)
        fw = final_norm_w[None, :]
        hp = _out(ys, yc, hp, w_o, fw, tm=256)
        hs = _out(ys2, yc2, hs, w_o, fw, tm=256)
        for lst, val in zip(outs, (a1.reshape(bp, HEADS, HEAD_DIM, D_STATE), a2, a3,
                                   s1.reshape(bs, HEADS, HEAD_DIM, D_STATE), s2, s3)):
            lst.append(val)
    return (hp.reshape(bp, lp, D_MODEL), hs.reshape(bs, ls, D_MODEL),
            *(jnp.stack(v) for v in outs))
```

```python
import functools

import jax
import jax.numpy as jnp
from jax import lax
from jax.experimental import pallas as pl
from jax.experimental.pallas import tpu as pltpu

F32 = jnp.float32
BF16 = jnp.bfloat16

D_MODEL = 2048
D_SSM = 2048
D_SHORT = 2048
HEADS = 32
HEAD_DIM = 64
D_STATE = 128
GROUPS = 4
HPG = HEADS // GROUPS
GROUP_W = HPG * HEAD_DIM
GN = GROUPS * D_STATE
CONV_DIM = D_SSM + 2 * GN
CONV_K = 4
SHORT_K = 3
SSD_CHUNK = 128
EPS = 1e-6
LANES = 128
SUBLANES = 8
VMEM_LIMIT = 56 * 1024 * 1024

TN = 512
Z_TILES = D_SSM // TN
XC_TILES = CONV_DIM // TN
MAIN_VALID = D_SSM + CONV_DIM + HEADS
MAIN_W = D_SSM + CONV_DIM + TN
DT_BLOCK = (D_SSM + CONV_DIM) // LANES

NT_DIMS = (((1,), (1,)), ((), ()))
LOG2E = 1.4426950408889634
SHORT_SEQS_PER_STEP = 8


def _silu(x):
    return x * (1.0 / (1.0 + jnp.exp(-x)))


def _softplus(x):
    return jnp.maximum(x, 0.0) + jnp.log1p(jnp.exp(-jnp.abs(x)))


def _split3(x):
    hi = x.astype(BF16)
    r = x - hi.astype(F32)
    mid = r.astype(BF16)
    lo = (r - mid.astype(F32)).astype(BF16)
    return hi, mid, lo


def _wprep_body(*refs, shift, valid):
    if shift:
        a_ref, b_ref, o_ref = refs
        w = jnp.concatenate([a_ref[shift:, :], b_ref[...]], axis=0)
    else:
        a_ref, o_ref = refs
        w = a_ref[...]
    row = pl.program_id(0) * TN + lax.broadcasted_iota(jnp.int32, w.shape, 0)
    o_ref[...] = jnp.where(row < valid, w, 0.0).T.astype(BF16)


def _wprep(w_t, *, start, n_out, valid):
    base = (start // TN) * TN
    shift = start - base
    assert shift % SUBLANES == 0 and n_out % TN == 0
    in_specs = [pl.BlockSpec((TN, D_MODEL), lambda j: (base // TN + j, 0))]
    operands = [w_t]
    if shift:
        assert TN % shift == 0 and base % shift == 0
        in_specs.append(pl.BlockSpec((shift, D_MODEL), lambda j: ((base + TN * (j + 1)) // shift, 0)))
        operands.append(w_t)
    return pl.pallas_call(
        functools.partial(_wprep_body, shift=shift, valid=valid),
        grid=(n_out // TN,),
        in_specs=in_specs,
        out_specs=pl.BlockSpec((D_MODEL, TN), lambda j: (0, j)),
        out_shape=jax.ShapeDtypeStruct((D_MODEL, n_out), BF16),
        compiler_params=pltpu.CompilerParams(dimension_semantics=("arbitrary",)),
        name="wprep",
    )(*operands)


def _proj_z_body(x_ref, nw_ref, w_ref, wdt_ref, dtb_ref, h_ref, zg_ref, dt_ref, *, tm):
    @pl.when(pl.program_id(1) == 0)
    def _():
        blk = 32
        nw = nw_ref[...]

        def body(i, carry):
            r = pl.ds(pl.multiple_of(i * blk, blk), blk)
            xf = x_ref[r, :]
            var = jnp.mean(xf * xf, axis=-1, keepdims=True)
            h_ref[r, :] = (xf * lax.rsqrt(var + EPS) * nw).astype(BF16)
            return carry

        lax.fori_loop(0, tm // blk, body, 0, unroll=2)
        dt_raw = jnp.dot(h_ref[...], wdt_ref[...], preferred_element_type=F32)
        dt_ref[...] = _softplus(dt_raw + dtb_ref[...])

    acc = jnp.dot(h_ref[...], w_ref[...], preferred_element_type=F32)
    zg_ref[...] = _silu(acc).astype(zg_ref.dtype)


def _proj_z(x, nw, w_main, dtb, zg_dtype, *, tm):
    t = x.shape[0]
    return pl.pallas_call(
        functools.partial(_proj_z_body, tm=tm),
        grid=(t // tm, Z_TILES),
        in_specs=[
            pl.BlockSpec((tm, D_MODEL), lambda m, j: (m, 0)),
            pl.BlockSpec((1, D_MODEL), lambda m, j: (0, 0)),
            pl.BlockSpec((D_MODEL, TN), lambda m, j: (0, j)),
            pl.BlockSpec((D_MODEL, LANES), lambda m, j: (0, DT_BLOCK)),
            pl.BlockSpec((1, LANES), lambda m, j: (0, 0)),
        ],
        out_specs=[pl.BlockSpec((tm, D_MODEL), lambda m, j: (m, 0)),
                   pl.BlockSpec((tm, TN), lambda m, j: (m, j)),
                   pl.BlockSpec((tm, LANES), lambda m, j: (m, 0))],
        out_shape=[jax.ShapeDtypeStruct((t, D_MODEL), BF16),
                   jax.ShapeDtypeStruct((t, D_SSM), zg_dtype),
                   jax.ShapeDtypeStruct((t, LANES), F32)],
        compiler_params=pltpu.CompilerParams(
            dimension_semantics=("arbitrary", "arbitrary"), vmem_limit_bytes=VMEM_LIMIT),
        name="in_proj_z",
    )(x, nw, w_main, w_main, dtb)


def _load_history(pad_ref, st_ref, carry, j, *, hist, tps):
    def from_state():
        pad_ref[:, 0:SUBLANES, :] = jnp.zeros((pad_ref.shape[0], SUBLANES, pad_ref.shape[2]), F32)
        pad_ref[:, SUBLANES - hist:SUBLANES, :] = st_ref[...]

    if tps > 1:
        (carry_ref,) = carry
        first = (pl.program_id(0) % tps) == 0
        pl.when(first)(from_state)

        @pl.when(jnp.logical_not(first))
        def _():
            pad_ref[:, 0:SUBLANES, :] = carry_ref[j]
    else:
        from_state()


def _causal_conv(pad_ref, cw_ref, taps):
    xe = pad_ref[...]
    conv = xe[:, SUBLANES:, :] * cw_ref[taps - 1:taps, :]
    for s in range(1, taps):
        shifted = pltpu.roll(xe, s, axis=1)[:, SUBLANES:, :]
        conv = conv + shifted * cw_ref[taps - 1 - s:taps - s, :]
    return conv


def _proj_xc_body(h_ref, w_ref, cw_ref, cb_ref, st_ref, xc_ref, tails_ref, pad_ref, *carry,
                  tm, nseq, tps):
    j = pl.program_id(1)
    lt = tm // nseq
    _load_history(pad_ref, st_ref, carry, j, hist=CONV_K - 1, tps=tps)
    acc = jnp.dot(h_ref[...], w_ref[...], preferred_element_type=F32)
    pad_ref[:, SUBLANES:SUBLANES + lt, :] = acc.reshape(nseq, lt, TN)
    conv = _causal_conv(pad_ref, cw_ref, CONV_K)
    xc_ref[...] = _silu(conv + cb_ref[...]).reshape(tm, TN)
    tails_ref[...] = pad_ref[:, lt + SUBLANES - (CONV_K - 1):lt + SUBLANES, :]
    if tps > 1:
        carry[0][j] = pad_ref[:, lt:lt + SUBLANES, :]


def _proj_xc(h, w_main, conv_w, conv_b, conv_state, *, tm, nseq, tps):
    t = h.shape[0]
    lt = tm // nseq
    scratch = [pltpu.VMEM((nseq, SUBLANES + lt, TN), F32)]
    if tps > 1:
        scratch.append(pltpu.VMEM((XC_TILES, nseq, SUBLANES, TN), F32))
    xc, tails = pl.pallas_call(
        functools.partial(_proj_xc_body, tm=tm, nseq=nseq, tps=tps),
        grid=(t // tm, XC_TILES),
        in_specs=[
            pl.BlockSpec((tm, D_MODEL), lambda m, j: (m, 0)),
            pl.BlockSpec((D_MODEL, TN), lambda m, j: (0, Z_TILES + j)),
            pl.BlockSpec((CONV_K, TN), lambda m, j: (0, j)),
            pl.BlockSpec((1, TN), lambda m, j: (0, j)),
            pl.BlockSpec((nseq, CONV_K - 1, TN), lambda m, j: (m // tps, 0, j)),
        ],
        out_specs=[pl.BlockSpec((tm, TN), lambda m, j: (m, j)),
                   pl.BlockSpec((nseq, CONV_K - 1, TN), lambda m, j: (m, 0, j))],
        out_shape=[jax.ShapeDtypeStruct((t, CONV_DIM), F32),
                   jax.ShapeDtypeStruct(((t // tm) * nseq, CONV_K - 1, CONV_DIM), F32)],
        scratch_shapes=scratch,
        compiler_params=pltpu.CompilerParams(
            dimension_semantics=("arbitrary", "arbitrary"), vmem_limit_bytes=VMEM_LIMIT),
        name="in_proj_xc",
    )(h, w_main, conv_w, conv_b, conv_state)
    tails = tails.reshape(t // (tm * tps), tps, nseq, CONV_K - 1, CONV_DIM)[:, -1]
    return xc, tails.reshape(conv_state.shape)


def _short_body(h_ref, wz_ref, wb_ref, wc_ref, wv_ref, cw_ref, st_ref,
                y_ref, ns_ref, pad_ref, *carry, tm, tn, nseq, tps):
    j = pl.program_id(1)
    lt = tm // nseq
    _load_history(pad_ref, st_ref, carry, j, hist=SHORT_K - 1, tps=tps)
    h = h_ref[...]
    c = jnp.dot(h, wc_ref[...], preferred_element_type=F32)
    v = jnp.dot(h, wv_ref[...], preferred_element_type=F32)
    pad_ref[:, SUBLANES:SUBLANES + lt, :] = (c * v).reshape(nseq, lt, tn)
    conv = _causal_conv(pad_ref, cw_ref, SHORT_K)
    b = jnp.dot(h, wb_ref[...], preferred_element_type=F32)
    z = jnp.dot(h, wz_ref[...], preferred_element_type=F32)
    y = b * conv.reshape(tm, tn) * _silu(z)
    y_ref[...] = y.astype(y_ref.dtype)
    ns_ref[...] = pad_ref[:, lt + SUBLANES - (SHORT_K - 1):lt + SUBLANES, :]
    if tps > 1:
        carry[0][j] = pad_ref[:, lt:lt + SUBLANES, :]


def _short(h, w_sh, conv_w, state, out_dtype, *, tm, tn, nseq, tps):
    t = h.shape[0]
    nj = D_SHORT // tn
    lt = tm // nseq
    scratch = [pltpu.VMEM((nseq, SUBLANES + lt, tn), F32)]
    if tps > 1:
        scratch.append(pltpu.VMEM((nj, nseq, SUBLANES, tn), F32))

    def wspec(k):
        return pl.BlockSpec((D_MODEL, tn), lambda m, j, k=k: (0, k * nj + j))

    y, tails = pl.pallas_call(
        functools.partial(_short_body, tm=tm, tn=tn, nseq=nseq, tps=tps),
        grid=(t // tm, nj),
        in_specs=[
            pl.BlockSpec((tm, D_MODEL), lambda m, j: (m, 0)),
            wspec(0), wspec(1), wspec(2), wspec(3),
            pl.BlockSpec((SHORT_K, tn), lambda m, j: (0, j)),
            pl.BlockSpec((nseq, SHORT_K - 1, tn), lambda m, j: (m // tps, 0, j)),
        ],
        out_specs=[pl.BlockSpec((tm, tn), lambda m, j: (m, j)),
                   pl.BlockSpec((nseq, SHORT_K - 1, tn), lambda m, j: (m, 0, j))],
        out_shape=[jax.ShapeDtypeStruct((t, D_SHORT), out_dtype),
                   jax.ShapeDtypeStruct(((t // tm) * nseq, SHORT_K - 1, D_SHORT), F32)],
        scratch_shapes=scratch,
        compiler_params=pltpu.CompilerParams(
            dimension_semantics=("arbitrary", "arbitrary"), vmem_limit_bytes=VMEM_LIMIT),
        name="short_conv",
    )(h, w_sh, w_sh, w_sh, w_sh, conv_w, state)
    tails = tails.reshape(t // (tm * tps), tps, nseq, SHORT_K - 1, D_SHORT)[:, -1]
    return y, tails.reshape(state.shape)


def _dot_split(x, w, pieces):
    out = None
    r = x
    for _ in range(pieces):
        p = r.astype(BF16)
        d = jnp.dot(p, w, preferred_element_type=F32)
        out = d if out is None else out + d
        r = r - p.astype(F32)
    return out


def _cumsum_rows(x, causal):
    tri = causal.astype(BF16)
    return sum(jnp.dot(tri, p, preferred_element_type=F32) for p in _split3(x))


def _gated_norm_store(y, zg_ref, gnw_ref, y_ref):
    y = y * zg_ref[...].astype(F32)
    for g in range(GROUPS):
        cs = slice(g * GROUP_W, (g + 1) * GROUP_W)
        blk = y[:, cs]
        ms = jnp.mean(blk * blk, axis=-1, keepdims=True)
        y_ref[:, cs] = (blk * lax.rsqrt(ms + EPS) * gnw_ref[:, cs]).astype(y_ref.dtype)


def _ssd_chunk_body(zg_ref, xc_ref, dt_ref, h0_ref, alr_ref, expand_ref, dexp_ref, gnw_ref,
                    y_ref, hout_ref, hst_ref, ysc_ref, *, lc, nc):
    c = pl.program_id(1)

    @pl.when(c == 0)
    def _():
        hst_ref[...] = h0_ref[0].T

    row_i = lax.broadcasted_iota(jnp.int32, (lc, lc), 0)
    col_i = lax.broadcasted_iota(jnp.int32, (lc, lc), 1)
    causal = row_i >= col_i
    lane_lo = lax.broadcasted_iota(jnp.int32, (lc, LANES), 1) < HEAD_DIM

    dt = dt_ref[...]
    la = dt * (-jnp.exp(alr_ref[...]))
    a_cum = _cumsum_rows(la, causal)
    a_last = a_cum[lc - 1:lc, :]
    expand = expand_ref[...]
    e_exp = _dot_split(jnp.exp(a_cum), expand, 2)
    w_exp = _dot_split(jnp.exp(a_last - a_cum) * dt, expand, 2)
    a2 = a_cum * LOG2E
    r_t = (a2 - jnp.log2(dt)).T

    for g in range(GROUPS):
        bg = xc_ref[:, D_SSM + g * D_STATE:D_SSM + (g + 1) * D_STATE]
        cg = xc_ref[:, D_SSM + GN + g * D_STATE:D_SSM + GN + (g + 1) * D_STATE].astype(BF16)
        cols = slice(g * GROUP_W, (g + 1) * GROUP_W)
        h_prev = hst_ref[:, cols]
        cb = lax.dot_general(cg, bg.astype(BF16), NT_DIMS, preferred_element_type=F32)
        y_off = jnp.dot(cg, h_prev.astype(BF16), preferred_element_type=F32)
        xs_g = xc_ref[:, cols]
        for pr in range(HPG // 2):
            mats = []
            for hh in (g * HPG + 2 * pr, g * HPG + 2 * pr + 1):
                seg = a2[:, hh:hh + 1] - r_t[hh:hh + 1, :]
                dec = jnp.exp2(jnp.where(causal, seg, -jnp.inf))
                mats.append((cb * dec).astype(BF16))
            xp = xs_g[:, pr * LANES:(pr + 1) * LANES].astype(BF16)
            rhs = jnp.concatenate([jnp.where(lane_lo, xp, 0), jnp.where(lane_lo, 0, xp)], axis=0)
            y_diag = jnp.dot(jnp.concatenate(mats, axis=1), rhs, preferred_element_type=F32)
            pc = slice(g * GROUP_W + pr * LANES, g * GROUP_W + (pr + 1) * LANES)
            ysc_ref[:, pc] = y_diag + y_off[:, pr * LANES:(pr + 1) * LANES] * e_exp[:, pc]
        xw = (xs_g * w_exp[:, cols]).astype(BF16)
        s_t = jnp.dot(bg.T.astype(BF16), xw, preferred_element_type=F32)
        hst_ref[:, cols] = h_prev * e_exp[lc - 1:lc, cols] + s_t

    y = ysc_ref[...] + xc_ref[:, 0:D_SSM] * dexp_ref[...]
    _gated_norm_store(y, zg_ref, gnw_ref, y_ref)

    @pl.when(c == nc - 1)
    def _():
        hout_ref[0] = hst_ref[...].T


def _ssd_short_body(zg_ref, xc_ref, dt_ref, h0_ref, alr_ref, expand_ref, exps_ref, dexp_ref, gnw_ref,
                    y_ref, hout_ref, ysc_ref, *, lc, spb):
    rows = spb * lc
    hl = HEADS * lc
    tloc = lax.broadcasted_iota(jnp.int32, (rows, LANES), 0) % lc
    dt = dt_ref[...]
    a_cum = dt * (-jnp.exp(alr_ref[...]))
    k = 1
    while k < lc:
        a_cum = a_cum + jnp.where(tloc >= k, pltpu.roll(a_cum, k, axis=0), 0.0)
        k *= 2
    a_last = a_cum.reshape(spb, lc, LANES)[:, lc - 1:lc, :]
    a_last_b = jnp.broadcast_to(a_last, (spb, lc, LANES)).reshape(rows, LANES)
    expand = expand_ref[...]
    e_exp = _dot_split(jnp.exp(a_cum), expand, 2)
    w_exp = _dot_split(jnp.exp(a_last_b - a_cum) * dt, expand, 2)
    cd_t = jnp.exp(a_last.reshape(spb, LANES)).T

    z2 = _dot_split(jnp.concatenate([a_cum, dt], axis=0), exps_ref[...], 3)
    zc, zd = z2[:rows], z2[rows:]
    trow = lax.broadcasted_iota(jnp.int32, (rows, hl), 0) % lc
    tsrc = lax.broadcasted_iota(jnp.int32, (rows, hl), 1) % lc
    diag = trow == tsrc

    def per_source(z):
        d = jnp.where(diag, z, 0.0).reshape(spb, lc, hl).sum(axis=1, keepdims=True)
        return jnp.broadcast_to(d, (spb, lc, hl)).reshape(rows, hl)

    dec = jnp.exp(jnp.where(trow >= tsrc, zc - per_source(zc), -jnp.inf)) * per_source(zd)

    pair_w = HPG * lc
    blockdiag = (lax.broadcasted_iota(jnp.int32, (pair_w, GROUP_W), 0) // lc
                 == lax.broadcasted_iota(jnp.int32, (pair_w, GROUP_W), 1) // HEAD_DIM)

    for q in range(spb):
        tok = slice(q * lc, (q + 1) * lc)
        for g in range(GROUPS):
            bg = xc_ref[tok, D_SSM + g * D_STATE:D_SSM + (g + 1) * D_STATE]
            cg = xc_ref[tok, D_SSM + GN + g * D_STATE:D_SSM + GN + (g + 1) * D_STATE].astype(BF16)
            cols = slice(g * GROUP_W, (g + 1) * GROUP_W)
            h_prev = h0_ref[q, cols, :]
            xs_g = xc_ref[tok, cols]
            b_tiled = jnp.concatenate([bg] * HPG, axis=0).astype(BF16)
            cb = lax.dot_general(cg, b_tiled, NT_DIMS, preferred_element_type=F32)
            m = (cb * dec[tok, g * pair_w:(g + 1) * pair_w]).astype(BF16)
            x_diag = jnp.where(blockdiag, jnp.concatenate([xs_g] * HPG, axis=0), 0.0).astype(BF16)
            y_diag = jnp.dot(m, x_diag, preferred_element_type=F32)
            y_off = lax.dot_general(cg, h_prev.astype(BF16), NT_DIMS, preferred_element_type=F32)
            ysc_ref[tok, cols] = y_diag + y_off * e_exp[tok, cols]
            xw_t = (xs_g * w_exp[tok, cols]).T.astype(BF16)
            s_g = jnp.dot(xw_t, bg.astype(BF16), preferred_element_type=F32)
            cd = jnp.concatenate(
                [jnp.broadcast_to(cd_t[g * HPG + jh:g * HPG + jh + 1, q:q + 1], (HEAD_DIM, D_STATE))
                 for jh in range(HPG)], axis=0)
            hout_ref[q, cols, :] = h_prev * cd + s_g

    y = ysc_ref[...] + xc_ref[:, 0:D_SSM] * dexp_ref[...]
    _gated_norm_store(y, zg_ref, gnw_ref, y_ref)


def _ssd(zg, xc, dt, h0, alr, expand, exps, dexp, gnw, out_dtype, *, nb, nc, lc):
    t = zg.shape[0]
    par = lambda r, width: pl.BlockSpec((r, width), lambda b, c: (0, 0))
    params = [par(1, LANES), par(LANES, D_SSM)]
    operands = [alr, expand]
    if lc == SSD_CHUNK:
        spb = 1
        body = functools.partial(_ssd_chunk_body, lc=lc, nc=nc)
        scratch = [pltpu.VMEM((D_STATE, D_SSM), F32)]
    else:
        assert nc == 1
        spb = SHORT_SEQS_PER_STEP
        body = functools.partial(_ssd_short_body, lc=lc, spb=spb)
        scratch = []
        params.append(par(LANES, HEADS * lc))
        operands.append(exps)
    rows = spb * lc
    tok = lambda width: pl.BlockSpec((rows, width), lambda b, c: (b * nc + c, 0))
    h_spec = pl.BlockSpec((spb, D_SSM, D_STATE), lambda b, c: (b, 0, 0))
    return pl.pallas_call(
        body,
        grid=(nb // spb, nc),
        in_specs=[tok(D_SSM), tok(CONV_DIM), tok(LANES), h_spec, *params, par(1, D_SSM), par(1, D_SSM)],
        out_specs=[tok(D_SSM), h_spec],
        out_shape=[jax.ShapeDtypeStruct((t, D_SSM), out_dtype),
                   jax.ShapeDtypeStruct((nb, D_SSM, D_STATE), F32)],
        scratch_shapes=scratch + [pltpu.VMEM((rows, D_SSM), F32)],
        compiler_params=pltpu.CompilerParams(
            dimension_semantics=("arbitrary", "arbitrary"), vmem_limit_bytes=VMEM_LIMIT),
        name="ssd_scan",
    )(zg, xc, dt, h0, *operands, dexp, gnw)


def _out_body(ys_ref, yc_ref, x_ref, w1_ref, w2_ref, fnw_ref, o_ref):
    acc = jnp.dot(ys_ref[...].astype(BF16), w1_ref[...], preferred_element_type=F32)
    acc = acc + jnp.dot(yc_ref[...].astype(BF16), w2_ref[...], preferred_element_type=F32)
    r = x_ref[...] + acc
    var = jnp.mean(r * r, axis=-1, keepdims=True)
    o_ref[...] = r * lax.rsqrt(var + EPS) * fnw_ref[...]


def _out(ys, yc, x, w_o, fnw, *, tm):
    t = x.shape[0]
    row = lambda width: pl.BlockSpec((tm, width), lambda m: (m, 0))
    return pl.pallas_call(
        _out_body,
        grid=(t // tm,),
        in_specs=[row(D_SSM), row(D_SHORT), row(D_MODEL),
                  pl.BlockSpec((D_SSM, D_MODEL), lambda m: (0, 0)),
                  pl.BlockSpec((D_SHORT, D_MODEL), lambda m: (1, 0)),
                  pl.BlockSpec((1, D_MODEL), lambda m: (0, 0))],
        out_specs=row(D_MODEL),
        out_shape=jax.ShapeDtypeStruct((t, D_MODEL), F32),
        compiler_params=pltpu.CompilerParams(
            dimension_semantics=("arbitrary",), vmem_limit_bytes=VMEM_LIMIT),
        name="out_proj",
    )(ys, yc, x, w_o, w_o, fnw)


def _mixer(x2d, h0, conv_state, short_state, p, *, nb, seqlen, act_dtype, tm, nseq, tps):
    lc = min(SSD_CHUNK, seqlen)
    nc = seqlen // lc
    h, zg, dt = _proj_z(x2d, p["nw"], p["w_main"], p["dtb"], act_dtype, tm=tm)
    xc, conv_new = _proj_xc(h, p["w_main"], p["conv_w"], p["conv_b"], conv_state,
                            tm=tm, nseq=nseq, tps=tps)
    y_c, short_new = _short(h, p["w_sh"], p["conv_short_w"], short_state, act_dtype,
                            tm=tm, tn=256, nseq=nseq, tps=tps)
    y_s, h_new = _ssd(zg, xc, dt, h0, p["alr"], p["expand"], p["expand_pairs"](lc), p["dexp"], p["gnw"],
                      act_dtype, nb=nb, nc=nc, lc=lc)
    return y_s, y_c, h_new, conv_new, short_new


def kernel(x_prompt, x_sample, state_ssm, state_conv_ssd, state_conv_short, norm_w, w_in, conv_ssd_w,
           conv_ssd_b, dt_bias, a_log, d_skip, ssd_norm_w, conv_short_w, w_out, final_norm_w):
    depth = norm_w.shape[0]
    assert depth == 1, "the output projection fuses the final rmsnorm, valid for a single layer"
    bp, lp, _ = x_prompt.shape
    bs, ls, _ = x_sample.shape
    hp = x_prompt.reshape(bp * lp, D_MODEL)
    hs = x_sample.reshape(bs * ls, D_MODEL)
    pad_h = LANES - HEADS
    expand = (jnp.arange(LANES)[:, None] == jnp.arange(D_SSM)[None, :] // HEAD_DIM).astype(BF16)
    outs = [[] for _ in range(6)]
    for layer in range(depth):
        w_t = w_in[layer].T
        p = dict(
            nw=norm_w[layer][None, :],
            w_main=_wprep(w_t, start=0, n_out=MAIN_W, valid=MAIN_VALID),
            w_sh=_wprep(w_t, start=MAIN_VALID, n_out=4 * D_SHORT, valid=4 * D_SHORT),
            conv_short_w=conv_short_w[layer],
            conv_w=conv_ssd_w[layer],
            conv_b=conv_ssd_b[layer][None, :],
            dtb=jnp.pad(dt_bias[layer], (0, pad_h))[None, :],
            alr=jnp.pad(a_log[layer], (0, pad_h))[None, :],
            expand=expand,
            expand_pairs=lambda lc: (jnp.arange(LANES)[:, None]
                                     == jnp.arange(HEADS * lc)[None, :] // lc).astype(BF16),
            dexp=jnp.repeat(d_skip[layer], HEAD_DIM)[None, :],
            gnw=ssd_norm_w[layer][None, :],
        )
        w_o = w_out[layer].astype(BF16)

        tm_p = 1024
        ys, yc, a1, a2, a3 = _mixer(
            hp, jnp.zeros((bp, D_SSM, D_STATE), F32), jnp.zeros((bp, CONV_K - 1, CONV_DIM), F32),
            jnp.zeros((bp, SHORT_K - 1, D_SHORT), F32), p,
            nb=bp, seqlen=lp, act_dtype=BF16, tm=tm_p, nseq=1, tps=lp // tm_p)
        ys2, yc2, s1, s2, s3 = _mixer(
            hs, state_ssm[layer].reshape(bs, D_SSM, D_STATE), state_conv_ssd[layer],
            state_conv_short[layer], p,
            nb=bs, seqlen=ls, act_dtype=F32, tm=bs * ls, nseq=bs, tps=1)
        fw = final_norm_w[None, :]
        hp = _out(ys, yc, hp, w_o, fw, tm=256)
        hs = _out(ys2, yc2, hs, w_o, fw, tm=256)
        for lst, val in zip(outs, (a1.reshape(bp, HEADS, HEAD_DIM, D_STATE), a2, a3,
                                   s1.reshape(bs, HEADS, HEAD_DIM, D_STATE), s2, s3)):
            lst.append(val)
    return (hp.reshape(bp, lp, D_MODEL), hs.reshape(bs, ls, D_MODEL),
            *(jnp.stack(v) for v in outs))
```

```python
import functools

import jax
import jax.numpy as jnp
from jax import lax
from jax.experimental import pallas as pl
from jax.experimental.pallas import tpu as pltpu

F32 = jnp.float32
BF16 = jnp.bfloat16

D_MODEL = 2048
D_SSM = 2048
D_SHORT = 2048
HEADS = 32
HEAD_DIM = 64
D_STATE = 128
GROUPS = 4
HPG = HEADS // GROUPS
GROUP_W = HPG * HEAD_DIM
GN = GROUPS * D_STATE
CONV_DIM = D_SSM + 2 * GN
CONV_K = 4
SHORT_K = 3
SSD_CHUNK = 128
EPS = 1e-6
LANES = 128
SUBLANES = 8
VMEM_LIMIT = 56 * 1024 * 1024

TN = 512
Z_TILES = D_SSM // TN
XC_TILES = CONV_DIM // TN
MAIN_VALID = D_SSM + CONV_DIM + HEADS
DT_BLOCK = (D_SSM + CONV_DIM) // LANES

NT_DIMS = (((1,), (1,)), ((), ()))
LOG2E = 1.4426950408889634
SHORT_SEQS_PER_STEP = 8


def _silu(x):
    half = 0.5 * x
    return half + half * jnp.tanh(half)


def _softplus(x):
    return jnp.maximum(x, 0.0) + jnp.log1p(jnp.exp(-jnp.abs(x)))


def _split3(x):
    hi = x.astype(BF16)
    r = x - hi.astype(F32)
    mid = r.astype(BF16)
    lo = (r - mid.astype(F32)).astype(BF16)
    return hi, mid, lo


def _wprep_body(*refs, shift, valid):
    if shift:
        a_ref, b_ref, o_ref = refs
        w = jnp.concatenate([a_ref[shift:, :], b_ref[...]], axis=0)
    else:
        a_ref, o_ref = refs
        w = a_ref[...]
    row = pl.program_id(0) * TN + lax.broadcasted_iota(jnp.int32, w.shape, 0)
    o_ref[...] = jnp.where(row < valid, w, 0.0).T.astype(BF16)


def _wprep(w_t, *, start, n_out, valid):
    base = (start // TN) * TN
    shift = start - base
    assert shift % SUBLANES == 0 and n_out % TN == 0
    in_specs = [pl.BlockSpec((TN, D_MODEL), lambda j: (base // TN + j, 0))]
    operands = [w_t]
    if shift:
        assert TN % shift == 0 and base % shift == 0
        in_specs.append(pl.BlockSpec((shift, D_MODEL), lambda j: ((base + TN * (j + 1)) // shift, 0)))
        operands.append(w_t)
    return pl.pallas_call(
        functools.partial(_wprep_body, shift=shift, valid=valid),
        grid=(n_out // TN,),
        in_specs=in_specs,
        out_specs=pl.BlockSpec((D_MODEL, TN), lambda j: (0, j)),
        out_shape=jax.ShapeDtypeStruct((D_MODEL, n_out), BF16),
        compiler_params=pltpu.CompilerParams(dimension_semantics=("arbitrary",)),
        name="wprep",
    )(*operands)


def _proj_z_body(x_ref, nw_ref, w_ref, wdt_ref, dtb_ref, h_ref, zg_ref, dt_ref, *, tm):
    @pl.when(pl.program_id(1) == 0)
    def _():
        blk = 32
        nw = nw_ref[...]

        def body(i, carry):
            r = pl.ds(pl.multiple_of(i * blk, blk), blk)
            xf = x_ref[r, :]
            var = jnp.mean(xf * xf, axis=-1, keepdims=True)
            h_ref[r, :] = (xf * lax.rsqrt(var + EPS) * nw).astype(BF16)
            return carry

        lax.fori_loop(0, tm // blk, body, 0, unroll=2)
        dt_raw = lax.dot_general(h_ref[...], wdt_ref[...].astype(BF16), NT_DIMS,
                                 preferred_element_type=F32)
        dt_ref[...] = _softplus(dt_raw + dtb_ref[...])

    acc = lax.dot_general(h_ref[...], w_ref[...].astype(BF16), NT_DIMS, preferred_element_type=F32)
    zg_ref[...] = _silu(acc).astype(zg_ref.dtype)


def _proj_z(x, nw, w_t, dtb, zg_dtype, *, tm):
    t = x.shape[0]
    return pl.pallas_call(
        functools.partial(_proj_z_body, tm=tm),
        grid=(t // tm, Z_TILES),
        in_specs=[
            pl.BlockSpec((tm, D_MODEL), lambda m, j: (m, 0)),
            pl.BlockSpec((1, D_MODEL), lambda m, j: (0, 0)),
            pl.BlockSpec((TN, D_MODEL), lambda m, j: (j, 0)),
            pl.BlockSpec((LANES, D_MODEL), lambda m, j: (DT_BLOCK, 0)),
            pl.BlockSpec((1, LANES), lambda m, j: (0, 0)),
        ],
        out_specs=[pl.BlockSpec((tm, D_MODEL), lambda m, j: (m, 0)),
                   pl.BlockSpec((tm, TN), lambda m, j: (m, j)),
                   pl.BlockSpec((tm, LANES), lambda m, j: (m, 0))],
        out_shape=[jax.ShapeDtypeStruct((t, D_MODEL), BF16),
                   jax.ShapeDtypeStruct((t, D_SSM), zg_dtype),
                   jax.ShapeDtypeStruct((t, LANES), F32)],
        compiler_params=pltpu.CompilerParams(
            dimension_semantics=("arbitrary", "arbitrary"), vmem_limit_bytes=VMEM_LIMIT),
        name="in_proj_z",
    )(x, nw, w_t, w_t, dtb)


def _load_history(pad_ref, st_ref, carry, j, *, hist, tps):
    def from_state():
        pad_ref[:, 0:SUBLANES, :] = jnp.zeros((pad_ref.shape[0], SUBLANES, pad_ref.shape[2]), F32)
        pad_ref[:, SUBLANES - hist:SUBLANES, :] = st_ref[...]

    if tps > 1:
        (carry_ref,) = carry
        first = (pl.program_id(0) % tps) == 0
        pl.when(first)(from_state)

        @pl.when(jnp.logical_not(first))
        def _():
            pad_ref[:, 0:SUBLANES, :] = carry_ref[j]
    else:
        from_state()


def _causal_conv(pad_ref, cw_ref, taps):
    xe = pad_ref[...]
    conv = xe[:, SUBLANES:, :] * cw_ref[taps - 1:taps, :]
    for s in range(1, taps):
        shifted = pltpu.roll(xe, s, axis=1)[:, SUBLANES:, :]
        conv = conv + shifted * cw_ref[taps - 1 - s:taps - s, :]
    return conv


def _proj_xc_body(h_ref, w_ref, cw_ref, cb_ref, st_ref, xc_ref, tails_ref, pad_ref, *carry,
                  tm, nseq, tps):
    j = pl.program_id(1)
    lt = tm // nseq
    _load_history(pad_ref, st_ref, carry, j, hist=CONV_K - 1, tps=tps)
    acc = lax.dot_general(h_ref[...], w_ref[...].astype(BF16), NT_DIMS, preferred_element_type=F32)
    pad_ref[:, SUBLANES:SUBLANES + lt, :] = acc.reshape(nseq, lt, TN)
    conv = _causal_conv(pad_ref, cw_ref, CONV_K)
    xc_ref[...] = _silu(conv + cb_ref[...]).reshape(tm, TN)
    tails_ref[...] = pad_ref[:, lt + SUBLANES - (CONV_K - 1):lt + SUBLANES, :]
    if tps > 1:
        carry[0][j] = pad_ref[:, lt:lt + SUBLANES, :]


def _proj_xc(h, w_t, conv_w, conv_b, conv_state, *, tm, nseq, tps):
    t = h.shape[0]
    lt = tm // nseq
    scratch = [pltpu.VMEM((nseq, SUBLANES + lt, TN), F32)]
    if tps > 1:
        scratch.append(pltpu.VMEM((XC_TILES, nseq, SUBLANES, TN), F32))
    xc, tails = pl.pallas_call(
        functools.partial(_proj_xc_body, tm=tm, nseq=nseq, tps=tps),
        grid=(t // tm, XC_TILES),
        in_specs=[
            pl.BlockSpec((tm, D_MODEL), lambda m, j: (m, 0)),
            pl.BlockSpec((TN, D_MODEL), lambda m, j: (Z_TILES + j, 0)),
            pl.BlockSpec((CONV_K, TN), lambda m, j: (0, j)),
            pl.BlockSpec((1, TN), lambda m, j: (0, j)),
            pl.BlockSpec((nseq, CONV_K - 1, TN), lambda m, j: (m // tps, 0, j)),
        ],
        out_specs=[pl.BlockSpec((tm, TN), lambda m, j: (m, j)),
                   pl.BlockSpec((nseq, CONV_K - 1, TN), lambda m, j: (m, 0, j))],
        out_shape=[jax.ShapeDtypeStruct((t, CONV_DIM), F32),
                   jax.ShapeDtypeStruct(((t // tm) * nseq, CONV_K - 1, CONV_DIM), F32)],
        scratch_shapes=scratch,
        compiler_params=pltpu.CompilerParams(
            dimension_semantics=("arbitrary", "arbitrary"), vmem_limit_bytes=VMEM_LIMIT),
        name="in_proj_xc",
    )(h, w_t, conv_w, conv_b, conv_state)
    tails = tails.reshape(t // (tm * tps), tps, nseq, CONV_K - 1, CONV_DIM)[:, -1]
    return xc, tails.reshape(conv_state.shape)


def _short_body(h_ref, wz_ref, wb_ref, wc_ref, wv_ref, cw_ref, st_ref,
                y_ref, ns_ref, pad_ref, *carry, tm, tn, nseq, tps):
    j = pl.program_id(1)
    lt = tm // nseq
    _load_history(pad_ref, st_ref, carry, j, hist=SHORT_K - 1, tps=tps)
    h = h_ref[...]
    c = jnp.dot(h, wc_ref[...], preferred_element_type=F32)
    v = jnp.dot(h, wv_ref[...], preferred_element_type=F32)
    pad_ref[:, SUBLANES:SUBLANES + lt, :] = (c * v).reshape(nseq, lt, tn)
    conv = _causal_conv(pad_ref, cw_ref, SHORT_K)
    b = jnp.dot(h, wb_ref[...], preferred_element_type=F32)
    z = jnp.dot(h, wz_ref[...], preferred_element_type=F32)
    y = b * conv.reshape(tm, tn) * _silu(z)
    y_ref[...] = y.astype(y_ref.dtype)
    ns_ref[...] = pad_ref[:, lt + SUBLANES - (SHORT_K - 1):lt + SUBLANES, :]
    if tps > 1:
        carry[0][j] = pad_ref[:, lt:lt + SUBLANES, :]


def _short(h, w_sh, conv_w, state, out_dtype, *, tm, tn, nseq, tps):
    t = h.shape[0]
    nj = D_SHORT // tn
    lt = tm // nseq
    scratch = [pltpu.VMEM((nseq, SUBLANES + lt, tn), F32)]
    if tps > 1:
        scratch.append(pltpu.VMEM((nj, nseq, SUBLANES, tn), F32))

    def wspec(k):
        return pl.BlockSpec((D_MODEL, tn), lambda m, j, k=k: (0, k * nj + j))

    y, tails = pl.pallas_call(
        functools.partial(_short_body, tm=tm, tn=tn, nseq=nseq, tps=tps),
        grid=(t // tm, nj),
        in_specs=[
            pl.BlockSpec((tm, D_MODEL), lambda m, j: (m, 0)),
            wspec(0), wspec(1), wspec(2), wspec(3),
            pl.BlockSpec((SHORT_K, tn), lambda m, j: (0, j)),
            pl.BlockSpec((nseq, SHORT_K - 1, tn), lambda m, j: (m // tps, 0, j)),
        ],
        out_specs=[pl.BlockSpec((tm, tn), lambda m, j: (m, j)),
                   pl.BlockSpec((nseq, SHORT_K - 1, tn), lambda m, j: (m, 0, j))],
        out_shape=[jax.ShapeDtypeStruct((t, D_SHORT), out_dtype),
                   jax.ShapeDtypeStruct(((t // tm) * nseq, SHORT_K - 1, D_SHORT), F32)],
        scratch_shapes=scratch,
        compiler_params=pltpu.CompilerParams(
            dimension_semantics=("arbitrary", "arbitrary"), vmem_limit_bytes=VMEM_LIMIT),
        name="short_conv",
    )(h, w_sh, w_sh, w_sh, w_sh, conv_w, state)
    tails = tails.reshape(t // (tm * tps), tps, nseq, SHORT_K - 1, D_SHORT)[:, -1]
    return y, tails.reshape(state.shape)


def _dot_split(x, w, pieces):
    out = None
    r = x
    for _ in range(pieces):
        p = r.astype(BF16)
        d = jnp.dot(p, w, preferred_element_type=F32)
        out = d if out is None else out + d
        r = r - p.astype(F32)
    return out


def _cumsum_rows(x, causal):
    tri = causal.astype(BF16)
    return sum(jnp.dot(tri, p, preferred_element_type=F32) for p in _split3(x))


def _gated_norm_store(y, zg_ref, gnw_ref, y_ref):
    y = y * zg_ref[...].astype(F32)
    for g in range(GROUPS):
        cs = slice(g * GROUP_W, (g + 1) * GROUP_W)
        blk = y[:, cs]
        ms = jnp.mean(blk * blk, axis=-1, keepdims=True)
        y_ref[:, cs] = (blk * lax.rsqrt(ms + EPS) * gnw_ref[:, cs]).astype(y_ref.dtype)


def _ssd_chunk_body(zg_ref, xc_ref, dt_ref, h0_ref, alr_ref, expand_ref, dexp_ref, gnw_ref,
                    y_ref, hout_ref, hst_ref, ysc_ref, *, lc, nc):
    c = pl.program_id(1)

    @pl.when(c == 0)
    def _():
        hst_ref[...] = h0_ref[0].T

    row_i = lax.broadcasted_iota(jnp.int32, (lc, lc), 0)
    col_i = lax.broadcasted_iota(jnp.int32, (lc, lc), 1)
    causal = row_i >= col_i
    lane_lo = lax.broadcasted_iota(jnp.int32, (lc, LANES), 1) < HEAD_DIM

    dt = dt_ref[...]
    la = dt * (-jnp.exp(alr_ref[...]))
    a_cum = _cumsum_rows(la, causal)
    a_last = a_cum[lc - 1:lc, :]
    expand = expand_ref[...]
    factors = jnp.concatenate([jnp.exp(a_cum), jnp.exp(a_last - a_cum) * dt], axis=0).astype(BF16)
    both = jnp.dot(factors, expand, preferred_element_type=F32)
    e_exp, w_exp = both[:lc], both[lc:]
    a2 = a_cum * LOG2E
    r_t = (a2 - jnp.log2(dt)).T

    for g in range(GROUPS):
        bg = xc_ref[:, D_SSM + g * D_STATE:D_SSM + (g + 1) * D_STATE]
        cg = xc_ref[:, D_SSM + GN + g * D_STATE:D_SSM + GN + (g + 1) * D_STATE].astype(BF16)
        cols = slice(g * GROUP_W, (g + 1) * GROUP_W)
        h_prev = hst_ref[:, cols]
        cb = lax.dot_general(cg, bg.astype(BF16), NT_DIMS, preferred_element_type=F32)
        y_off = jnp.dot(cg, h_prev.astype(BF16), preferred_element_type=F32)
        xs_g = xc_ref[:, cols]
        for pr in range(HPG // 2):
            mats = []
            for hh in (g * HPG + 2 * pr, g * HPG + 2 * pr + 1):
                seg = a2[:, hh:hh + 1] - r_t[hh:hh + 1, :]
                dec = jnp.exp2(jnp.where(causal, seg, -jnp.inf))
                mats.append((cb * dec).astype(BF16))
            xp = xs_g[:, pr * LANES:(pr + 1) * LANES].astype(BF16)
            rhs = jnp.concatenate([jnp.where(lane_lo, xp, 0), jnp.where(lane_lo, 0, xp)], axis=0)
            y_diag = jnp.dot(jnp.concatenate(mats, axis=1), rhs, preferred_element_type=F32)
            pc = slice(g * GROUP_W + pr * LANES, g * GROUP_W + (pr + 1) * LANES)
            ysc_ref[:, pc] = y_diag + y_off[:, pr * LANES:(pr + 1) * LANES] * e_exp[:, pc]
        xw = (xs_g * w_exp[:, cols]).astype(BF16)
        s_t = jnp.dot(bg.T.astype(BF16), xw, preferred_element_type=F32)
        hst_ref[:, cols] = h_prev * e_exp[lc - 1:lc, cols] + s_t

    y = ysc_ref[...] + xc_ref[:, 0:D_SSM] * dexp_ref[...]
    _gated_norm_store(y, zg_ref, gnw_ref, y_ref)

    @pl.when(c == nc - 1)
    def _():
        hout_ref[0] = hst_ref[...].T


def _ssd_short_body(zg_ref, xc_ref, dt_ref, h0_ref, alr_ref, expand_ref, exps_ref, dexp_ref, gnw_ref,
                    y_ref, hout_ref, ysc_ref, *, lc, spb):
    rows = spb * lc
    hl = HEADS * lc
    tloc = lax.broadcasted_iota(jnp.int32, (rows, LANES), 0) % lc
    dt = dt_ref[...]
    a_cum = dt * (-jnp.exp(alr_ref[...]))
    k = 1
    while k < lc:
        a_cum = a_cum + jnp.where(tloc >= k, pltpu.roll(a_cum, k, axis=0), 0.0)
        k *= 2
    a_last = a_cum.reshape(spb, lc, LANES)[:, lc - 1:lc, :]
    a_last_b = jnp.broadcast_to(a_last, (spb, lc, LANES)).reshape(rows, LANES)
    expand = expand_ref[...]
    e_exp = _dot_split(jnp.exp(a_cum), expand, 2)
    w_exp = _dot_split(jnp.exp(a_last_b - a_cum) * dt, expand, 2)
    cd_t = jnp.exp(a_last.reshape(spb, LANES)).T

    z2 = _dot_split(jnp.concatenate([a_cum, dt], axis=0), exps_ref[...], 3)
    zc, zd = z2[:rows], z2[rows:]
    trow = lax.broadcasted_iota(jnp.int32, (rows, hl), 0) % lc
    tsrc = lax.broadcasted_iota(jnp.int32, (rows, hl), 1) % lc
    diag = trow == tsrc

    def per_source(z):
        d = jnp.where(diag, z, 0.0).reshape(spb, lc, hl).sum(axis=1, keepdims=True)
        return jnp.broadcast_to(d, (spb, lc, hl)).reshape(rows, hl)

    dec = jnp.exp(jnp.where(trow >= tsrc, zc - per_source(zc), -jnp.inf)) * per_source(zd)

    pair_w = HPG * lc
    blockdiag = (lax.broadcasted_iota(jnp.int32, (pair_w, GROUP_W), 0) // lc
                 == lax.broadcasted_iota(jnp.int32, (pair_w, GROUP_W), 1) // HEAD_DIM)

    for q in range(spb):
        tok = slice(q * lc, (q + 1) * lc)
        for g in range(GROUPS):
            bg = xc_ref[tok, D_SSM + g * D_STATE:D_SSM + (g + 1) * D_STATE]
            cg = xc_ref[tok, D_SSM + GN + g * D_STATE:D_SSM + GN + (g + 1) * D_STATE].astype(BF16)
            cols = slice(g * GROUP_W, (g + 1) * GROUP_W)
            h_prev = h0_ref[q, cols, :]
            xs_g = xc_ref[tok, cols]
            b_tiled = jnp.concatenate([bg] * HPG, axis=0).astype(BF16)
            cb = lax.dot_general(cg, b_tiled, NT_DIMS, preferred_element_type=F32)
            m = (cb * dec[tok, g * pair_w:(g + 1) * pair_w]).astype(BF16)
            x_diag = jnp.where(blockdiag, jnp.concatenate([xs_g] * HPG, axis=0), 0.0).astype(BF16)
            y_diag = jnp.dot(m, x_diag, preferred_element_type=F32)
            y_off = lax.dot_general(cg, h_prev.astype(BF16), NT_DIMS, preferred_element_type=F32)
            ysc_ref[tok, cols] = y_diag + y_off * e_exp[tok, cols]
            xw_t = (xs_g * w_exp[tok, cols]).T.astype(BF16)
            s_g = jnp.dot(xw_t, bg.astype(BF16), preferred_element_type=F32)
            cd = jnp.concatenate(
                [jnp.broadcast_to(cd_t[g * HPG + jh:g * HPG + jh + 1, q:q + 1], (HEAD_DIM, D_STATE))
                 for jh in range(HPG)], axis=0)
            hout_ref[q, cols, :] = h_prev * cd + s_g

    y = ysc_ref[...] + xc_ref[:, 0:D_SSM] * dexp_ref[...]
    _gated_norm_store(y, zg_ref, gnw_ref, y_ref)


def _ssd(zg, xc, dt, h0, alr, expand, exps, dexp, gnw, out_dtype, *, nb, nc, lc):
    t = zg.shape[0]
    par = lambda r, width: pl.BlockSpec((r, width), lambda b, c: (0, 0))
    params = [par(1, LANES), par(LANES, D_SSM)]
    operands = [alr, expand]
    if lc == SSD_CHUNK:
        spb = 1
        body = functools.partial(_ssd_chunk_body, lc=lc, nc=nc)
        scratch = [pltpu.VMEM((D_STATE, D_SSM), F32)]
    else:
        assert nc == 1
        spb = SHORT_SEQS_PER_STEP
        body = functools.partial(_ssd_short_body, lc=lc, spb=spb)
        scratch = []
        params.append(par(LANES, HEADS * lc))
        operands.append(exps)
    rows = spb * lc
    tok = lambda width: pl.BlockSpec((rows, width), lambda b, c: (b * nc + c, 0))
    h_spec = pl.BlockSpec((spb, D_SSM, D_STATE), lambda b, c: (b, 0, 0))
    return pl.pallas_call(
        body,
        grid=(nb // spb, nc),
        in_specs=[tok(D_SSM), tok(CONV_DIM), tok(LANES), h_spec, *params, par(1, D_SSM), par(1, D_SSM)],
        out_specs=[tok(D_SSM), h_spec],
        out_shape=[jax.ShapeDtypeStruct((t, D_SSM), out_dtype),
                   jax.ShapeDtypeStruct((nb, D_SSM, D_STATE), F32)],
        scratch_shapes=scratch + [pltpu.VMEM((rows, D_SSM), F32)],
        compiler_params=pltpu.CompilerParams(
            dimension_semantics=("arbitrary", "arbitrary"), vmem_limit_bytes=VMEM_LIMIT),
        name="ssd_scan",
    )(zg, xc, dt, h0, *operands, dexp, gnw)


def _out_body(ys_ref, yc_ref, x_ref, w1_ref, w2_ref, fnw_ref, o_ref):
    acc = jnp.dot(ys_ref[...].astype(BF16), w1_ref[...], preferred_element_type=F32)
    acc = acc + jnp.dot(yc_ref[...].astype(BF16), w2_ref[...], preferred_element_type=F32)
    r = x_ref[...] + acc
    var = jnp.mean(r * r, axis=-1, keepdims=True)
    o_ref[...] = r * lax.rsqrt(var + EPS) * fnw_ref[...]


def _out(ys, yc, x, w_o, fnw, *, tm):
    t = x.shape[0]
    row = lambda width: pl.BlockSpec((tm, width), lambda m: (m, 0))
    return pl.pallas_call(
        _out_body,
        grid=(t // tm,),
        in_specs=[row(D_SSM), row(D_SHORT), row(D_MODEL),
                  pl.BlockSpec((D_SSM, D_MODEL), lambda m: (0, 0)),
                  pl.BlockSpec((D_SHORT, D_MODEL), lambda m: (1, 0)),
                  pl.BlockSpec((1, D_MODEL), lambda m: (0, 0))],
        out_specs=row(D_MODEL),
        out_shape=jax.ShapeDtypeStruct((t, D_MODEL), F32),
        compiler_params=pltpu.CompilerParams(
            dimension_semantics=("arbitrary",), vmem_limit_bytes=VMEM_LIMIT),
        name="out_proj",
    )(ys, yc, x, w_o, w_o, fnw)


def _mixer(x2d, h0, conv_state, short_state, p, *, nb, seqlen, act_dtype, tm, nseq, tps):
    lc = min(SSD_CHUNK, seqlen)
    nc = seqlen // lc
    h, zg, dt = _proj_z(x2d, p["nw"], p["w_t"], p["dtb"], act_dtype, tm=tm)
    xc, conv_new = _proj_xc(h, p["w_t"], p["conv_w"], p["conv_b"], conv_state,
                            tm=tm, nseq=nseq, tps=tps)
    y_c, short_new = _short(h, p["w_sh"], p["conv_short_w"], short_state, act_dtype,
                            tm=tm, tn=256, nseq=nseq, tps=tps)
    y_s, h_new = _ssd(zg, xc, dt, h0, p["alr"], p["expand"], p["expand_pairs"](lc), p["dexp"], p["gnw"],
                      act_dtype, nb=nb, nc=nc, lc=lc)
    return y_s, y_c, h_new, conv_new, short_new


def kernel(x_prompt, x_sample, state_ssm, state_conv_ssd, state_conv_short, norm_w, w_in, conv_ssd_w,
           conv_ssd_b, dt_bias, a_log, d_skip, ssd_norm_w, conv_short_w, w_out, final_norm_w):
    depth = norm_w.shape[0]
    assert depth == 1, "the output projection fuses the final rmsnorm, valid for a single layer"
    bp, lp, _ = x_prompt.shape
    bs, ls, _ = x_sample.shape
    hp = x_prompt.reshape(bp * lp, D_MODEL)
    hs = x_sample.reshape(bs * ls, D_MODEL)
    pad_h = LANES - HEADS
    expand = (jnp.arange(LANES)[:, None] == jnp.arange(D_SSM)[None, :] // HEAD_DIM).astype(BF16)
    outs = [[] for _ in range(6)]
    for layer in range(depth):
        w_t = w_in[layer].T
        p = dict(
            nw=norm_w[layer][None, :],
            w_t=w_t,
            w_sh=_wprep(w_t, start=MAIN_VALID, n_out=4 * D_SHORT, valid=4 * D_SHORT),
            conv_short_w=conv_short_w[layer],
            conv_w=conv_ssd_w[layer],
            conv_b=conv_ssd_b[layer][None, :],
            dtb=jnp.pad(dt_bias[layer], (0, pad_h))[None, :],
            alr=jnp.pad(a_log[layer], (0, pad_h))[None, :],
            expand=expand,
            expand_pairs=lambda lc: (jnp.arange(LANES)[:, None]
                                     == jnp.arange(HEADS * lc)[None, :] // lc).astype(BF16),
            dexp=jnp.repeat(d_skip[layer], HEAD_DIM)[None, :],
            gnw=ssd_norm_w[layer][None, :],
        )
        w_o = w_out[layer].astype(BF16)

        tm_p = 1024
        ys, yc, a1, a2, a3 = _mixer(
            hp, jnp.zeros((bp, D_SSM, D_STATE), F32), jnp.zeros((bp, CONV_K - 1, CONV_DIM), F32),
            jnp.zeros((bp, SHORT_K - 1, D_SHORT), F32), p,
            nb=bp, seqlen=lp, act_dtype=BF16, tm=tm_p, nseq=1, tps=lp // tm_p)
        ys2, yc2, s1, s2, s3 = _mixer(
            hs, state_ssm[layer].reshape(bs, D_SSM, D_STATE), state_conv_ssd[layer],
            state_conv_short[layer], p,
            nb=bs, seqlen=ls, act_dtype=F32, tm=bs * ls, nseq=bs, tps=1)
        fw = final_norm_w[None, :]
        hp = _out(ys, yc, hp, w_o, fw, tm=256)
        hs = _out(ys2, yc2, hs, w_o, fw, tm=256)
        for lst, val in zip(outs, (a1.reshape(bp, HEADS, HEAD_DIM, D_STATE), a2, a3,
                                   s1.reshape(bs, HEADS, HEAD_DIM, D_STATE), s2, s3)):
            lst.append(val)
    return (hp.reshape(bp, lp, D_MODEL), hs.reshape(bs, ls, D_MODEL),
            *(jnp.stack(v) for v in outs))
```

```python
import functools

import jax
import jax.numpy as jnp
from jax import lax
from jax.experimental import pallas as pl
from jax.experimental.pallas import tpu as pltpu

F32 = jnp.float32
BF16 = jnp.bfloat16

D_MODEL = 2048
D_SSM = 2048
D_SHORT = 2048
HEADS = 32
HEAD_DIM = 64
D_STATE = 128
GROUPS = 4
HPG = HEADS // GROUPS
GROUP_W = HPG * HEAD_DIM
GN = GROUPS * D_STATE
CONV_DIM = D_SSM + 2 * GN
CONV_K = 4
SHORT_K = 3
SSD_CHUNK = 128
EPS = 1e-6
LANES = 128
SUBLANES = 8
VMEM_LIMIT = 56 * 1024 * 1024

TN = 512
Z_TILES = D_SSM // TN
XC_TILES = CONV_DIM // TN
MAIN_VALID = D_SSM + CONV_DIM + HEADS
DT_BLOCK = (D_SSM + CONV_DIM) // LANES

NT_DIMS = (((1,), (1,)), ((), ()))
LOG2E = 1.4426950408889634
SHORT_SEQS_PER_STEP = 8


def _silu(x):
    half = 0.5 * x
    return half + half * jnp.tanh(half)


def _softplus(x):
    return jnp.maximum(x, 0.0) + jnp.log1p(jnp.exp(-jnp.abs(x)))


def _split3(x):
    hi = x.astype(BF16)
    r = x - hi.astype(F32)
    mid = r.astype(BF16)
    lo = (r - mid.astype(F32)).astype(BF16)
    return hi, mid, lo


def _proj_z_body(x_ref, nw_ref, w_ref, wdt_ref, dtb_ref, h_ref, zg_ref, dt_ref, *, tm):
    @pl.when(pl.program_id(1) == 0)
    def _():
        blk = 32
        nw = nw_ref[...]

        def body(i, carry):
            r = pl.ds(pl.multiple_of(i * blk, blk), blk)
            xf = x_ref[r, :]
            var = jnp.mean(xf * xf, axis=-1, keepdims=True)
            h_ref[r, :] = (xf * lax.rsqrt(var + EPS) * nw).astype(BF16)
            return carry

        lax.fori_loop(0, tm // blk, body, 0, unroll=8)
        dt_raw = lax.dot_general(h_ref[...], wdt_ref[...].astype(BF16), NT_DIMS,
                                 preferred_element_type=F32)
        dt_ref[...] = _softplus(dt_raw + dtb_ref[...])

    acc = lax.dot_general(h_ref[...], w_ref[...].astype(BF16), NT_DIMS, preferred_element_type=F32)
    zg_ref[...] = _silu(acc).astype(zg_ref.dtype)


def _proj_z(x, nw, w_t, dtb, zg_dtype, *, tm):
    t = x.shape[0]
    return pl.pallas_call(
        functools.partial(_proj_z_body, tm=tm),
        grid=(t // tm, Z_TILES),
        in_specs=[
            pl.BlockSpec((tm, D_MODEL), lambda m, j: (m, 0)),
            pl.BlockSpec((1, D_MODEL), lambda m, j: (0, 0)),
            pl.BlockSpec((TN, D_MODEL), lambda m, j: (j, 0)),
            pl.BlockSpec((LANES, D_MODEL), lambda m, j: (DT_BLOCK, 0)),
            pl.BlockSpec((1, LANES), lambda m, j: (0, 0)),
        ],
        out_specs=[pl.BlockSpec((tm, D_MODEL), lambda m, j: (m, 0)),
                   pl.BlockSpec((tm, TN), lambda m, j: (m, j)),
                   pl.BlockSpec((tm, LANES), lambda m, j: (m, 0))],
        out_shape=[jax.ShapeDtypeStruct((t, D_MODEL), BF16),
                   jax.ShapeDtypeStruct((t, D_SSM), zg_dtype),
                   jax.ShapeDtypeStruct((t, LANES), F32)],
        compiler_params=pltpu.CompilerParams(
            dimension_semantics=("arbitrary", "arbitrary"), vmem_limit_bytes=VMEM_LIMIT),
        name="in_proj_z",
    )(x, nw, w_t, w_t, dtb)


def _load_history(pad_ref, st_ref, carry, j, *, hist, tps):
    def from_state():
        pad_ref[:, 0:SUBLANES, :] = jnp.zeros((pad_ref.shape[0], SUBLANES, pad_ref.shape[2]), F32)
        pad_ref[:, SUBLANES - hist:SUBLANES, :] = st_ref[...]

    if tps > 1:
        (carry_ref,) = carry
        first = (pl.program_id(0) % tps) == 0
        pl.when(first)(from_state)

        @pl.when(jnp.logical_not(first))
        def _():
            pad_ref[:, 0:SUBLANES, :] = carry_ref[j]
    else:
        from_state()


def _causal_conv(pad_ref, cw_ref, taps):
    xe = pad_ref[...]
    conv = xe[:, SUBLANES:, :] * cw_ref[taps - 1:taps, :]
    for s in range(1, taps):
        shifted = pltpu.roll(xe, s, axis=1)[:, SUBLANES:, :]
        conv = conv + shifted * cw_ref[taps - 1 - s:taps - s, :]
    return conv


def _proj_xc_body(h_ref, w_ref, cw_ref, cb_ref, st_ref, xc_ref, tails_ref, pad_ref, *carry,
                  tm, nseq, tps):
    j = pl.program_id(1)
    lt = tm // nseq
    _load_history(pad_ref, st_ref, carry, j, hist=CONV_K - 1, tps=tps)
    acc = lax.dot_general(h_ref[...], w_ref[...].astype(BF16), NT_DIMS, preferred_element_type=F32)
    pad_ref[:, SUBLANES:SUBLANES + lt, :] = acc.reshape(nseq, lt, TN)
    conv = _causal_conv(pad_ref, cw_ref, CONV_K)
    xc_ref[...] = _silu(conv + cb_ref[...]).reshape(tm, TN)
    tails_ref[...] = pad_ref[:, lt + SUBLANES - (CONV_K - 1):lt + SUBLANES, :]
    if tps > 1:
        carry[0][j] = pad_ref[:, lt:lt + SUBLANES, :]


def _proj_xc(h, w_t, conv_w, conv_b, conv_state, *, tm, nseq, tps):
    t = h.shape[0]
    lt = tm // nseq
    scratch = [pltpu.VMEM((nseq, SUBLANES + lt, TN), F32)]
    if tps > 1:
        scratch.append(pltpu.VMEM((XC_TILES, nseq, SUBLANES, TN), F32))
    xc, tails = pl.pallas_call(
        functools.partial(_proj_xc_body, tm=tm, nseq=nseq, tps=tps),
        grid=(t // tm, XC_TILES),
        in_specs=[
            pl.BlockSpec((tm, D_MODEL), lambda m, j: (m, 0)),
            pl.BlockSpec((TN, D_MODEL), lambda m, j: (Z_TILES + j, 0)),
            pl.BlockSpec((CONV_K, TN), lambda m, j: (0, j)),
            pl.BlockSpec((1, TN), lambda m, j: (0, j)),
            pl.BlockSpec((nseq, CONV_K - 1, TN), lambda m, j: (m // tps, 0, j)),
        ],
        out_specs=[pl.BlockSpec((tm, TN), lambda m, j: (m, j)),
                   pl.BlockSpec((nseq, CONV_K - 1, TN), lambda m, j: (m, 0, j))],
        out_shape=[jax.ShapeDtypeStruct((t, CONV_DIM), F32),
                   jax.ShapeDtypeStruct(((t // tm) * nseq, CONV_K - 1, CONV_DIM), F32)],
        scratch_shapes=scratch,
        compiler_params=pltpu.CompilerParams(
            dimension_semantics=("arbitrary", "arbitrary"), vmem_limit_bytes=VMEM_LIMIT),
        name="in_proj_xc",
    )(h, w_t, conv_w, conv_b, conv_state)
    tails = tails.reshape(t // (tm * tps), tps, nseq, CONV_K - 1, CONV_DIM)[:, -1]
    return xc, tails.reshape(conv_state.shape)


def _short_body(h_ref, *refs, tm, tn, nseq, tps, shift):
    w_refs, (cw_ref, st_ref, y_ref, ns_ref, pad_ref), carry = refs[:8], refs[8:13], refs[13:]
    j = pl.program_id(1)
    lt = tm // nseq
    _load_history(pad_ref, st_ref, carry, j, hist=SHORT_K - 1, tps=tps)
    h = h_ref[...]

    def band(k):
        a_ref, b_ref = w_refs[2 * k], w_refs[2 * k + 1]
        w = jnp.concatenate([a_ref[shift:, :], b_ref[...]], axis=0).astype(BF16)
        return lax.dot_general(h, w, NT_DIMS, preferred_element_type=F32)

    c = band(2)
    v = band(3)
    pad_ref[:, SUBLANES:SUBLANES + lt, :] = (c * v).reshape(nseq, lt, tn)
    conv = _causal_conv(pad_ref, cw_ref, SHORT_K)
    b = band(1)
    z = band(0)
    y = b * conv.reshape(tm, tn) * _silu(z)
    y_ref[...] = y.astype(y_ref.dtype)
    ns_ref[...] = pad_ref[:, lt + SUBLANES - (SHORT_K - 1):lt + SUBLANES, :]
    if tps > 1:
        carry[0][j] = pad_ref[:, lt:lt + SUBLANES, :]


def _short(h, w_t, conv_w, state, out_dtype, *, tm, tn, nseq, tps):
    t = h.shape[0]
    nj = D_SHORT // tn
    lt = tm // nseq
    base = (MAIN_VALID // tn) * tn
    shift = MAIN_VALID - base
    assert shift % SUBLANES == 0 and tn % shift == 0 and base % shift == 0
    scratch = [pltpu.VMEM((nseq, SUBLANES + lt, tn), F32)]
    if tps > 1:
        scratch.append(pltpu.VMEM((nj, nseq, SUBLANES, tn), F32))
    w_specs = []
    for k in range(4):
        r0 = base + k * D_SHORT
        w_specs.append(pl.BlockSpec((tn, D_MODEL), lambda m, j, r0=r0: (r0 // tn + j, 0)))
        w_specs.append(pl.BlockSpec((shift, D_MODEL), lambda m, j, r0=r0: ((r0 + tn * (j + 1)) // shift, 0)))

    y, tails = pl.pallas_call(
        functools.partial(_short_body, tm=tm, tn=tn, nseq=nseq, tps=tps, shift=shift),
        grid=(t // tm, nj),
        in_specs=[
            pl.BlockSpec((tm, D_MODEL), lambda m, j: (m, 0)),
            *w_specs,
            pl.BlockSpec((SHORT_K, tn), lambda m, j: (0, j)),
            pl.BlockSpec((nseq, SHORT_K - 1, tn), lambda m, j: (m // tps, 0, j)),
        ],
        out_specs=[pl.BlockSpec((tm, tn), lambda m, j: (m, j)),
                   pl.BlockSpec((nseq, SHORT_K - 1, tn), lambda m, j: (m, 0, j))],
        out_shape=[jax.ShapeDtypeStruct((t, D_SHORT), out_dtype),
                   jax.ShapeDtypeStruct(((t // tm) * nseq, SHORT_K - 1, D_SHORT), F32)],
        scratch_shapes=scratch,
        compiler_params=pltpu.CompilerParams(
            dimension_semantics=("arbitrary", "arbitrary"), vmem_limit_bytes=VMEM_LIMIT),
        name="short_conv",
    )(h, *([w_t] * 8), conv_w, state)
    tails = tails.reshape(t // (tm * tps), tps, nseq, SHORT_K - 1, D_SHORT)[:, -1]
    return y, tails.reshape(state.shape)


def _dot_split(x, w, pieces):
    out = None
    r = x
    for _ in range(pieces):
        p = r.astype(BF16)
        d = jnp.dot(p, w, preferred_element_type=F32)
        out = d if out is None else out + d
        r = r - p.astype(F32)
    return out


def _cumsum_rows(x, causal):
    tri = causal.astype(BF16)
    return sum(jnp.dot(tri, p, preferred_element_type=F32) for p in _split3(x))


def _gated_norm_store(y, zg_ref, gnw_ref, y_ref):
    y = y * zg_ref[...].astype(F32)
    for g in range(GROUPS):
        cs = slice(g * GROUP_W, (g + 1) * GROUP_W)
        blk = y[:, cs]
        ms = jnp.mean(blk * blk, axis=-1, keepdims=True)
        y_ref[:, cs] = (blk * lax.rsqrt(ms + EPS) * gnw_ref[:, cs]).astype(y_ref.dtype)


def _ssd_chunk_body(zg_ref, xc_ref, dt_ref, h0_ref, alr_ref, expand_ref, dexp_ref, gnw_ref,
                    y_ref, hout_ref, hst_ref, ysc_ref, *, lc, nc):
    c = pl.program_id(1)

    @pl.when(c == 0)
    def _():
        hst_ref[...] = h0_ref[0].T

    row_i = lax.broadcasted_iota(jnp.int32, (lc, lc), 0)
    col_i = lax.broadcasted_iota(jnp.int32, (lc, lc), 1)
    causal = row_i >= col_i
    lane_lo = lax.broadcasted_iota(jnp.int32, (lc, LANES), 1) < HEAD_DIM

    dt = dt_ref[...]
    la = dt * (-jnp.exp(alr_ref[...]))
    a_cum = _cumsum_rows(la, causal)
    a_last = a_cum[lc - 1:lc, :]
    factors = jnp.concatenate([jnp.exp(a_cum), jnp.exp(a_last - a_cum) * dt], axis=0).astype(BF16)
    a2 = a_cum * LOG2E
    r_t = (a2 - jnp.log2(dt)).T

    for g in range(GROUPS):
        bg = xc_ref[:, D_SSM + g * D_STATE:D_SSM + (g + 1) * D_STATE]
        cg = xc_ref[:, D_SSM + GN + g * D_STATE:D_SSM + GN + (g + 1) * D_STATE].astype(BF16)
        cols = slice(g * GROUP_W, (g + 1) * GROUP_W)
        h_prev = hst_ref[:, cols]
        cb = lax.dot_general(cg, bg.astype(BF16), NT_DIMS, preferred_element_type=F32)
        y_off = jnp.dot(cg, h_prev.astype(BF16), preferred_element_type=F32)
        xs_g = xc_ref[:, cols]
        both = jnp.dot(factors, expand_ref[:, cols], preferred_element_type=F32)
        e_exp, w_exp = both[:lc], both[lc:]
        for pr in range(HPG // 2):
            mats = []
            for hh in (g * HPG + 2 * pr, g * HPG + 2 * pr + 1):
                seg = a2[:, hh:hh + 1] - r_t[hh:hh + 1, :]
                dec = jnp.exp2(jnp.where(causal, seg, -jnp.inf))
                mats.append((cb * dec).astype(BF16))
            xp = xs_g[:, pr * LANES:(pr + 1) * LANES].astype(BF16)
            rhs = jnp.concatenate([jnp.where(lane_lo, xp, 0), jnp.where(lane_lo, 0, xp)], axis=0)
            y_diag = jnp.dot(jnp.concatenate(mats, axis=1), rhs, preferred_element_type=F32)
            pc = slice(g * GROUP_W + pr * LANES, g * GROUP_W + (pr + 1) * LANES)
            ysc_ref[:, pc] = y_diag + (y_off * e_exp)[:, pr * LANES:(pr + 1) * LANES]
        xw = (xs_g * w_exp).astype(BF16)
        s_t = jnp.dot(bg.T.astype(BF16), xw, preferred_element_type=F32)
        hst_ref[:, cols] = h_prev * e_exp[lc - 1:lc, :] + s_t

    y = ysc_ref[...] + xc_ref[:, 0:D_SSM] * dexp_ref[...]
    _gated_norm_store(y, zg_ref, gnw_ref, y_ref)

    @pl.when(c == nc - 1)
    def _():
        hout_ref[0] = hst_ref[...].T


def _ssd_short_body(zg_ref, xc_ref, dt_ref, h0_ref, alr_ref, expand_ref, exps_ref, dexp_ref, gnw_ref,
                    y_ref, hout_ref, ysc_ref, *, lc, spb):
    rows = spb * lc
    hl = HEADS * lc
    tloc = lax.broadcasted_iota(jnp.int32, (rows, LANES), 0) % lc
    dt = dt_ref[...]
    a_cum = dt * (-jnp.exp(alr_ref[...]))
    k = 1
    while k < lc:
        a_cum = a_cum + jnp.where(tloc >= k, pltpu.roll(a_cum, k, axis=0), 0.0)
        k *= 2
    a_last = a_cum.reshape(spb, lc, LANES)[:, lc - 1:lc, :]
    a_last_b = jnp.broadcast_to(a_last, (spb, lc, LANES)).reshape(rows, LANES)
    expand = expand_ref[...]
    e_exp = _dot_split(jnp.exp(a_cum), expand, 2)
    w_exp = _dot_split(jnp.exp(a_last_b - a_cum) * dt, expand, 2)
    cd_t = jnp.exp(a_last.reshape(spb, LANES)).T

    z2 = _dot_split(jnp.concatenate([a_cum, dt], axis=0), exps_ref[...], 3)
    zc, zd = z2[:rows], z2[rows:]
    trow = lax.broadcasted_iota(jnp.int32, (rows, hl), 0) % lc
    tsrc = lax.broadcasted_iota(jnp.int32, (rows, hl), 1) % lc
    diag = trow == tsrc

    def per_source(z):
        d = jnp.where(diag, z, 0.0).reshape(spb, lc, hl).sum(axis=1, keepdims=True)
        return jnp.broadcast_to(d, (spb, lc, hl)).reshape(rows, hl)

    dec = jnp.exp(jnp.where(trow >= tsrc, zc - per_source(zc), -jnp.inf)) * per_source(zd)

    pair_w = HPG * lc
    blockdiag = (lax.broadcasted_iota(jnp.int32, (pair_w, GROUP_W), 0) // lc
                 == lax.broadcasted_iota(jnp.int32, (pair_w, GROUP_W), 1) // HEAD_DIM)

    for q in range(spb):
        tok = slice(q * lc, (q + 1) * lc)
        for g in range(GROUPS):
            bg = xc_ref[tok, D_SSM + g * D_STATE:D_SSM + (g + 1) * D_STATE]
            cg = xc_ref[tok, D_SSM + GN + g * D_STATE:D_SSM + GN + (g + 1) * D_STATE].astype(BF16)
            cols = slice(g * GROUP_W, (g + 1) * GROUP_W)
            h_prev = h0_ref[q, cols, :]
            xs_g = xc_ref[tok, cols]
            b_tiled = jnp.concatenate([bg] * HPG, axis=0).astype(BF16)
            cb = lax.dot_general(cg, b_tiled, NT_DIMS, preferred_element_type=F32)
            m = (cb * dec[tok, g * pair_w:(g + 1) * pair_w]).astype(BF16)
            x_diag = jnp.where(blockdiag, jnp.concatenate([xs_g] * HPG, axis=0), 0.0).astype(BF16)
            y_diag = jnp.dot(m, x_diag, preferred_element_type=F32)
            y_off = lax.dot_general(cg, h_prev.astype(BF16), NT_DIMS, preferred_element_type=F32)
            ysc_ref[tok, cols] = y_diag + y_off * e_exp[tok, cols]
            xw_t = (xs_g * w_exp[tok, cols]).T.astype(BF16)
            s_g = jnp.dot(xw_t, bg.astype(BF16), preferred_element_type=F32)
            cd = jnp.concatenate(
                [jnp.broadcast_to(cd_t[g * HPG + jh:g * HPG + jh + 1, q:q + 1], (HEAD_DIM, D_STATE))
                 for jh in range(HPG)], axis=0)
            hout_ref[q, cols, :] = h_prev * cd + s_g

    y = ysc_ref[...] + xc_ref[:, 0:D_SSM] * dexp_ref[...]
    _gated_norm_store(y, zg_ref, gnw_ref, y_ref)


def _ssd(zg, xc, dt, h0, alr, expand, exps, dexp, gnw, out_dtype, *, nb, nc, lc):
    t = zg.shape[0]
    par = lambda r, width: pl.BlockSpec((r, width), lambda b, c: (0, 0))
    params = [par(1, LANES), par(LANES, D_SSM)]
    operands = [alr, expand]
    if lc == SSD_CHUNK:
        spb = 1
        body = functools.partial(_ssd_chunk_body, lc=lc, nc=nc)
        scratch = [pltpu.VMEM((D_STATE, D_SSM), F32)]
    else:
        assert nc == 1
        spb = SHORT_SEQS_PER_STEP
        body = functools.partial(_ssd_short_body, lc=lc, spb=spb)
        scratch = []
        params.append(par(LANES, HEADS * lc))
        operands.append(exps)
    rows = spb * lc
    tok = lambda width: pl.BlockSpec((rows, width), lambda b, c: (b * nc + c, 0))
    h_spec = pl.BlockSpec((spb, D_SSM, D_STATE), lambda b, c: (b, 0, 0))
    return pl.pallas_call(
        body,
        grid=(nb // spb, nc),
        in_specs=[tok(D_SSM), tok(CONV_DIM), tok(LANES), h_spec, *params, par(1, D_SSM), par(1, D_SSM)],
        out_specs=[tok(D_SSM), h_spec],
        out_shape=[jax.ShapeDtypeStruct((t, D_SSM), out_dtype),
                   jax.ShapeDtypeStruct((nb, D_SSM, D_STATE), F32)],
        scratch_shapes=scratch + [pltpu.VMEM((rows, D_SSM), F32)],
        compiler_params=pltpu.CompilerParams(
            dimension_semantics=("arbitrary", "arbitrary"), vmem_limit_bytes=VMEM_LIMIT),
        name="ssd_scan",
    )(zg, xc, dt, h0, *operands, dexp, gnw)


def _out_body(ys_ref, yc_ref, x_ref, w1_ref, w2_ref, fnw_ref, o_ref):
    acc = jnp.dot(ys_ref[...].astype(BF16), w1_ref[...], preferred_element_type=F32)
    acc = acc + jnp.dot(yc_ref[...].astype(BF16), w2_ref[...], preferred_element_type=F32)
    r = x_ref[...] + acc
    var = jnp.mean(r * r, axis=-1, keepdims=True)
    o_ref[...] = r * lax.rsqrt(var + EPS) * fnw_ref[...]


def _out(ys, yc, x, w_o, fnw, *, tm):
    t = x.shape[0]
    row = lambda width: pl.BlockSpec((tm, width), lambda m: (m, 0))
    return pl.pallas_call(
        _out_body,
        grid=(t // tm,),
        in_specs=[row(D_SSM), row(D_SHORT), row(D_MODEL),
                  pl.BlockSpec((D_SSM, D_MODEL), lambda m: (0, 0)),
                  pl.BlockSpec((D_SHORT, D_MODEL), lambda m: (1, 0)),
                  pl.BlockSpec((1, D_MODEL), lambda m: (0, 0))],
        out_specs=row(D_MODEL),
        out_shape=jax.ShapeDtypeStruct((t, D_MODEL), F32),
        compiler_params=pltpu.CompilerParams(
            dimension_semantics=("arbitrary",), vmem_limit_bytes=VMEM_LIMIT),
        name="out_proj",
    )(ys, yc, x, w_o, w_o, fnw)


def _mixer(x2d, h0, conv_state, short_state, p, *, nb, seqlen, act_dtype, tm, nseq, tps):
    lc = min(SSD_CHUNK, seqlen)
    nc = seqlen // lc
    h, zg, dt = _proj_z(x2d, p["nw"], p["w_t"], p["dtb"], act_dtype, tm=tm)
    xc, conv_new = _proj_xc(h, p["w_t"], p["conv_w"], p["conv_b"], conv_state,
                            tm=tm, nseq=nseq, tps=tps)
    y_c, short_new = _short(h, p["w_t"], p["conv_short_w"], short_state, act_dtype,
                            tm=tm, tn=256, nseq=nseq, tps=tps)
    y_s, h_new = _ssd(zg, xc, dt, h0, p["alr"], p["expand"], p["expand_pairs"](lc), p["dexp"], p["gnw"],
                      act_dtype, nb=nb, nc=nc, lc=lc)
    return y_s, y_c, h_new, conv_new, short_new


def kernel(x_prompt, x_sample, state_ssm, state_conv_ssd, state_conv_short, norm_w, w_in, conv_ssd_w,
           conv_ssd_b, dt_bias, a_log, d_skip, ssd_norm_w, conv_short_w, w_out, final_norm_w):
    depth = norm_w.shape[0]
    assert depth == 1, "the output projection fuses the final rmsnorm, valid for a single layer"
    bp, lp, _ = x_prompt.shape
    bs, ls, _ = x_sample.shape
    hp = x_prompt.reshape(bp * lp, D_MODEL)
    hs = x_sample.reshape(bs * ls, D_MODEL)
    pad_h = LANES - HEADS
    expand = (jnp.arange(LANES)[:, None] == jnp.arange(D_SSM)[None, :] // HEAD_DIM).astype(BF16)
    outs = [[] for _ in range(6)]
    for layer in range(depth):
        w_t = w_in[layer].T
        p = dict(
            nw=norm_w[layer][None, :],
            w_t=w_t,
            conv_short_w=conv_short_w[layer],
            conv_w=conv_ssd_w[layer],
            conv_b=conv_ssd_b[layer][None, :],
            dtb=jnp.pad(dt_bias[layer], (0, pad_h))[None, :],
            alr=jnp.pad(a_log[layer], (0, pad_h))[None, :],
            expand=expand,
            expand_pairs=lambda lc: (jnp.arange(LANES)[:, None]
                                     == jnp.arange(HEADS * lc)[None, :] // lc).astype(BF16),
            dexp=jnp.repeat(d_skip[layer], HEAD_DIM)[None, :],
            gnw=ssd_norm_w[layer][None, :],
        )
        w_o = w_out[layer].astype(BF16)

        tm_p = 1024
        ys, yc, a1, a2, a3 = _mixer(
            hp, jnp.zeros((bp, D_SSM, D_STATE), F32), jnp.zeros((bp, CONV_K - 1, CONV_DIM), F32),
            jnp.zeros((bp, SHORT_K - 1, D_SHORT), F32), p,
            nb=bp, seqlen=lp, act_dtype=BF16, tm=tm_p, nseq=1, tps=lp // tm_p)
        ys2, yc2, s1, s2, s3 = _mixer(
            hs, state_ssm[layer].reshape(bs, D_SSM, D_STATE), state_conv_ssd[layer],
            state_conv_short[layer], p,
            nb=bs, seqlen=ls, act_dtype=F32, tm=bs * ls, nseq=bs, tps=1)
        fw = final_norm_w[None, :]
        hp = _out(ys, yc, hp, w_o, fw, tm=256)
        hs = _out(ys2, yc2, hs, w_o, fw, tm=256)
        for lst, val in zip(outs, (a1.reshape(bp, HEADS, HEAD_DIM, D_STATE), a2, a3,
                                   s1.reshape(bs, HEADS, HEAD_DIM, D_STATE), s2, s3)):
            lst.append(val)
    return (hp.reshape(bp, lp, D_MODEL), hs.reshape(bs, ls, D_MODEL),
            *(jnp.stack(v) for v in outs))
```

```python
import functools

import jax
import jax.numpy as jnp
from jax import lax
from jax.experimental import pallas as pl
from jax.experimental.pallas import tpu as pltpu

F32 = jnp.float32
BF16 = jnp.bfloat16

D_MODEL = 2048
D_SSM = 2048
D_SHORT = 2048
HEADS = 32
HEAD_DIM = 64
D_STATE = 128
GROUPS = 4
HPG = HEADS // GROUPS
GROUP_W = HPG * HEAD_DIM
GN = GROUPS * D_STATE
CONV_DIM = D_SSM + 2 * GN
CONV_K = 4
SHORT_K = 3
SSD_CHUNK = 128
EPS = 1e-6
LANES = 128
SUBLANES = 8
VMEM_LIMIT = 56 * 1024 * 1024

TN = 512
Z_TILES = D_SSM // TN
XC_TILES = CONV_DIM // TN
MAIN_VALID = D_SSM + CONV_DIM + HEADS
DT_BLOCK = (D_SSM + CONV_DIM) // LANES

NT_DIMS = (((1,), (1,)), ((), ()))
LOG2E = 1.4426950408889634
SHORT_SEQS_PER_STEP = 8


def _silu(x):
    half = 0.5 * x
    return half + half * jnp.tanh(half)


def _softplus(x):
    return jnp.maximum(x, 0.0) + jnp.log1p(jnp.exp(-jnp.abs(x)))


def _split3(x):
    hi = x.astype(BF16)
    r = x - hi.astype(F32)
    mid = r.astype(BF16)
    lo = (r - mid.astype(F32)).astype(BF16)
    return hi, mid, lo


def _proj_z_body(x_ref, nw_ref, w_ref, wdt_ref, dtb_ref, h_ref, zg_ref, dt_ref, *, tm):
    @pl.when(pl.program_id(1) == 0)
    def _():
        blk = 32
        nw = nw_ref[...]

        def body(i, carry):
            r = pl.ds(pl.multiple_of(i * blk, blk), blk)
            xf = x_ref[r, :]
            var = jnp.mean(xf * xf, axis=-1, keepdims=True)
            h_ref[r, :] = (xf * lax.rsqrt(var + EPS) * nw).astype(BF16)
            return carry

        lax.fori_loop(0, tm // blk, body, 0, unroll=8)
        dt_raw = lax.dot_general(h_ref[...], wdt_ref[...].astype(BF16), NT_DIMS,
                                 preferred_element_type=F32)
        dt_ref[...] = _softplus(dt_raw + dtb_ref[...])

    acc = lax.dot_general(h_ref[...], w_ref[...].astype(BF16), NT_DIMS, preferred_element_type=F32)
    zg_ref[...] = _silu(acc).astype(zg_ref.dtype)


def _proj_z(x, nw, w_t, dtb, zg_dtype, *, tm):
    t = x.shape[0]
    return pl.pallas_call(
        functools.partial(_proj_z_body, tm=tm),
        grid=(t // tm, Z_TILES),
        in_specs=[
            pl.BlockSpec((tm, D_MODEL), lambda m, j: (m, 0)),
            pl.BlockSpec((1, D_MODEL), lambda m, j: (0, 0)),
            pl.BlockSpec((TN, D_MODEL), lambda m, j: (j, 0)),
            pl.BlockSpec((LANES, D_MODEL), lambda m, j: (DT_BLOCK, 0)),
            pl.BlockSpec((1, LANES), lambda m, j: (0, 0)),
        ],
        out_specs=[pl.BlockSpec((tm, D_MODEL), lambda m, j: (m, 0)),
                   pl.BlockSpec((tm, TN), lambda m, j: (m, j)),
                   pl.BlockSpec((tm, LANES), lambda m, j: (m, 0))],
        out_shape=[jax.ShapeDtypeStruct((t, D_MODEL), BF16),
                   jax.ShapeDtypeStruct((t, D_SSM), zg_dtype),
                   jax.ShapeDtypeStruct((t, LANES), F32)],
        compiler_params=pltpu.CompilerParams(
            dimension_semantics=("arbitrary", "arbitrary"), vmem_limit_bytes=VMEM_LIMIT),
        name="in_proj_z",
    )(x, nw, w_t, w_t, dtb)


def _load_history(pad_ref, st_ref, carry, j, *, first, hist, tps):
    def from_state():
        pad_ref[:, 0:SUBLANES, :] = jnp.zeros((pad_ref.shape[0], SUBLANES, pad_ref.shape[2]), F32)
        pad_ref[:, SUBLANES - hist:SUBLANES, :] = st_ref[...]

    if tps > 1:
        (carry_ref,) = carry
        pl.when(first)(from_state)

        @pl.when(jnp.logical_not(first))
        def _():
            pad_ref[:, 0:SUBLANES, :] = carry_ref[j]
    else:
        from_state()


def _causal_conv(pad_ref, cw_ref, taps):
    xe = pad_ref[...]
    if taps == 4:
        s1 = pltpu.roll(xe, 1, axis=1)
        near = xe * cw_ref[3:4, :] + s1 * cw_ref[2:3, :]
        far = xe * cw_ref[1:2, :] + s1 * cw_ref[0:1, :]
        return near[:, SUBLANES:, :] + pltpu.roll(far, 2, axis=1)[:, SUBLANES:, :]
    conv = xe[:, SUBLANES:, :] * cw_ref[taps - 1:taps, :]
    for s in range(1, taps):
        shifted = pltpu.roll(xe, s, axis=1)[:, SUBLANES:, :]
        conv = conv + shifted * cw_ref[taps - 1 - s:taps - s, :]
    return conv


def _proj_xc_body(h_ref, w_ref, cw_ref, cb_ref, st_ref, xc_ref, tails_ref, pad_ref, *carry,
                  tm, nseq, tps):
    j = pl.program_id(1)
    lt = tm // nseq
    _load_history(pad_ref, st_ref, carry, j, first=(pl.program_id(0) % tps) == 0, hist=CONV_K - 1,
                  tps=tps)
    acc = lax.dot_general(h_ref[...], w_ref[...].astype(BF16), NT_DIMS, preferred_element_type=F32)
    pad_ref[:, SUBLANES:SUBLANES + lt, :] = acc.reshape(nseq, lt, TN)
    conv = _causal_conv(pad_ref, cw_ref, CONV_K)
    xc_ref[...] = _silu(conv + cb_ref[...]).reshape(tm, TN)
    tails_ref[...] = pad_ref[:, lt + SUBLANES - (CONV_K - 1):lt + SUBLANES, :]
    if tps > 1:
        carry[0][j] = pad_ref[:, lt:lt + SUBLANES, :]


def _proj_xc(h, w_t, conv_w, conv_b, conv_state, *, tm, nseq, tps):
    t = h.shape[0]
    lt = tm // nseq
    scratch = [pltpu.VMEM((nseq, SUBLANES + lt, TN), F32)]
    if tps > 1:
        scratch.append(pltpu.VMEM((XC_TILES, nseq, SUBLANES, TN), F32))
    xc, tails = pl.pallas_call(
        functools.partial(_proj_xc_body, tm=tm, nseq=nseq, tps=tps),
        grid=(t // tm, XC_TILES),
        in_specs=[
            pl.BlockSpec((tm, D_MODEL), lambda m, j: (m, 0)),
            pl.BlockSpec((TN, D_MODEL), lambda m, j: (Z_TILES + j, 0)),
            pl.BlockSpec((CONV_K, TN), lambda m, j: (0, j)),
            pl.BlockSpec((1, TN), lambda m, j: (0, j)),
            pl.BlockSpec((nseq, CONV_K - 1, TN), lambda m, j: (m // tps, 0, j)),
        ],
        out_specs=[pl.BlockSpec((tm, TN), lambda m, j: (m, j)),
                   pl.BlockSpec((nseq, CONV_K - 1, TN), lambda m, j: (m, 0, j))],
        out_shape=[jax.ShapeDtypeStruct((t, CONV_DIM), F32),
                   jax.ShapeDtypeStruct(((t // tm) * nseq, CONV_K - 1, CONV_DIM), F32)],
        scratch_shapes=scratch,
        compiler_params=pltpu.CompilerParams(
            dimension_semantics=("arbitrary", "arbitrary"), vmem_limit_bytes=VMEM_LIMIT),
        name="in_proj_xc",
    )(h, w_t, conv_w, conv_b, conv_state)
    tails = tails.reshape(t // (tm * tps), tps, nseq, CONV_K - 1, CONV_DIM)[:, -1]
    return xc, tails.reshape(conv_state.shape)


def _short_body(h_ref, *refs, tm, tn, nseq, tps, shift):
    w_refs, (cw_ref, st_ref, y_ref, ns_ref, wbf_ref, pad_ref), carry = refs[:8], refs[8:14], refs[14:]
    m = pl.program_id(1)
    lt = tm // nseq

    @pl.when(m == 0)
    def _():
        for k in range(4):
            a_ref, b_ref = w_refs[2 * k], w_refs[2 * k + 1]
            wbf_ref[k] = jnp.concatenate([a_ref[shift:, :], b_ref[...]], axis=0).astype(BF16)

    _load_history(pad_ref, st_ref, carry, 0, first=(m % tps) == 0, hist=SHORT_K - 1, tps=tps)
    h = h_ref[...]
    band = lambda k: lax.dot_general(h, wbf_ref[k], NT_DIMS, preferred_element_type=F32)
    c = band(2)
    v = band(3)
    pad_ref[:, SUBLANES:SUBLANES + lt, :] = (c * v).reshape(nseq, lt, tn)
    conv = _causal_conv(pad_ref, cw_ref, SHORT_K)
    b = band(1)
    z = band(0)
    y = b * conv.reshape(tm, tn) * _silu(z)
    y_ref[...] = y.astype(y_ref.dtype)
    ns_ref[...] = pad_ref[:, lt + SUBLANES - (SHORT_K - 1):lt + SUBLANES, :]
    if tps > 1:
        carry[0][0] = pad_ref[:, lt:lt + SUBLANES, :]


def _short(h, w_t, conv_w, state, out_dtype, *, tm, tn, nseq, tps):
    t = h.shape[0]
    nj = D_SHORT // tn
    lt = tm // nseq
    base = (MAIN_VALID // tn) * tn
    shift = MAIN_VALID - base
    assert shift % SUBLANES == 0 and tn % shift == 0 and base % shift == 0
    scratch = [pltpu.VMEM((4, tn, D_MODEL), BF16), pltpu.VMEM((nseq, SUBLANES + lt, tn), F32)]
    if tps > 1:
        scratch.append(pltpu.VMEM((1, nseq, SUBLANES, tn), F32))
    w_specs = []
    for k in range(4):
        r0 = base + k * D_SHORT
        w_specs.append(pl.BlockSpec((tn, D_MODEL), lambda j, m, r0=r0: (r0 // tn + j, 0)))
        w_specs.append(pl.BlockSpec((shift, D_MODEL), lambda j, m, r0=r0: ((r0 + tn * (j + 1)) // shift, 0)))

    y, tails = pl.pallas_call(
        functools.partial(_short_body, tm=tm, tn=tn, nseq=nseq, tps=tps, shift=shift),
        grid=(nj, t // tm),
        in_specs=[
            pl.BlockSpec((tm, D_MODEL), lambda j, m: (m, 0)),
            *w_specs,
            pl.BlockSpec((SHORT_K, tn), lambda j, m: (0, j)),
            pl.BlockSpec((nseq, SHORT_K - 1, tn), lambda j, m: (m // tps, 0, j)),
        ],
        out_specs=[pl.BlockSpec((tm, tn), lambda j, m: (m, j)),
                   pl.BlockSpec((nseq, SHORT_K - 1, tn), lambda j, m: (m, 0, j))],
        out_shape=[jax.ShapeDtypeStruct((t, D_SHORT), out_dtype),
                   jax.ShapeDtypeStruct(((t // tm) * nseq, SHORT_K - 1, D_SHORT), F32)],
        scratch_shapes=scratch,
        compiler_params=pltpu.CompilerParams(
            dimension_semantics=("arbitrary", "arbitrary"), vmem_limit_bytes=VMEM_LIMIT),
        name="short_conv",
    )(h, *([w_t] * 8), conv_w, state)
    tails = tails.reshape(t // (tm * tps), tps, nseq, SHORT_K - 1, D_SHORT)[:, -1]
    return y, tails.reshape(state.shape)


def _dot_split(x, w, pieces):
    out = None
    r = x
    for _ in range(pieces):
        p = r.astype(BF16)
        d = jnp.dot(p, w, preferred_element_type=F32)
        out = d if out is None else out + d
        r = r - p.astype(F32)
    return out


def _cumsum_rows(x, causal):
    tri = causal.astype(BF16)
    return sum(jnp.dot(tri, p, preferred_element_type=F32) for p in _split3(x))


def _gated_norm_store(y, zg_ref, gnw_ref, y_ref):
    y = y * zg_ref[...].astype(F32)
    for g in range(GROUPS):
        cs = slice(g * GROUP_W, (g + 1) * GROUP_W)
        blk = y[:, cs]
        ms = jnp.mean(blk * blk, axis=-1, keepdims=True)
        y_ref[:, cs] = (blk * lax.rsqrt(ms + EPS) * gnw_ref[:, cs]).astype(y_ref.dtype)


def _ssd_chunk_body(zg_ref, xc_ref, dt_ref, h0_ref, alr_ref, expand_ref, dexp_ref, gnw_ref,
                    y_ref, hout_ref, hst_ref, ysc_ref, *, lc, nc):
    c = pl.program_id(1)

    @pl.when(c == 0)
    def _():
        hst_ref[...] = h0_ref[0].T

    row_i = lax.broadcasted_iota(jnp.int32, (lc, lc), 0)
    col_i = lax.broadcasted_iota(jnp.int32, (lc, lc), 1)
    causal = row_i >= col_i
    lane_lo = lax.broadcasted_iota(jnp.int32, (lc, LANES), 1) < HEAD_DIM

    dt = dt_ref[...]
    la = dt * (-jnp.exp(alr_ref[...]))
    a_cum = _cumsum_rows(la, causal)
    a_last = a_cum[lc - 1:lc, :]
    factors = jnp.concatenate([jnp.exp(a_cum), jnp.exp(a_last - a_cum) * dt], axis=0).astype(BF16)
    a2 = a_cum * LOG2E
    r_t = (a2 - jnp.log2(dt)).T

    for g in range(GROUPS):
        bg = xc_ref[:, D_SSM + g * D_STATE:D_SSM + (g + 1) * D_STATE]
        cg = xc_ref[:, D_SSM + GN + g * D_STATE:D_SSM + GN + (g + 1) * D_STATE].astype(BF16)
        cols = slice(g * GROUP_W, (g + 1) * GROUP_W)
        h_prev = hst_ref[:, cols]
        cb = lax.dot_general(cg, bg.astype(BF16), NT_DIMS, preferred_element_type=F32)
        y_off = jnp.dot(cg, h_prev.astype(BF16), preferred_element_type=F32)
        xs_g = xc_ref[:, cols]
        both = jnp.dot(factors, expand_ref[:, cols], preferred_element_type=F32)
        e_exp, w_exp = both[:lc], both[lc:]
        for pr in range(HPG // 2):
            mats = []
            for hh in (g * HPG + 2 * pr, g * HPG + 2 * pr + 1):
                seg = a2[:, hh:hh + 1] - r_t[hh:hh + 1, :]
                dec = jnp.exp2(jnp.where(causal, seg, -jnp.inf))
                mats.append((cb * dec).astype(BF16))
            xp = xs_g[:, pr * LANES:(pr + 1) * LANES].astype(BF16)
            rhs = jnp.concatenate([jnp.where(lane_lo, xp, 0), jnp.where(lane_lo, 0, xp)], axis=0)
            y_diag = jnp.dot(jnp.concatenate(mats, axis=1), rhs, preferred_element_type=F32)
            pc = slice(g * GROUP_W + pr * LANES, g * GROUP_W + (pr + 1) * LANES)
            ysc_ref[:, pc] = y_diag + (y_off * e_exp)[:, pr * LANES:(pr + 1) * LANES]
        xw = (xs_g * w_exp).astype(BF16)
        s_t = jnp.dot(bg.T.astype(BF16), xw, preferred_element_type=F32)
        hst_ref[:, cols] = h_prev * e_exp[lc - 1:lc, :] + s_t

    y = ysc_ref[...] + xc_ref[:, 0:D_SSM] * dexp_ref[...]
    _gated_norm_store(y, zg_ref, gnw_ref, y_ref)

    @pl.when(c == nc - 1)
    def _():
        hout_ref[0] = hst_ref[...].T


def _ssd_short_body(zg_ref, xc_ref, dt_ref, h0_ref, alr_ref, expand_ref, exps_ref, dexp_ref, gnw_ref,
                    y_ref, hout_ref, ysc_ref, *, lc, spb):
    rows = spb * lc
    hl = HEADS * lc
    tloc = lax.broadcasted_iota(jnp.int32, (rows, LANES), 0) % lc
    dt = dt_ref[...]
    a_cum = dt * (-jnp.exp(alr_ref[...]))
    k = 1
    while k < lc:
        a_cum = a_cum + jnp.where(tloc >= k, pltpu.roll(a_cum, k, axis=0), 0.0)
        k *= 2
    a_last = a_cum.reshape(spb, lc, LANES)[:, lc - 1:lc, :]
    a_last_b = jnp.broadcast_to(a_last, (spb, lc, LANES)).reshape(rows, LANES)
    expand = expand_ref[...]
    e_exp = _dot_split(jnp.exp(a_cum), expand, 2)
    w_exp = _dot_split(jnp.exp(a_last_b - a_cum) * dt, expand, 2)
    cd_t = jnp.exp(a_last.reshape(spb, LANES)).T

    z2 = _dot_split(jnp.concatenate([a_cum, dt], axis=0), exps_ref[...], 3)
    zc, zd = z2[:rows], z2[rows:]
    trow = lax.broadcasted_iota(jnp.int32, (rows, hl), 0) % lc
    tsrc = lax.broadcasted_iota(jnp.int32, (rows, hl), 1) % lc
    diag = trow == tsrc

    def per_source(z):
        d = jnp.where(diag, z, 0.0).reshape(spb, lc, hl).sum(axis=1, keepdims=True)
        return jnp.broadcast_to(d, (spb, lc, hl)).reshape(rows, hl)

    dec = jnp.exp(jnp.where(trow >= tsrc, zc - per_source(zc), -jnp.inf)) * per_source(zd)

    pair_w = HPG * lc
    blockdiag = (lax.broadcasted_iota(jnp.int32, (pair_w, GROUP_W), 0) // lc
                 == lax.broadcasted_iota(jnp.int32, (pair_w, GROUP_W), 1) // HEAD_DIM)

    for q in range(spb):
        tok = slice(q * lc, (q + 1) * lc)
        for g in range(GROUPS):
            bg = xc_ref[tok, D_SSM + g * D_STATE:D_SSM + (g + 1) * D_STATE]
            cg = xc_ref[tok, D_SSM + GN + g * D_STATE:D_SSM + GN + (g + 1) * D_STATE].astype(BF16)
            cols = slice(g * GROUP_W, (g + 1) * GROUP_W)
            h_prev = h0_ref[q, cols, :]
            xs_g = xc_ref[tok, cols]
            b_tiled = jnp.concatenate([bg] * HPG, axis=0).astype(BF16)
            cb = lax.dot_general(cg, b_tiled, NT_DIMS, preferred_element_type=F32)
            m = (cb * dec[tok, g * pair_w:(g + 1) * pair_w]).astype(BF16)
            x_diag = jnp.where(blockdiag, jnp.concatenate([xs_g] * HPG, axis=0), 0.0).astype(BF16)
            y_diag = jnp.dot(m, x_diag, preferred_element_type=F32)
            y_off = lax.dot_general(cg, h_prev.astype(BF16), NT_DIMS, preferred_element_type=F32)
            ysc_ref[tok, cols] = y_diag + y_off * e_exp[tok, cols]
            xw_t = (xs_g * w_exp[tok, cols]).T.astype(BF16)
            s_g = jnp.dot(xw_t, bg.astype(BF16), preferred_element_type=F32)
            cd = jnp.concatenate(
                [jnp.broadcast_to(cd_t[g * HPG + jh:g * HPG + jh + 1, q:q + 1], (HEAD_DIM, D_STATE))
                 for jh in range(HPG)], axis=0)
            hout_ref[q, cols, :] = h_prev * cd + s_g

    y = ysc_ref[...] + xc_ref[:, 0:D_SSM] * dexp_ref[...]
    _gated_norm_store(y, zg_ref, gnw_ref, y_ref)


def _ssd(zg, xc, dt, h0, alr, expand, exps, dexp, gnw, out_dtype, *, nb, nc, lc):
    t = zg.shape[0]
    par = lambda r, width: pl.BlockSpec((r, width), lambda b, c: (0, 0))
    params = [par(1, LANES), par(LANES, D_SSM)]
    operands = [alr, expand]
    if lc == SSD_CHUNK:
        spb = 1
        body = functools.partial(_ssd_chunk_body, lc=lc, nc=nc)
        scratch = [pltpu.VMEM((D_STATE, D_SSM), F32)]
    else:
        assert nc == 1
        spb = SHORT_SEQS_PER_STEP
        body = functools.partial(_ssd_short_body, lc=lc, spb=spb)
        scratch = []
        params.append(par(LANES, HEADS * lc))
        operands.append(exps)
    rows = spb * lc
    tok = lambda width: pl.BlockSpec((rows, width), lambda b, c: (b * nc + c, 0))
    h_spec = pl.BlockSpec((spb, D_SSM, D_STATE), lambda b, c: (b, 0, 0))
    return pl.pallas_call(
        body,
        grid=(nb // spb, nc),
        in_specs=[tok(D_SSM), tok(CONV_DIM), tok(LANES), h_spec, *params, par(1, D_SSM), par(1, D_SSM)],
        out_specs=[tok(D_SSM), h_spec],
        out_shape=[jax.ShapeDtypeStruct((t, D_SSM), out_dtype),
                   jax.ShapeDtypeStruct((nb, D_SSM, D_STATE), F32)],
        scratch_shapes=scratch + [pltpu.VMEM((rows, D_SSM), F32)],
        compiler_params=pltpu.CompilerParams(
            dimension_semantics=("arbitrary", "arbitrary"), vmem_limit_bytes=VMEM_LIMIT),
        name="ssd_scan",
    )(zg, xc, dt, h0, *operands, dexp, gnw)


def _out_body(ys_ref, yc_ref, x_ref, w1_ref, w2_ref, fnw_ref, o_ref):
    acc = jnp.dot(ys_ref[...].astype(BF16), w1_ref[...], preferred_element_type=F32)
    acc = acc + jnp.dot(yc_ref[...].astype(BF16), w2_ref[...], preferred_element_type=F32)
    r = x_ref[...] + acc
    var = jnp.mean(r * r, axis=-1, keepdims=True)
    o_ref[...] = r * lax.rsqrt(var + EPS) * fnw_ref[...]


def _out(ys, yc, x, w_o, fnw, *, tm):
    t = x.shape[0]
    row = lambda width: pl.BlockSpec((tm, width), lambda m: (m, 0))
    return pl.pallas_call(
        _out_body,
        grid=(t // tm,),
        in_specs=[row(D_SSM), row(D_SHORT), row(D_MODEL),
                  pl.BlockSpec((D_SSM, D_MODEL), lambda m: (0, 0), pipeline_mode=pl.Buffered(1)),
                  pl.BlockSpec((D_SHORT, D_MODEL), lambda m: (1, 0), pipeline_mode=pl.Buffered(1)),
                  pl.BlockSpec((1, D_MODEL), lambda m: (0, 0))],
        out_specs=row(D_MODEL),
        out_shape=jax.ShapeDtypeStruct((t, D_MODEL), F32),
        compiler_params=pltpu.CompilerParams(
            dimension_semantics=("arbitrary",), vmem_limit_bytes=VMEM_LIMIT),
        name="out_proj",
    )(ys, yc, x, w_o, w_o, fnw)


def _mixer(x2d, h0, conv_state, short_state, p, *, nb, seqlen, act_dtype, tm, nseq, tps):
    lc = min(SSD_CHUNK, seqlen)
    nc = seqlen // lc
    h, zg, dt = _proj_z(x2d, p["nw"], p["w_t"], p["dtb"], act_dtype, tm=tm)
    xc, conv_new = _proj_xc(h, p["w_t"], p["conv_w"], p["conv_b"], conv_state,
                            tm=tm, nseq=nseq, tps=tps)
    y_c, short_new = _short(h, p["w_t"], p["conv_short_w"], short_state, act_dtype,
                            tm=tm, tn=256, nseq=nseq, tps=tps)
    y_s, h_new = _ssd(zg, xc, dt, h0, p["alr"], p["expand"], p["expand_pairs"](lc), p["dexp"], p["gnw"],
                      act_dtype, nb=nb, nc=nc, lc=lc)
    return y_s, y_c, h_new, conv_new, short_new


def kernel(x_prompt, x_sample, state_ssm, state_conv_ssd, state_conv_short, norm_w, w_in, conv_ssd_w,
           conv_ssd_b, dt_bias, a_log, d_skip, ssd_norm_w, conv_short_w, w_out, final_norm_w):
    depth = norm_w.shape[0]
    assert depth == 1, "the output projection fuses the final rmsnorm, valid for a single layer"
    bp, lp, _ = x_prompt.shape
    bs, ls, _ = x_sample.shape
    hp = x_prompt.reshape(bp * lp, D_MODEL)
    hs = x_sample.reshape(bs * ls, D_MODEL)
    pad_h = LANES - HEADS
    expand = (jnp.arange(LANES)[:, None] == jnp.arange(D_SSM)[None, :] // HEAD_DIM).astype(BF16)
    outs = [[] for _ in range(6)]
    for layer in range(depth):
        w_t = w_in[layer].T
        p = dict(
            nw=norm_w[layer][None, :],
            w_t=w_t,
            conv_short_w=conv_short_w[layer],
            conv_w=conv_ssd_w[layer],
            conv_b=conv_ssd_b[layer][None, :],
            dtb=jnp.pad(dt_bias[layer], (0, pad_h))[None, :],
            alr=jnp.pad(a_log[layer], (0, pad_h))[None, :],
            expand=expand,
            expand_pairs=lambda lc: (jnp.arange(LANES)[:, None]
                                     == jnp.arange(HEADS * lc)[None, :] // lc).astype(BF16),
            dexp=jnp.repeat(d_skip[layer], HEAD_DIM)[None, :],
            gnw=ssd_norm_w[layer][None, :],
        )
        w_o = w_out[layer].astype(BF16)

        tm_p = 1024
        ys, yc, a1, a2, a3 = _mixer(
            hp, jnp.zeros((bp, D_SSM, D_STATE), F32), jnp.zeros((bp, CONV_K - 1, CONV_DIM), F32),
            jnp.zeros((bp, SHORT_K - 1, D_SHORT), F32), p,
            nb=bp, seqlen=lp, act_dtype=BF16, tm=tm_p, nseq=1, tps=lp // tm_p)
        ys2, yc2, s1, s2, s3 = _mixer(
            hs, state_ssm[layer].reshape(bs, D_SSM, D_STATE), state_conv_ssd[layer],
            state_conv_short[layer], p,
            nb=bs, seqlen=ls, act_dtype=F32, tm=bs * ls, nseq=bs, tps=1)
        fw = final_norm_w[None, :]
        hp = _out(ys, yc, hp, w_o, fw, tm=512)
        hs = _out(ys2, yc2, hs, w_o, fw, tm=512)
        for lst, val in zip(outs, (a1.reshape(bp, HEADS, HEAD_DIM, D_STATE), a2, a3,
                                   s1.reshape(bs, HEADS, HEAD_DIM, D_STATE), s2, s3)):
            lst.append(val)
    return (hp.reshape(bp, lp, D_MODEL), hs.reshape(bs, ls, D_MODEL),
            *(jnp.stack(v) for v in outs))
```

```python
import functools

import jax
import jax.numpy as jnp
from jax import lax
from jax.experimental import pallas as pl
from jax.experimental.pallas import tpu as pltpu

F32 = jnp.float32
BF16 = jnp.bfloat16

D_MODEL = 2048
D_SSM = 2048
D_SHORT = 2048
HEADS = 32
HEAD_DIM = 64
D_STATE = 128
GROUPS = 4
HPG = HEADS // GROUPS
GROUP_W = HPG * HEAD_DIM
GN = GROUPS * D_STATE
CONV_DIM = D_SSM + 2 * GN
CONV_K = 4
SHORT_K = 3
SSD_CHUNK = 128
EPS = 1e-6
LANES = 128
SUBLANES = 8
VMEM_LIMIT = 56 * 1024 * 1024

TN = 1024
Z_TILES = D_SSM // TN
XC_TILES = CONV_DIM // TN
MAIN_VALID = D_SSM + CONV_DIM + HEADS
DT_BLOCK = (D_SSM + CONV_DIM) // LANES

NT_DIMS = (((1,), (1,)), ((), ()))
LOG2E = 1.4426950408889634
SHORT_SEQS_PER_STEP = 8


def _silu(x):
    half = 0.5 * x
    return half + half * jnp.tanh(half)


def _softplus(x):
    return jnp.maximum(x, 0.0) + jnp.log1p(jnp.exp(-jnp.abs(x)))


def _proj_z_body(x_ref, nw_ref, w_ref, wdt_ref, dtb_ref, h_ref, zg_ref, dt_ref, *, tm):
    @pl.when(pl.program_id(1) == 0)
    def _():
        blk = 32
        nw = nw_ref[...]

        def body(i, carry):
            r = pl.ds(pl.multiple_of(i * blk, blk), blk)
            xf = x_ref[r, :]
            var = jnp.mean(xf * xf, axis=-1, keepdims=True)
            h_ref[r, :] = (xf * lax.rsqrt(var + EPS) * nw).astype(BF16)
            return carry

        lax.fori_loop(0, tm // blk, body, 0, unroll=8)
        dt_raw = lax.dot_general(h_ref[...], wdt_ref[...].astype(BF16), NT_DIMS,
                                 preferred_element_type=F32)
        dt_ref[...] = _softplus(dt_raw + dtb_ref[...])

    acc = lax.dot_general(h_ref[...], w_ref[...].astype(BF16), NT_DIMS, preferred_element_type=F32)
    zg_ref[...] = _silu(acc).astype(zg_ref.dtype)


def _proj_z(x, nw, w_t, dtb, zg_dtype, *, tm):
    t = x.shape[0]
    return pl.pallas_call(
        functools.partial(_proj_z_body, tm=tm),
        grid=(t // tm, Z_TILES),
        in_specs=[
            pl.BlockSpec((tm, D_MODEL), lambda m, j: (m, 0)),
            pl.BlockSpec((1, D_MODEL), lambda m, j: (0, 0)),
            pl.BlockSpec((TN, D_MODEL), lambda m, j: (j, 0)),
            pl.BlockSpec((LANES, D_MODEL), lambda m, j: (DT_BLOCK, 0)),
            pl.BlockSpec((1, LANES), lambda m, j: (0, 0)),
        ],
        out_specs=[pl.BlockSpec((tm, D_MODEL), lambda m, j: (m, 0)),
                   pl.BlockSpec((tm, TN), lambda m, j: (m, j)),
                   pl.BlockSpec((tm, LANES), lambda m, j: (m, 0))],
        out_shape=[jax.ShapeDtypeStruct((t, D_MODEL), BF16),
                   jax.ShapeDtypeStruct((t, D_SSM), zg_dtype),
                   jax.ShapeDtypeStruct((t, LANES), F32)],
        compiler_params=pltpu.CompilerParams(
            dimension_semantics=("arbitrary", "arbitrary"), vmem_limit_bytes=VMEM_LIMIT),
        name="in_proj_z",
    )(x, nw, w_t, w_t, dtb)


def _load_history(pad_ref, st_ref, carry, j, *, first, hist, tps):
    def from_state():
        pad_ref[:, 0:SUBLANES, :] = jnp.zeros((pad_ref.shape[0], SUBLANES, pad_ref.shape[2]), F32)
        pad_ref[:, SUBLANES - hist:SUBLANES, :] = st_ref[...]

    if tps > 1:
        (carry_ref,) = carry
        pl.when(first)(from_state)

        @pl.when(jnp.logical_not(first))
        def _():
            pad_ref[:, 0:SUBLANES, :] = carry_ref[j]
    else:
        from_state()


def _causal_conv(pad_ref, cw_ref, taps):
    xe = pad_ref[...]
    if taps == 4:
        s1 = pltpu.roll(xe, 1, axis=1)
        near = xe * cw_ref[3:4, :] + s1 * cw_ref[2:3, :]
        far = xe * cw_ref[1:2, :] + s1 * cw_ref[0:1, :]
        return near[:, SUBLANES:, :] + pltpu.roll(far, 2, axis=1)[:, SUBLANES:, :]
    conv = xe[:, SUBLANES:, :] * cw_ref[taps - 1:taps, :]
    for s in range(1, taps):
        shifted = pltpu.roll(xe, s, axis=1)[:, SUBLANES:, :]
        conv = conv + shifted * cw_ref[taps - 1 - s:taps - s, :]
    return conv


def _proj_xc_body(h_ref, w_ref, cw_ref, cb_ref, st_ref, xc_ref, tails_ref, pad_ref, *carry,
                  tm, nseq, tps):
    j = pl.program_id(1)
    lt = tm // nseq
    _load_history(pad_ref, st_ref, carry, j, first=(pl.program_id(0) % tps) == 0, hist=CONV_K - 1,
                  tps=tps)
    acc = lax.dot_general(h_ref[...], w_ref[...].astype(BF16), NT_DIMS, preferred_element_type=F32)
    pad_ref[:, SUBLANES:SUBLANES + lt, :] = acc.reshape(nseq, lt, TN)
    conv = _causal_conv(pad_ref, cw_ref, CONV_K)
    xc_ref[...] = _silu(conv + cb_ref[...]).reshape(tm, TN)
    tails_ref[...] = pad_ref[:, lt + SUBLANES - (CONV_K - 1):lt + SUBLANES, :]
    if tps > 1:
        carry[0][j] = pad_ref[:, lt:lt + SUBLANES, :]


def _proj_xc(h, w_t, conv_w, conv_b, conv_state, *, tm, nseq, tps):
    t = h.shape[0]
    lt = tm // nseq
    scratch = [pltpu.VMEM((nseq, SUBLANES + lt, TN), F32)]
    if tps > 1:
        scratch.append(pltpu.VMEM((XC_TILES, nseq, SUBLANES, TN), F32))
    xc, tails = pl.pallas_call(
        functools.partial(_proj_xc_body, tm=tm, nseq=nseq, tps=tps),
        grid=(t // tm, XC_TILES),
        in_specs=[
            pl.BlockSpec((tm, D_MODEL), lambda m, j: (m, 0)),
            pl.BlockSpec((TN, D_MODEL), lambda m, j: (Z_TILES + j, 0)),
            pl.BlockSpec((CONV_K, TN), lambda m, j: (0, j)),
            pl.BlockSpec((1, TN), lambda m, j: (0, j)),
            pl.BlockSpec((nseq, CONV_K - 1, TN), lambda m, j: (m // tps, 0, j)),
        ],
        out_specs=[pl.BlockSpec((tm, TN), lambda m, j: (m, j)),
                   pl.BlockSpec((nseq, CONV_K - 1, TN), lambda m, j: (m, 0, j))],
        out_shape=[jax.ShapeDtypeStruct((t, CONV_DIM), F32),
                   jax.ShapeDtypeStruct(((t // tm) * nseq, CONV_K - 1, CONV_DIM), F32)],
        scratch_shapes=scratch,
        compiler_params=pltpu.CompilerParams(
            dimension_semantics=("arbitrary", "arbitrary"), vmem_limit_bytes=VMEM_LIMIT),
        name="in_proj_xc",
    )(h, w_t, conv_w, conv_b, conv_state)
    tails = tails.reshape(t // (tm * tps), tps, nseq, CONV_K - 1, CONV_DIM)[:, -1]
    return xc, tails.reshape(conv_state.shape)


def _short_body(h_ref, *refs, tm, tn, nseq, tps, shift):
    w_refs, (cw_ref, st_ref, y_ref, ns_ref, wbf_ref, pad_ref), carry = refs[:8], refs[8:14], refs[14:]
    m = pl.program_id(1)
    lt = tm // nseq

    @pl.when(m == 0)
    def _():
        for k in range(4):
            a_ref, b_ref = w_refs[2 * k], w_refs[2 * k + 1]
            wbf_ref[k] = jnp.concatenate([a_ref[shift:, :], b_ref[...]], axis=0).astype(BF16)

    _load_history(pad_ref, st_ref, carry, 0, first=(m % tps) == 0, hist=SHORT_K - 1, tps=tps)
    h = h_ref[...]
    band = lambda k: lax.dot_general(h, wbf_ref[k], NT_DIMS, preferred_element_type=F32)
    c = band(2)
    v = band(3)
    pad_ref[:, SUBLANES:SUBLANES + lt, :] = (c * v).reshape(nseq, lt, tn)
    conv = _causal_conv(pad_ref, cw_ref, SHORT_K)
    b = band(1)
    z = band(0)
    y = b * conv.reshape(tm, tn) * _silu(z)
    y_ref[...] = y.astype(y_ref.dtype)
    ns_ref[...] = pad_ref[:, lt + SUBLANES - (SHORT_K - 1):lt + SUBLANES, :]
    if tps > 1:
        carry[0][0] = pad_ref[:, lt:lt + SUBLANES, :]


def _short(h, w_t, conv_w, state, out_dtype, *, tm, tn, nseq, tps):
    t = h.shape[0]
    nj = D_SHORT // tn
    lt = tm // nseq
    base = (MAIN_VALID // tn) * tn
    shift = MAIN_VALID - base
    assert shift % SUBLANES == 0 and tn % shift == 0 and base % shift == 0
    scratch = [pltpu.VMEM((4, tn, D_MODEL), BF16), pltpu.VMEM((nseq, SUBLANES + lt, tn), F32)]
    if tps > 1:
        scratch.append(pltpu.VMEM((1, nseq, SUBLANES, tn), F32))
    w_specs = []
    for k in range(4):
        r0 = base + k * D_SHORT
        w_specs.append(pl.BlockSpec((tn, D_MODEL), lambda j, m, r0=r0: (r0 // tn + j, 0)))
        w_specs.append(pl.BlockSpec((shift, D_MODEL), lambda j, m, r0=r0: ((r0 + tn * (j + 1)) // shift, 0)))

    y, tails = pl.pallas_call(
        functools.partial(_short_body, tm=tm, tn=tn, nseq=nseq, tps=tps, shift=shift),
        grid=(nj, t // tm),
        in_specs=[
            pl.BlockSpec((tm, D_MODEL), lambda j, m: (m, 0)),
            *w_specs,
            pl.BlockSpec((SHORT_K, tn), lambda j, m: (0, j)),
            pl.BlockSpec((nseq, SHORT_K - 1, tn), lambda j, m: (m // tps, 0, j)),
        ],
        out_specs=[pl.BlockSpec((tm, tn), lambda j, m: (m, j)),
                   pl.BlockSpec((nseq, SHORT_K - 1, tn), lambda j, m: (m, 0, j))],
        out_shape=[jax.ShapeDtypeStruct((t, D_SHORT), out_dtype),
                   jax.ShapeDtypeStruct(((t // tm) * nseq, SHORT_K - 1, D_SHORT), F32)],
        scratch_shapes=scratch,
        compiler_params=pltpu.CompilerParams(
            dimension_semantics=("arbitrary", "arbitrary"), vmem_limit_bytes=VMEM_LIMIT),
        name="short_conv",
    )(h, *([w_t] * 8), conv_w, state)
    tails = tails.reshape(t // (tm * tps), tps, nseq, SHORT_K - 1, D_SHORT)[:, -1]
    return y, tails.reshape(state.shape)


def _dot_split(x, w, pieces):
    out = None
    r = x
    for _ in range(pieces):
        p = r.astype(BF16)
        d = jnp.dot(p, w, preferred_element_type=F32)
        out = d if out is None else out + d
        r = r - p.astype(F32)
    return out


def _prefix_sum_rows(x, period):
    t = lax.broadcasted_iota(jnp.int32, x.shape, 0) % period
    k = 1
    while k < period:
        x = x + jnp.where(t >= k, pltpu.roll(x, k, axis=0), 0.0)
        k *= 2
    return x


def _gated_norm_store(y, zg_ref, gnw_ref, y_ref):
    y = y * zg_ref[...].astype(F32)
    for g in range(GROUPS):
        cs = slice(g * GROUP_W, (g + 1) * GROUP_W)
        blk = y[:, cs]
        ms = jnp.mean(blk * blk, axis=-1, keepdims=True)
        y_ref[:, cs] = (blk * lax.rsqrt(ms + EPS) * gnw_ref[:, cs]).astype(y_ref.dtype)


def _ssd_chunk_body(zg_ref, xc_ref, dt_ref, h0_ref, alr_ref, expand_ref, dexp_ref, gnw_ref,
                    y_ref, hout_ref, hst_ref, ysc_ref, *, lc, nc):
    c = pl.program_id(1)

    @pl.when(c == 0)
    def _():
        hst_ref[...] = h0_ref[0].T

    row_i = lax.broadcasted_iota(jnp.int32, (lc, lc), 0)
    col_i = lax.broadcasted_iota(jnp.int32, (lc, lc), 1)
    causal = row_i >= col_i
    lane_lo = lax.broadcasted_iota(jnp.int32, (lc, LANES), 1) < HEAD_DIM

    dt = dt_ref[...]
    la = dt * (-jnp.exp(alr_ref[...]))
    a_cum = _prefix_sum_rows(la, lc)
    a_last = a_cum[lc - 1:lc, :]
    factors = jnp.concatenate([jnp.exp(a_cum), jnp.exp(a_last - a_cum) * dt], axis=0).astype(BF16)
    a2 = a_cum * LOG2E
    r_t = (a2 - jnp.log2(dt)).T

    for g in range(GROUPS):
        bg = xc_ref[:, D_SSM + g * D_STATE:D_SSM + (g + 1) * D_STATE]
        cg = xc_ref[:, D_SSM + GN + g * D_STATE:D_SSM + GN + (g + 1) * D_STATE].astype(BF16)
        cols = slice(g * GROUP_W, (g + 1) * GROUP_W)
        h_prev = hst_ref[:, cols]
        cb = lax.dot_general(cg, bg.astype(BF16), NT_DIMS, preferred_element_type=F32)
        y_off = jnp.dot(cg, h_prev.astype(BF16), preferred_element_type=F32)
        xs_g = xc_ref[:, cols]
        both = jnp.dot(factors, expand_ref[:, cols], preferred_element_type=F32)
        e_exp, w_exp = both[:lc], both[lc:]
        for pr in range(HPG // 2):
            mats = []
            for hh in (g * HPG + 2 * pr, g * HPG + 2 * pr + 1):
                seg = a2[:, hh:hh + 1] - r_t[hh:hh + 1, :]
                dec = jnp.exp2(jnp.where(causal, seg, -jnp.inf))
                mats.append((cb * dec).astype(BF16))
            xp = xs_g[:, pr * LANES:(pr + 1) * LANES].astype(BF16)
            rhs = jnp.concatenate([jnp.where(lane_lo, xp, 0), jnp.where(lane_lo, 0, xp)], axis=0)
            y_diag = jnp.dot(jnp.concatenate(mats, axis=1), rhs, preferred_element_type=F32)
            pc = slice(g * GROUP_W + pr * LANES, g * GROUP_W + (pr + 1) * LANES)
            ysc_ref[:, pc] = y_diag + (y_off * e_exp)[:, pr * LANES:(pr + 1) * LANES]
        xw = (xs_g * w_exp).astype(BF16)
        s_t = jnp.dot(bg.T.astype(BF16), xw, preferred_element_type=F32)
        hst_ref[:, cols] = h_prev * e_exp[lc - 1:lc, :] + s_t

    y = ysc_ref[...] + xc_ref[:, 0:D_SSM] * dexp_ref[...]
    _gated_norm_store(y, zg_ref, gnw_ref, y_ref)

    @pl.when(c == nc - 1)
    def _():
        hout_ref[0] = hst_ref[...].T


def _ssd_short_body(zg_ref, xc_ref, dt_ref, h0_ref, alr_ref, expand_ref, exps_ref, dexp_ref, gnw_ref,
                    y_ref, hout_ref, ysc_ref, *, lc, spb):
    rows = spb * lc
    hl = HEADS * lc
    dt = dt_ref[...]
    a_cum = _prefix_sum_rows(dt * (-jnp.exp(alr_ref[...])), lc)
    a_last = a_cum.reshape(spb, lc, LANES)[:, lc - 1:lc, :]
    a_last_b = jnp.broadcast_to(a_last, (spb, lc, LANES)).reshape(rows, LANES)
    expand = expand_ref[...]
    e_exp = _dot_split(jnp.exp(a_cum), expand, 2)
    w_exp = _dot_split(jnp.exp(a_last_b - a_cum) * dt, expand, 2)
    cd_t = jnp.exp(a_last.reshape(spb, LANES)).T

    z2 = _dot_split(jnp.concatenate([a_cum, dt], axis=0), exps_ref[...], 3)
    zc, zd = z2[:rows], z2[rows:]
    trow = lax.broadcasted_iota(jnp.int32, (rows, hl), 0) % lc
    tsrc = lax.broadcasted_iota(jnp.int32, (rows, hl), 1) % lc
    diag = trow == tsrc

    def per_source(z):
        d = jnp.where(diag, z, 0.0).reshape(spb, lc, hl).sum(axis=1, keepdims=True)
        return jnp.broadcast_to(d, (spb, lc, hl)).reshape(rows, hl)

    dec = jnp.exp(jnp.where(trow >= tsrc, zc - per_source(zc), -jnp.inf)) * per_source(zd)

    pair_w = HPG * lc
    blockdiag = (lax.broadcasted_iota(jnp.int32, (pair_w, GROUP_W), 0) // lc
                 == lax.broadcasted_iota(jnp.int32, (pair_w, GROUP_W), 1) // HEAD_DIM)

    for q in range(spb):
        tok = slice(q * lc, (q + 1) * lc)
        for g in range(GROUPS):
            bg = xc_ref[tok, D_SSM + g * D_STATE:D_SSM + (g + 1) * D_STATE]
            cg = xc_ref[tok, D_SSM + GN + g * D_STATE:D_SSM + GN + (g + 1) * D_STATE].astype(BF16)
            cols = slice(g * GROUP_W, (g + 1) * GROUP_W)
            h_prev = h0_ref[q, cols, :]
            xs_g = xc_ref[tok, cols]
            b_tiled = jnp.concatenate([bg] * HPG, axis=0).astype(BF16)
            cb = lax.dot_general(cg, b_tiled, NT_DIMS, preferred_element_type=F32)
            m = (cb * dec[tok, g * pair_w:(g + 1) * pair_w]).astype(BF16)
            x_diag = jnp.where(blockdiag, jnp.concatenate([xs_g] * HPG, axis=0), 0.0).astype(BF16)
            y_diag = jnp.dot(m, x_diag, preferred_element_type=F32)
            y_off = lax.dot_general(cg, h_prev.astype(BF16), NT_DIMS, preferred_element_type=F32)
            ysc_ref[tok, cols] = y_diag + y_off * e_exp[tok, cols]
            xw_t = (xs_g * w_exp[tok, cols]).T.astype(BF16)
            s_g = jnp.dot(xw_t, bg.astype(BF16), preferred_element_type=F32)
            cd = jnp.concatenate(
                [jnp.broadcast_to(cd_t[g * HPG + jh:g * HPG + jh + 1, q:q + 1], (HEAD_DIM, D_STATE))
                 for jh in range(HPG)], axis=0)
            hout_ref[q, cols, :] = h_prev * cd + s_g

    y = ysc_ref[...] + xc_ref[:, 0:D_SSM] * dexp_ref[...]
    _gated_norm_store(y, zg_ref, gnw_ref, y_ref)


def _ssd(zg, xc, dt, h0, alr, expand, exps, dexp, gnw, out_dtype, *, nb, nc, lc):
    t = zg.shape[0]
    par = lambda r, width: pl.BlockSpec((r, width), lambda b, c: (0, 0))
    params = [par(1, LANES), par(LANES, D_SSM)]
    operands = [alr, expand]
    if lc == SSD_CHUNK:
        spb = 1
        body = functools.partial(_ssd_chunk_body, lc=lc, nc=nc)
        scratch = [pltpu.VMEM((D_STATE, D_SSM), F32)]
    else:
        assert nc == 1
        spb = SHORT_SEQS_PER_STEP
        body = functools.partial(_ssd_short_body, lc=lc, spb=spb)
        scratch = []
        params.append(par(LANES, HEADS * lc))
        operands.append(exps)
    rows = spb * lc
    tok = lambda width: pl.BlockSpec((rows, width), lambda b, c: (b * nc + c, 0))
    h_spec = pl.BlockSpec((spb, D_SSM, D_STATE), lambda b, c: (b, 0, 0))
    return pl.pallas_call(
        body,
        grid=(nb // spb, nc),
        in_specs=[tok(D_SSM), tok(CONV_DIM), tok(LANES), h_spec, *params, par(1, D_SSM), par(1, D_SSM)],
        out_specs=[tok(D_SSM), h_spec],
        out_shape=[jax.ShapeDtypeStruct((t, D_SSM), out_dtype),
                   jax.ShapeDtypeStruct((nb, D_SSM, D_STATE), F32)],
        scratch_shapes=scratch + [pltpu.VMEM((rows, D_SSM), F32)],
        compiler_params=pltpu.CompilerParams(
            dimension_semantics=("arbitrary", "arbitrary"), vmem_limit_bytes=VMEM_LIMIT),
        name="ssd_scan",
    )(zg, xc, dt, h0, *operands, dexp, gnw)


def _out_body(ys_ref, yc_ref, x_ref, w1_ref, w2_ref, fnw_ref, o_ref):
    acc = jnp.dot(ys_ref[...].astype(BF16), w1_ref[...], preferred_element_type=F32)
    acc = acc + jnp.dot(yc_ref[...].astype(BF16), w2_ref[...], preferred_element_type=F32)
    r = x_ref[...] + acc
    var = jnp.mean(r * r, axis=-1, keepdims=True)
    o_ref[...] = r * lax.rsqrt(var + EPS) * fnw_ref[...]


def _out(ys, yc, x, w_o, fnw, *, tm):
    t = x.shape[0]
    row = lambda width: pl.BlockSpec((tm, width), lambda m: (m, 0))
    return pl.pallas_call(
        _out_body,
        grid=(t // tm,),
        in_specs=[row(D_SSM), row(D_SHORT), row(D_MODEL),
                  pl.BlockSpec((D_SSM, D_MODEL), lambda m: (0, 0), pipeline_mode=pl.Buffered(1)),
                  pl.BlockSpec((D_SHORT, D_MODEL), lambda m: (1, 0), pipeline_mode=pl.Buffered(1)),
                  pl.BlockSpec((1, D_MODEL), lambda m: (0, 0))],
        out_specs=row(D_MODEL),
        out_shape=jax.ShapeDtypeStruct((t, D_MODEL), F32),
        compiler_params=pltpu.CompilerParams(
            dimension_semantics=("arbitrary",), vmem_limit_bytes=VMEM_LIMIT),
        name="out_proj",
    )(ys, yc, x, w_o, w_o, fnw)


def _mixer(x2d, h0, conv_state, short_state, p, *, nb, seqlen, act_dtype, tm, nseq, tps):
    lc = min(SSD_CHUNK, seqlen)
    nc = seqlen // lc
    h, zg, dt = _proj_z(x2d, p["nw"], p["w_t"], p["dtb"], act_dtype, tm=tm)
    xc, conv_new = _proj_xc(h, p["w_t"], p["conv_w"], p["conv_b"], conv_state,
                            tm=tm, nseq=nseq, tps=tps)
    y_c, short_new = _short(h, p["w_t"], p["conv_short_w"], short_state, act_dtype,
                            tm=tm, tn=256, nseq=nseq, tps=tps)
    y_s, h_new = _ssd(zg, xc, dt, h0, p["alr"], p["expand"], p["expand_pairs"](lc), p["dexp"], p["gnw"],
                      act_dtype, nb=nb, nc=nc, lc=lc)
    return y_s, y_c, h_new, conv_new, short_new


def kernel(x_prompt, x_sample, state_ssm, state_conv_ssd, state_conv_short, norm_w, w_in, conv_ssd_w,
           conv_ssd_b, dt_bias, a_log, d_skip, ssd_norm_w, conv_short_w, w_out, final_norm_w):
    depth = norm_w.shape[0]
    assert depth == 1, "the output projection fuses the final rmsnorm, valid for a single layer"
    bp, lp, _ = x_prompt.shape
    bs, ls, _ = x_sample.shape
    hp = x_prompt.reshape(bp * lp, D_MODEL)
    hs = x_sample.reshape(bs * ls, D_MODEL)
    pad_h = LANES - HEADS
    expand = (jnp.arange(LANES)[:, None] == jnp.arange(D_SSM)[None, :] // HEAD_DIM).astype(BF16)
    outs = [[] for _ in range(6)]
    for layer in range(depth):
        w_t = w_in[layer].T
        p = dict(
            nw=norm_w[layer][None, :],
            w_t=w_t,
            conv_short_w=conv_short_w[layer],
            conv_w=conv_ssd_w[layer],
            conv_b=conv_ssd_b[layer][None, :],
            dtb=jnp.pad(dt_bias[layer], (0, pad_h))[None, :],
            alr=jnp.pad(a_log[layer], (0, pad_h))[None, :],
            expand=expand,
            expand_pairs=lambda lc: (jnp.arange(LANES)[:, None]
                                     == jnp.arange(HEADS * lc)[None, :] // lc).astype(BF16),
            dexp=jnp.repeat(d_skip[layer], HEAD_DIM)[None, :],
            gnw=ssd_norm_w[layer][None, :],
        )
        w_o = w_out[layer].astype(BF16)

        tm_p = 1024
        ys, yc, a1, a2, a3 = _mixer(
            hp, jnp.zeros((bp, D_SSM, D_STATE), F32), jnp.zeros((bp, CONV_K - 1, CONV_DIM), F32),
            jnp.zeros((bp, SHORT_K - 1, D_SHORT), F32), p,
            nb=bp, seqlen=lp, act_dtype=BF16, tm=tm_p, nseq=1, tps=lp // tm_p)
        ys2, yc2, s1, s2, s3 = _mixer(
            hs, state_ssm[layer].reshape(bs, D_SSM, D_STATE), state_conv_ssd[layer],
            state_conv_short[layer], p,
            nb=bs, seqlen=ls, act_dtype=F32, tm=bs * ls, nseq=bs, tps=1)
        fw = final_norm_w[None, :]
        hp = _out(ys, yc, hp, w_o, fw, tm=512)
        hs = _out(ys2, yc2, hs, w_o, fw, tm=512)
        for lst, val in zip(outs, (a1.reshape(bp, HEADS, HEAD_DIM, D_STATE), a2, a3,
                                   s1.reshape(bs, HEADS, HEAD_DIM, D_STATE), s2, s3)):
            lst.append(val)
    return (hp.reshape(bp, lp, D_MODEL), hs.reshape(bs, ls, D_MODEL),
            *(jnp.stack(v) for v in outs))
```

```python
import functools

import jax
import jax.numpy as jnp
from jax import lax
from jax.experimental import pallas as pl
from jax.experimental.pallas import tpu as pltpu

F32 = jnp.float32
BF16 = jnp.bfloat16

D_MODEL = 2048
D_SSM = 2048
D_SHORT = 2048
HEADS = 32
HEAD_DIM = 64
D_STATE = 128
GROUPS = 4
HPG = HEADS // GROUPS
GROUP_W = HPG * HEAD_DIM
GN = GROUPS * D_STATE
CONV_DIM = D_SSM + 2 * GN
CONV_K = 4
SHORT_K = 3
SSD_CHUNK = 128
EPS = 1e-6
LANES = 128
SUBLANES = 8
VMEM_LIMIT = 56 * 1024 * 1024

TN = 1024
Z_TILES = D_SSM // TN
XC_TILES = CONV_DIM // TN
MAIN_VALID = D_SSM + CONV_DIM + HEADS
DT_BLOCK = (D_SSM + CONV_DIM) // LANES

NT_DIMS = (((1,), (1,)), ((), ()))
LOG2E = 1.4426950408889634
SHORT_SEQS_PER_STEP = 8


def _silu(x):
    half = 0.5 * x
    return half + half * jnp.tanh(half)


def _softplus(x):
    return jnp.maximum(x, 0.0) + jnp.log1p(jnp.exp(-jnp.abs(x)))


def _proj_z_body(x_ref, nw_ref, w_ref, wdt_ref, dtb_ref, h_ref, zg_ref, dt_ref, *, tm):
    @pl.when(pl.program_id(1) == 0)
    def _():
        blk = 32
        nw = nw_ref[...]

        def body(i, carry):
            r = pl.ds(pl.multiple_of(i * blk, blk), blk)
            xf = x_ref[r, :]
            var = jnp.mean(xf * xf, axis=-1, keepdims=True)
            h_ref[r, :] = (xf * lax.rsqrt(var + EPS) * nw).astype(BF16)
            return carry

        lax.fori_loop(0, tm // blk, body, 0, unroll=8)
        dt_raw = lax.dot_general(h_ref[...], wdt_ref[...].astype(BF16), NT_DIMS,
                                 preferred_element_type=F32)
        dt_ref[...] = _softplus(dt_raw + dtb_ref[...])

    acc = lax.dot_general(h_ref[...], w_ref[...].astype(BF16), NT_DIMS, preferred_element_type=F32)
    zg_ref[...] = _silu(acc).astype(zg_ref.dtype)


def _proj_z(x, nw, w_t, dtb, zg_dtype, *, tm):
    t = x.shape[0]
    return pl.pallas_call(
        functools.partial(_proj_z_body, tm=tm),
        grid=(t // tm, Z_TILES),
        in_specs=[
            pl.BlockSpec((tm, D_MODEL), lambda m, j: (m, 0)),
            pl.BlockSpec((1, D_MODEL), lambda m, j: (0, 0)),
            pl.BlockSpec((TN, D_MODEL), lambda m, j: (j, 0)),
            pl.BlockSpec((LANES, D_MODEL), lambda m, j: (DT_BLOCK, 0)),
            pl.BlockSpec((1, LANES), lambda m, j: (0, 0)),
        ],
        out_specs=[pl.BlockSpec((tm, D_MODEL), lambda m, j: (m, 0)),
                   pl.BlockSpec((tm, TN), lambda m, j: (m, j)),
                   pl.BlockSpec((tm, LANES), lambda m, j: (m, 0))],
        out_shape=[jax.ShapeDtypeStruct((t, D_MODEL), BF16),
                   jax.ShapeDtypeStruct((t, D_SSM), zg_dtype),
                   jax.ShapeDtypeStruct((t, LANES), F32)],
        compiler_params=pltpu.CompilerParams(
            dimension_semantics=("arbitrary", "arbitrary"), vmem_limit_bytes=VMEM_LIMIT),
        name="in_proj_z",
    )(x, nw, w_t, w_t, dtb)


def _load_history(pad_ref, st_ref, carry, j, *, first, hist, tps):
    def from_state():
        pad_ref[:, 0:SUBLANES, :] = jnp.zeros((pad_ref.shape[0], SUBLANES, pad_ref.shape[2]), F32)
        pad_ref[:, SUBLANES - hist:SUBLANES, :] = st_ref[...]

    if tps > 1:
        (carry_ref,) = carry
        pl.when(first)(from_state)

        @pl.when(jnp.logical_not(first))
        def _():
            pad_ref[:, 0:SUBLANES, :] = carry_ref[j]
    else:
        from_state()


def _causal_conv(pad_ref, cw_ref, taps):
    xe = pad_ref[...]
    if taps == 4:
        s1 = pltpu.roll(xe, 1, axis=1)
        near = xe * cw_ref[3:4, :] + s1 * cw_ref[2:3, :]
        far = xe * cw_ref[1:2, :] + s1 * cw_ref[0:1, :]
        return near[:, SUBLANES:, :] + pltpu.roll(far, 2, axis=1)[:, SUBLANES:, :]
    conv = xe[:, SUBLANES:, :] * cw_ref[taps - 1:taps, :]
    for s in range(1, taps):
        shifted = pltpu.roll(xe, s, axis=1)[:, SUBLANES:, :]
        conv = conv + shifted * cw_ref[taps - 1 - s:taps - s, :]
    return conv


def _proj_xc_body(h_ref, w_ref, cw_ref, cb_ref, st_ref, xc_ref, tails_ref, pad_ref, *carry,
                  tm, nseq, tps):
    j = pl.program_id(1)
    lt = tm // nseq
    _load_history(pad_ref, st_ref, carry, j, first=(pl.program_id(0) % tps) == 0, hist=CONV_K - 1,
                  tps=tps)
    acc = lax.dot_general(h_ref[...], w_ref[...].astype(BF16), NT_DIMS, preferred_element_type=F32)
    pad_ref[:, SUBLANES:SUBLANES + lt, :] = acc.reshape(nseq, lt, TN)
    conv = _causal_conv(pad_ref, cw_ref, CONV_K)
    xc_ref[...] = _silu(conv + cb_ref[...]).reshape(tm, TN)
    tails_ref[...] = pad_ref[:, lt + SUBLANES - (CONV_K - 1):lt + SUBLANES, :]
    if tps > 1:
        carry[0][j] = pad_ref[:, lt:lt + SUBLANES, :]


def _proj_xc(h, w_t, conv_w, conv_b, conv_state, *, tm, nseq, tps):
    t = h.shape[0]
    lt = tm // nseq
    scratch = [pltpu.VMEM((nseq, SUBLANES + lt, TN), F32)]
    if tps > 1:
        scratch.append(pltpu.VMEM((XC_TILES, nseq, SUBLANES, TN), F32))
    xc, tails = pl.pallas_call(
        functools.partial(_proj_xc_body, tm=tm, nseq=nseq, tps=tps),
        grid=(t // tm, XC_TILES),
        in_specs=[
            pl.BlockSpec((tm, D_MODEL), lambda m, j: (m, 0)),
            pl.BlockSpec((TN, D_MODEL), lambda m, j: (Z_TILES + j, 0)),
            pl.BlockSpec((CONV_K, TN), lambda m, j: (0, j)),
            pl.BlockSpec((1, TN), lambda m, j: (0, j)),
            pl.BlockSpec((nseq, CONV_K - 1, TN), lambda m, j: (m // tps, 0, j)),
        ],
        out_specs=[pl.BlockSpec((tm, TN), lambda m, j: (m, j)),
                   pl.BlockSpec((nseq, CONV_K - 1, TN), lambda m, j: (m, 0, j))],
        out_shape=[jax.ShapeDtypeStruct((t, CONV_DIM), F32),
                   jax.ShapeDtypeStruct(((t // tm) * nseq, CONV_K - 1, CONV_DIM), F32)],
        scratch_shapes=scratch,
        compiler_params=pltpu.CompilerParams(
            dimension_semantics=("arbitrary", "arbitrary"), vmem_limit_bytes=VMEM_LIMIT),
        name="in_proj_xc",
    )(h, w_t, conv_w, conv_b, conv_state)
    tails = tails.reshape(t // (tm * tps), tps, nseq, CONV_K - 1, CONV_DIM)[:, -1]
    return xc, tails.reshape(conv_state.shape)


def _short_body(h_ref, *refs, tm, tn, nseq, tps, shift):
    w_refs, (cw_ref, st_ref, y_ref, ns_ref, wbf_ref, pad_ref), carry = refs[:8], refs[8:14], refs[14:]
    m = pl.program_id(1)
    lt = tm // nseq

    @pl.when(m == 0)
    def _():
        for k in range(4):
            a_ref, b_ref = w_refs[2 * k], w_refs[2 * k + 1]
            wbf_ref[k] = jnp.concatenate([a_ref[shift:, :], b_ref[...]], axis=0).astype(BF16)

    _load_history(pad_ref, st_ref, carry, 0, first=(m % tps) == 0, hist=SHORT_K - 1, tps=tps)
    h = h_ref[...]
    band = lambda k: lax.dot_general(h, wbf_ref[k], NT_DIMS, preferred_element_type=F32)
    c = band(2)
    v = band(3)
    pad_ref[:, SUBLANES:SUBLANES + lt, :] = (c * v).reshape(nseq, lt, tn)
    conv = _causal_conv(pad_ref, cw_ref, SHORT_K)
    b = band(1)
    z = band(0)
    y = b * conv.reshape(tm, tn) * _silu(z)
    y_ref[...] = y.astype(y_ref.dtype)
    ns_ref[...] = pad_ref[:, lt + SUBLANES - (SHORT_K - 1):lt + SUBLANES, :]
    if tps > 1:
        carry[0][0] = pad_ref[:, lt:lt + SUBLANES, :]


def _short(h, w_t, conv_w, state, out_dtype, *, tm, tn, nseq, tps):
    t = h.shape[0]
    nj = D_SHORT // tn
    lt = tm // nseq
    base = (MAIN_VALID // tn) * tn
    shift = MAIN_VALID - base
    assert shift % SUBLANES == 0 and tn % shift == 0 and base % shift == 0
    scratch = [pltpu.VMEM((4, tn, D_MODEL), BF16), pltpu.VMEM((nseq, SUBLANES + lt, tn), F32)]
    if tps > 1:
        scratch.append(pltpu.VMEM((1, nseq, SUBLANES, tn), F32))
    w_specs = []
    for k in range(4):
        r0 = base + k * D_SHORT
        w_specs.append(pl.BlockSpec((tn, D_MODEL), lambda j, m, r0=r0: (r0 // tn + j, 0)))
        w_specs.append(pl.BlockSpec((shift, D_MODEL), lambda j, m, r0=r0: ((r0 + tn * (j + 1)) // shift, 0)))

    y, tails = pl.pallas_call(
        functools.partial(_short_body, tm=tm, tn=tn, nseq=nseq, tps=tps, shift=shift),
        grid=(nj, t // tm),
        in_specs=[
            pl.BlockSpec((tm, D_MODEL), lambda j, m: (m, 0)),
            *w_specs,
            pl.BlockSpec((SHORT_K, tn), lambda j, m: (0, j)),
            pl.BlockSpec((nseq, SHORT_K - 1, tn), lambda j, m: (m // tps, 0, j)),
        ],
        out_specs=[pl.BlockSpec((tm, tn), lambda j, m: (m, j)),
                   pl.BlockSpec((nseq, SHORT_K - 1, tn), lambda j, m: (m, 0, j))],
        out_shape=[jax.ShapeDtypeStruct((t, D_SHORT), out_dtype),
                   jax.ShapeDtypeStruct(((t // tm) * nseq, SHORT_K - 1, D_SHORT), F32)],
        scratch_shapes=scratch,
        compiler_params=pltpu.CompilerParams(
            dimension_semantics=("arbitrary", "arbitrary"), vmem_limit_bytes=VMEM_LIMIT),
        name="short_conv",
    )(h, *([w_t] * 8), conv_w, state)
    tails = tails.reshape(t // (tm * tps), tps, nseq, SHORT_K - 1, D_SHORT)[:, -1]
    return y, tails.reshape(state.shape)


def _dot_split(x, w, pieces):
    out = None
    r = x
    for _ in range(pieces):
        p = r.astype(BF16)
        d = jnp.dot(p, w, preferred_element_type=F32)
        out = d if out is None else out + d
        r = r - p.astype(F32)
    return out


def _prefix_sum_rows(x, period):
    t = lax.broadcasted_iota(jnp.int32, x.shape, 0) % period
    k = 1
    while k < period:
        x = x + jnp.where(t >= k, pltpu.roll(x, k, axis=0), 0.0)
        k *= 2
    return x


def _gated_norm_store(y, zg_ref, gnw_ref, y_ref):
    y = y * zg_ref[...].astype(F32)
    for g in range(GROUPS):
        cs = slice(g * GROUP_W, (g + 1) * GROUP_W)
        blk = y[:, cs]
        ms = jnp.mean(blk * blk, axis=-1, keepdims=True)
        y_ref[:, cs] = (blk * lax.rsqrt(ms + EPS) * gnw_ref[:, cs]).astype(y_ref.dtype)


def _ssd_chunk_body(zg_ref, xc_ref, dt_ref, h0_ref, alr_ref, expand_ref, dexp_ref, gnw_ref, wsrc_ref,
                    y_ref, hout_ref, wdst_ref, hst_ref, ysc_ref, *, lc, nc):
    c = pl.program_id(1)
    wdst_ref[...] = wsrc_ref[...].astype(BF16)

    @pl.when(c == 0)
    def _():
        hst_ref[...] = h0_ref[0].T

    row_i = lax.broadcasted_iota(jnp.int32, (lc, lc), 0)
    col_i = lax.broadcasted_iota(jnp.int32, (lc, lc), 1)
    causal = row_i >= col_i
    lane_lo = lax.broadcasted_iota(jnp.int32, (lc, LANES), 1) < HEAD_DIM

    dt = dt_ref[...]
    la = dt * (-jnp.exp(alr_ref[...]))
    a_cum = _prefix_sum_rows(la, lc)
    a_last = a_cum[lc - 1:lc, :]
    factors = jnp.concatenate([jnp.exp(a_cum), jnp.exp(a_last - a_cum) * dt], axis=0).astype(BF16)
    a2 = a_cum * LOG2E
    r_t = (a2 - jnp.log2(dt)).T

    for g in range(GROUPS):
        bg = xc_ref[:, D_SSM + g * D_STATE:D_SSM + (g + 1) * D_STATE]
        cg = xc_ref[:, D_SSM + GN + g * D_STATE:D_SSM + GN + (g + 1) * D_STATE].astype(BF16)
        cols = slice(g * GROUP_W, (g + 1) * GROUP_W)
        h_prev = hst_ref[:, cols]
        cb = lax.dot_general(cg, bg.astype(BF16), NT_DIMS, preferred_element_type=F32)
        y_off = jnp.dot(cg, h_prev.astype(BF16), preferred_element_type=F32)
        xs_g = xc_ref[:, cols]
        both = jnp.dot(factors, expand_ref[:, cols], preferred_element_type=F32)
        e_exp, w_exp = both[:lc], both[lc:]
        for pr in range(HPG // 2):
            mats = []
            for hh in (g * HPG + 2 * pr, g * HPG + 2 * pr + 1):
                seg = a2[:, hh:hh + 1] - r_t[hh:hh + 1, :]
                dec = jnp.exp2(jnp.where(causal, seg, -jnp.inf))
                mats.append((cb * dec).astype(BF16))
            xp = xs_g[:, pr * LANES:(pr + 1) * LANES].astype(BF16)
            rhs = jnp.concatenate([jnp.where(lane_lo, xp, 0), jnp.where(lane_lo, 0, xp)], axis=0)
            y_diag = jnp.dot(jnp.concatenate(mats, axis=1), rhs, preferred_element_type=F32)
            pc = slice(g * GROUP_W + pr * LANES, g * GROUP_W + (pr + 1) * LANES)
            ysc_ref[:, pc] = y_diag + (y_off * e_exp)[:, pr * LANES:(pr + 1) * LANES]
        xw = (xs_g * w_exp).astype(BF16)
        s_t = jnp.dot(bg.T.astype(BF16), xw, preferred_element_type=F32)
        hst_ref[:, cols] = h_prev * e_exp[lc - 1:lc, :] + s_t

    y = ysc_ref[...] + xc_ref[:, 0:D_SSM] * dexp_ref[...]
    _gated_norm_store(y, zg_ref, gnw_ref, y_ref)

    @pl.when(c == nc - 1)
    def _():
        hout_ref[0] = hst_ref[...].T


def _ssd_short_body(zg_ref, xc_ref, dt_ref, h0_ref, alr_ref, expand_ref, exps_ref, dexp_ref, gnw_ref,
                    y_ref, hout_ref, ysc_ref, *, lc, spb):
    rows = spb * lc
    hl = HEADS * lc
    dt = dt_ref[...]
    a_cum = _prefix_sum_rows(dt * (-jnp.exp(alr_ref[...])), lc)
    a_last = a_cum.reshape(spb, lc, LANES)[:, lc - 1:lc, :]
    a_last_b = jnp.broadcast_to(a_last, (spb, lc, LANES)).reshape(rows, LANES)
    expand = expand_ref[...]
    e_exp = _dot_split(jnp.exp(a_cum), expand, 2)
    w_exp = _dot_split(jnp.exp(a_last_b - a_cum) * dt, expand, 2)
    cd_t = jnp.exp(a_last.reshape(spb, LANES)).T

    z2 = _dot_split(jnp.concatenate([a_cum, dt], axis=0), exps_ref[...], 3)
    zc, zd = z2[:rows], z2[rows:]
    trow = lax.broadcasted_iota(jnp.int32, (rows, hl), 0) % lc
    tsrc = lax.broadcasted_iota(jnp.int32, (rows, hl), 1) % lc
    diag = trow == tsrc

    def per_source(z):
        d = jnp.where(diag, z, 0.0).reshape(spb, lc, hl).sum(axis=1, keepdims=True)
        return jnp.broadcast_to(d, (spb, lc, hl)).reshape(rows, hl)

    dec = jnp.exp(jnp.where(trow >= tsrc, zc - per_source(zc), -jnp.inf)) * per_source(zd)

    pair_w = HPG * lc
    blockdiag = (lax.broadcasted_iota(jnp.int32, (pair_w, GROUP_W), 0) // lc
                 == lax.broadcasted_iota(jnp.int32, (pair_w, GROUP_W), 1) // HEAD_DIM)

    for q in range(spb):
        tok = slice(q * lc, (q + 1) * lc)
        for g in range(GROUPS):
            bg = xc_ref[tok, D_SSM + g * D_STATE:D_SSM + (g + 1) * D_STATE]
            cg = xc_ref[tok, D_SSM + GN + g * D_STATE:D_SSM + GN + (g + 1) * D_STATE].astype(BF16)
            cols = slice(g * GROUP_W, (g + 1) * GROUP_W)
            h_prev = h0_ref[q, cols, :]
            xs_g = xc_ref[tok, cols]
            b_tiled = jnp.concatenate([bg] * HPG, axis=0).astype(BF16)
            cb = lax.dot_general(cg, b_tiled, NT_DIMS, preferred_element_type=F32)
            m = (cb * dec[tok, g * pair_w:(g + 1) * pair_w]).astype(BF16)
            x_diag = jnp.where(blockdiag, jnp.concatenate([xs_g] * HPG, axis=0), 0.0).astype(BF16)
            y_diag = jnp.dot(m, x_diag, preferred_element_type=F32)
            y_off = lax.dot_general(cg, h_prev.astype(BF16), NT_DIMS, preferred_element_type=F32)
            ysc_ref[tok, cols] = y_diag + y_off * e_exp[tok, cols]
            xw_t = (xs_g * w_exp[tok, cols]).T.astype(BF16)
            s_g = jnp.dot(xw_t, bg.astype(BF16), preferred_element_type=F32)
            cd = jnp.concatenate(
                [jnp.broadcast_to(cd_t[g * HPG + jh:g * HPG + jh + 1, q:q + 1], (HEAD_DIM, D_STATE))
                 for jh in range(HPG)], axis=0)
            hout_ref[q, cols, :] = h_prev * cd + s_g

    y = ysc_ref[...] + xc_ref[:, 0:D_SSM] * dexp_ref[...]
    _gated_norm_store(y, zg_ref, gnw_ref, y_ref)


def _ssd_chunked(zg, xc, dt, h0, alr, expand, dexp, gnw, w_f32, out_dtype, *, nb, nc):
    t = zg.shape[0]
    lc = SSD_CHUNK
    wrows = w_f32.shape[0] // (nb * nc)
    assert wrows * nb * nc == w_f32.shape[0] and wrows % 16 == 0
    tok = lambda rows, width: pl.BlockSpec((rows, width), lambda b, c: (b * nc + c, 0))
    par = lambda r, width: pl.BlockSpec((r, width), lambda b, c: (0, 0))
    h_spec = pl.BlockSpec((1, D_SSM, D_STATE), lambda b, c: (b, 0, 0))
    return pl.pallas_call(
        functools.partial(_ssd_chunk_body, lc=lc, nc=nc),
        grid=(nb, nc),
        in_specs=[tok(lc, D_SSM), tok(lc, CONV_DIM), tok(lc, LANES), h_spec, par(1, LANES),
                  par(LANES, D_SSM), par(1, D_SSM), par(1, D_SSM), tok(wrows, w_f32.shape[1])],
        out_specs=[tok(lc, D_SSM), h_spec, tok(wrows, w_f32.shape[1])],
        out_shape=[jax.ShapeDtypeStruct((t, D_SSM), out_dtype),
                   jax.ShapeDtypeStruct((nb, D_SSM, D_STATE), F32),
                   jax.ShapeDtypeStruct(w_f32.shape, BF16)],
        scratch_shapes=[pltpu.VMEM((D_STATE, D_SSM), F32), pltpu.VMEM((lc, D_SSM), F32)],
        compiler_params=pltpu.CompilerParams(
            dimension_semantics=("arbitrary", "arbitrary"), vmem_limit_bytes=VMEM_LIMIT),
        name="ssd_scan",
    )(zg, xc, dt, h0, alr, expand, dexp, gnw, w_f32)


def _ssd_short(zg, xc, dt, h0, alr, expand, exps, dexp, gnw, out_dtype, *, nb, lc):
    t = zg.shape[0]
    spb = SHORT_SEQS_PER_STEP
    rows = spb * lc
    tok = lambda width: pl.BlockSpec((rows, width), lambda b: (b, 0))
    par = lambda r, width: pl.BlockSpec((r, width), lambda b: (0, 0))
    h_spec = pl.BlockSpec((spb, D_SSM, D_STATE), lambda b: (b, 0, 0))
    return pl.pallas_call(
        functools.partial(_ssd_short_body, lc=lc, spb=spb),
        grid=(nb // spb,),
        in_specs=[tok(D_SSM), tok(CONV_DIM), tok(LANES), h_spec, par(1, LANES), par(LANES, D_SSM),
                  par(LANES, HEADS * lc), par(1, D_SSM), par(1, D_SSM)],
        out_specs=[tok(D_SSM), h_spec],
        out_shape=[jax.ShapeDtypeStruct((t, D_SSM), out_dtype),
                   jax.ShapeDtypeStruct((nb, D_SSM, D_STATE), F32)],
        scratch_shapes=[pltpu.VMEM((rows, D_SSM), F32)],
        compiler_params=pltpu.CompilerParams(
            dimension_semantics=("arbitrary",), vmem_limit_bytes=VMEM_LIMIT),
        name="ssd_scan_short",
    )(zg, xc, dt, h0, alr, expand, exps, dexp, gnw)


def _out_body(ys_ref, yc_ref, x_ref, w1_ref, w2_ref, fnw_ref, o_ref):
    acc = jnp.dot(ys_ref[...].astype(BF16), w1_ref[...], preferred_element_type=F32)
    acc = acc + jnp.dot(yc_ref[...].astype(BF16), w2_ref[...], preferred_element_type=F32)
    r = x_ref[...] + acc
    var = jnp.mean(r * r, axis=-1, keepdims=True)
    o_ref[...] = r * lax.rsqrt(var + EPS) * fnw_ref[...]


def _out(ys, yc, x, w_o, fnw, *, tm):
    t = x.shape[0]
    row = lambda width: pl.BlockSpec((tm, width), lambda m: (m, 0))
    return pl.pallas_call(
        _out_body,
        grid=(t // tm,),
        in_specs=[row(D_SSM), row(D_SHORT), row(D_MODEL),
                  pl.BlockSpec((D_SSM, D_MODEL), lambda m: (0, 0), pipeline_mode=pl.Buffered(1)),
                  pl.BlockSpec((D_SHORT, D_MODEL), lambda m: (1, 0), pipeline_mode=pl.Buffered(1)),
                  pl.BlockSpec((1, D_MODEL), lambda m: (0, 0))],
        out_specs=row(D_MODEL),
        out_shape=jax.ShapeDtypeStruct((t, D_MODEL), F32),
        compiler_params=pltpu.CompilerParams(
            dimension_semantics=("arbitrary",), vmem_limit_bytes=VMEM_LIMIT),
        name="out_proj",
    )(ys, yc, x, w_o, w_o, fnw)


def _projections(x2d, conv_state, short_state, p, *, act_dtype, tm, nseq, tps):
    h, zg, dt = _proj_z(x2d, p["nw"], p["w_t"], p["dtb"], act_dtype, tm=tm)
    xc, conv_new = _proj_xc(h, p["w_t"], p["conv_w"], p["conv_b"], conv_state,
                            tm=tm, nseq=nseq, tps=tps)
    y_c, short_new = _short(h, p["w_t"], p["conv_short_w"], short_state, act_dtype,
                            tm=tm, tn=256, nseq=nseq, tps=tps)
    return zg, xc, dt, y_c, conv_new, short_new


def kernel(x_prompt, x_sample, state_ssm, state_conv_ssd, state_conv_short, norm_w, w_in, conv_ssd_w,
           conv_ssd_b, dt_bias, a_log, d_skip, ssd_norm_w, conv_short_w, w_out, final_norm_w):
    depth = norm_w.shape[0]
    assert depth == 1, "the output projection fuses the final rmsnorm, valid for a single layer"
    bp, lp, _ = x_prompt.shape
    bs, ls, _ = x_sample.shape
    assert lp % SSD_CHUNK == 0 and ls < SSD_CHUNK
    hp = x_prompt.reshape(bp * lp, D_MODEL)
    hs = x_sample.reshape(bs * ls, D_MODEL)
    pad_h = LANES - HEADS
    expand = (jnp.arange(LANES)[:, None] == jnp.arange(D_SSM)[None, :] // HEAD_DIM).astype(BF16)
    expand_pairs = (jnp.arange(LANES)[:, None] == jnp.arange(HEADS * ls)[None, :] // ls).astype(BF16)
    outs = [[] for _ in range(6)]
    for layer in range(depth):
        p = dict(
            nw=norm_w[layer][None, :],
            w_t=w_in[layer].T,
            conv_short_w=conv_short_w[layer],
            conv_w=conv_ssd_w[layer],
            conv_b=conv_ssd_b[layer][None, :],
            dtb=jnp.pad(dt_bias[layer], (0, pad_h))[None, :],
        )
        alr = jnp.pad(a_log[layer], (0, pad_h))[None, :]
        dexp = jnp.repeat(d_skip[layer], HEAD_DIM)[None, :]
        gnw = ssd_norm_w[layer][None, :]

        tm_p = 1024
        zg, xc, dt, yc, a2, a3 = _projections(
            hp, jnp.zeros((bp, CONV_K - 1, CONV_DIM), F32), jnp.zeros((bp, SHORT_K - 1, D_SHORT), F32), p,
            act_dtype=BF16, tm=tm_p, nseq=1, tps=lp // tm_p)
        ys, a1, w_o = _ssd_chunked(zg, xc, dt, jnp.zeros((bp, D_SSM, D_STATE), F32), alr, expand, dexp,
                                   gnw, w_out[layer], BF16, nb=bp, nc=lp // SSD_CHUNK)
        zg2, xc2, dt2, yc2, s2, s3 = _projections(
            hs, state_conv_ssd[layer], state_conv_short[layer], p,
            act_dtype=BF16, tm=bs * ls, nseq=bs, tps=1)
        ys2, s1 = _ssd_short(zg2, xc2, dt2, state_ssm[layer].reshape(bs, D_SSM, D_STATE), alr, expand,
                             expand_pairs, dexp, gnw, BF16, nb=bs, lc=ls)
        fw = final_norm_w[None, :]
        hp = _out(ys, yc, hp, w_o, fw, tm=512)
        hs = _out(ys2, yc2, hs, w_o, fw, tm=512)
        for lst, val in zip(outs, (a1.reshape(bp, HEADS, HEAD_DIM, D_STATE), a2, a3,
                                   s1.reshape(bs, HEADS, HEAD_DIM, D_STATE), s2, s3)):
            lst.append(val)
    return (hp.reshape(bp, lp, D_MODEL), hs.reshape(bs, ls, D_MODEL),
            *(jnp.stack(v) for v in outs))
```

```python
import functools

import jax
import jax.numpy as jnp
from jax import lax
from jax.experimental import pallas as pl
from jax.experimental.pallas import tpu as pltpu

F32 = jnp.float32
BF16 = jnp.bfloat16

D_MODEL = 2048
D_SSM = 2048
D_SHORT = 2048
HEADS = 32
HEAD_DIM = 64
D_STATE = 128
GROUPS = 4
HPG = HEADS // GROUPS
GROUP_W = HPG * HEAD_DIM
GN = GROUPS * D_STATE
CONV_DIM = D_SSM + 2 * GN
CONV_K = 4
SHORT_K = 3
SSD_CHUNK = 128
EPS = 1e-6
LANES = 128
SUBLANES = 8
BF16_SUBLANES = 16
VMEM_LIMIT = 56 * 1024 * 1024

ROW_TILE = 1024
TN = 1024
SHORT_ROW_TILE = 2048
SHORT_TN = 256
OUT_ROW_TILE = 512
NORM_ROWS = 32
NORM_UNROLL = 8
Z_TILES = D_SSM // TN
XC_TILES = CONV_DIM // TN
MAIN_VALID = D_SSM + CONV_DIM + HEADS
DT_BLOCK = (D_SSM + CONV_DIM) // LANES

NT_DIMS = (((1,), (1,)), ((), ()))
LOG2E = 1.4426950408889634
SHORT_SEQS_PER_STEP = 8


def _silu(x):
    half = 0.5 * x
    return half + half * jnp.tanh(half)


def _softplus(x):
    return jnp.maximum(x, 0.0) + jnp.log1p(jnp.exp(-jnp.abs(x)))


def _proj_z_body(x_ref, nw_ref, w_ref, wdt_ref, dtb_ref, h_ref, zg_ref, dt_ref, *, tm):
    @pl.when(pl.program_id(1) == 0)
    def _():
        nw = nw_ref[...]

        def body(i, carry):
            r = pl.ds(pl.multiple_of(i * NORM_ROWS, NORM_ROWS), NORM_ROWS)
            xf = x_ref[r, :]
            var = jnp.mean(xf * xf, axis=-1, keepdims=True)
            h_ref[r, :] = (xf * lax.rsqrt(var + EPS) * nw).astype(BF16)
            return carry

        lax.fori_loop(0, tm // NORM_ROWS, body, 0, unroll=NORM_UNROLL)
        dt_raw = lax.dot_general(h_ref[...], wdt_ref[...].astype(BF16), NT_DIMS,
                                 preferred_element_type=F32)
        lane = lax.broadcasted_iota(jnp.int32, dt_raw.shape, 1)
        dt_ref[...] = jnp.where(lane < HEADS, _softplus(dt_raw + dtb_ref[...]), 0.0)

    acc = lax.dot_general(h_ref[...], w_ref[...].astype(BF16), NT_DIMS, preferred_element_type=F32)
    zg_ref[...] = _silu(acc).astype(BF16)


def _proj_z(x, nw, w_t, dtb, *, tm):
    t = x.shape[0]
    return pl.pallas_call(
        functools.partial(_proj_z_body, tm=tm),
        grid=(t // tm, Z_TILES),
        in_specs=[
            pl.BlockSpec((tm, D_MODEL), lambda m, j: (m, 0)),
            pl.BlockSpec((1, D_MODEL), lambda m, j: (0, 0)),
            pl.BlockSpec((TN, D_MODEL), lambda m, j: (j, 0)),
            pl.BlockSpec((LANES, D_MODEL), lambda m, j: (DT_BLOCK, 0)),
            pl.BlockSpec((1, LANES), lambda m, j: (0, 0)),
        ],
        out_specs=[pl.BlockSpec((tm, D_MODEL), lambda m, j: (m, 0)),
                   pl.BlockSpec((tm, TN), lambda m, j: (m, j)),
                   pl.BlockSpec((tm, LANES), lambda m, j: (m, 0))],
        out_shape=[jax.ShapeDtypeStruct((t, D_MODEL), BF16),
                   jax.ShapeDtypeStruct((t, D_SSM), BF16),
                   jax.ShapeDtypeStruct((t, LANES), F32)],
        compiler_params=pltpu.CompilerParams(
            dimension_semantics=("arbitrary", "arbitrary"), vmem_limit_bytes=VMEM_LIMIT),
        name="in_proj_z",
    )(x, nw, w_t, w_t, dtb)


def _load_history(pad_ref, st_ref, carry, j, *, first, hist, tps):
    def from_state():
        pad_ref[:, 0:SUBLANES, :] = jnp.zeros((pad_ref.shape[0], SUBLANES, pad_ref.shape[2]), F32)
        pad_ref[:, SUBLANES - hist:SUBLANES, :] = st_ref[...]

    if tps > 1:
        (carry_ref,) = carry
        pl.when(first)(from_state)

        @pl.when(jnp.logical_not(first))
        def _():
            pad_ref[:, 0:SUBLANES, :] = carry_ref[j]
    else:
        from_state()


def _causal_conv(pad_ref, cw_ref, taps):
    xe = pad_ref[...]
    if taps == 4:
        s1 = pltpu.roll(xe, 1, axis=1)
        near = xe * cw_ref[3:4, :] + s1 * cw_ref[2:3, :]
        far = xe * cw_ref[1:2, :] + s1 * cw_ref[0:1, :]
        return near[:, SUBLANES:, :] + pltpu.roll(far, 2, axis=1)[:, SUBLANES:, :]
    conv = xe[:, SUBLANES:, :] * cw_ref[taps - 1:taps, :]
    for s in range(1, taps):
        shifted = pltpu.roll(xe, s, axis=1)[:, SUBLANES:, :]
        conv = conv + shifted * cw_ref[taps - 1 - s:taps - s, :]
    return conv


def _proj_xc_body(h_ref, w_ref, cw_ref, cb_ref, st_ref, xc_ref, tails_ref, pad_ref, *carry,
                  tm, nseq, tps):
    j = pl.program_id(1)
    lt = tm // nseq
    _load_history(pad_ref, st_ref, carry, j, first=(pl.program_id(0) % tps) == 0, hist=CONV_K - 1,
                  tps=tps)
    acc = lax.dot_general(h_ref[...], w_ref[...].astype(BF16), NT_DIMS, preferred_element_type=F32)
    pad_ref[:, SUBLANES:SUBLANES + lt, :] = acc.reshape(nseq, lt, TN)
    conv = _causal_conv(pad_ref, cw_ref, CONV_K)
    xc_ref[...] = _silu(conv + cb_ref[...]).reshape(tm, TN)
    tails_ref[...] = pad_ref[:, lt + SUBLANES - (CONV_K - 1):lt + SUBLANES, :]
    if tps > 1:
        carry[0][j] = pad_ref[:, lt:lt + SUBLANES, :]


def _proj_xc(h, w_t, conv_w, conv_b, conv_state, *, tm, nseq, tps):
    t = h.shape[0]
    lt = tm // nseq
    scratch = [pltpu.VMEM((nseq, SUBLANES + lt, TN), F32)]
    if tps > 1:
        scratch.append(pltpu.VMEM((XC_TILES, nseq, SUBLANES, TN), F32))
    xc, tails = pl.pallas_call(
        functools.partial(_proj_xc_body, tm=tm, nseq=nseq, tps=tps),
        grid=(t // tm, XC_TILES),
        in_specs=[
            pl.BlockSpec((tm, D_MODEL), lambda m, j: (m, 0)),
            pl.BlockSpec((TN, D_MODEL), lambda m, j: (Z_TILES + j, 0)),
            pl.BlockSpec((CONV_K, TN), lambda m, j: (0, j)),
            pl.BlockSpec((1, TN), lambda m, j: (0, j)),
            pl.BlockSpec((nseq, CONV_K - 1, TN), lambda m, j: (m // tps, 0, j)),
        ],
        out_specs=[pl.BlockSpec((tm, TN), lambda m, j: (m, j)),
                   pl.BlockSpec((nseq, CONV_K - 1, TN), lambda m, j: (m, 0, j))],
        out_shape=[jax.ShapeDtypeStruct((t, CONV_DIM), F32),
                   jax.ShapeDtypeStruct(((t // tm) * nseq, CONV_K - 1, CONV_DIM), F32)],
        scratch_shapes=scratch,
        compiler_params=pltpu.CompilerParams(
            dimension_semantics=("arbitrary", "arbitrary"), vmem_limit_bytes=VMEM_LIMIT),
        name="in_proj_xc",
    )(h, w_t, conv_w, conv_b, conv_state)
    tails = tails.reshape(t // (tm * tps), tps, nseq, CONV_K - 1, CONV_DIM)[:, -1]
    return xc, tails.reshape(conv_state.shape)


def _short_body(h_ref, *refs, tm, tn, nseq, tps, shift):
    w_refs, (cw_ref, st_ref, y_ref, ns_ref, wbf_ref, pad_ref), carry = refs[:8], refs[8:14], refs[14:]
    m = pl.program_id(1)
    lt = tm // nseq

    @pl.when(m == 0)
    def _():
        for k in range(4):
            a_ref, b_ref = w_refs[2 * k], w_refs[2 * k + 1]
            wbf_ref[k] = jnp.concatenate([a_ref[shift:, :], b_ref[...]], axis=0).astype(BF16)

    _load_history(pad_ref, st_ref, carry, 0, first=(m % tps) == 0, hist=SHORT_K - 1, tps=tps)
    h = h_ref[...]
    band = lambda k: lax.dot_general(h, wbf_ref[k], NT_DIMS, preferred_element_type=F32)
    c = band(2)
    v = band(3)
    pad_ref[:, SUBLANES:SUBLANES + lt, :] = (c * v).reshape(nseq, lt, tn)
    conv = _causal_conv(pad_ref, cw_ref, SHORT_K)
    b = band(1)
    z = band(0)
    y = b * conv.reshape(tm, tn) * _silu(z)
    y_ref[...] = y.astype(y_ref.dtype)
    ns_ref[...] = pad_ref[:, lt + SUBLANES - (SHORT_K - 1):lt + SUBLANES, :]
    if tps > 1:
        carry[0][0] = pad_ref[:, lt:lt + SUBLANES, :]


def _short(h, w_t, conv_w, state, *, tm, tn, nseq, tps):
    t = h.shape[0]
    nj = D_SHORT // tn
    lt = tm // nseq
    base = (MAIN_VALID // tn) * tn
    shift = MAIN_VALID - base
    assert shift % SUBLANES == 0 and tn % shift == 0 and base % shift == 0
    scratch = [pltpu.VMEM((4, tn, D_MODEL), BF16), pltpu.VMEM((nseq, SUBLANES + lt, tn), F32)]
    if tps > 1:
        scratch.append(pltpu.VMEM((1, nseq, SUBLANES, tn), F32))
    w_specs = []
    for k in range(4):
        r0 = base + k * D_SHORT
        w_specs.append(pl.BlockSpec((tn, D_MODEL), lambda j, m, r0=r0: (r0 // tn + j, 0)))
        w_specs.append(pl.BlockSpec((shift, D_MODEL), lambda j, m, r0=r0: ((r0 + tn * (j + 1)) // shift, 0)))

    y, tails = pl.pallas_call(
        functools.partial(_short_body, tm=tm, tn=tn, nseq=nseq, tps=tps, shift=shift),
        grid=(nj, t // tm),
        in_specs=[
            pl.BlockSpec((tm, D_MODEL), lambda j, m: (m, 0)),
            *w_specs,
            pl.BlockSpec((SHORT_K, tn), lambda j, m: (0, j)),
            pl.BlockSpec((nseq, SHORT_K - 1, tn), lambda j, m: (m // tps, 0, j)),
        ],
        out_specs=[pl.BlockSpec((tm, tn), lambda j, m: (m, j)),
                   pl.BlockSpec((nseq, SHORT_K - 1, tn), lambda j, m: (m, 0, j))],
        out_shape=[jax.ShapeDtypeStruct((t, D_SHORT), BF16),
                   jax.ShapeDtypeStruct(((t // tm) * nseq, SHORT_K - 1, D_SHORT), F32)],
        scratch_shapes=scratch,
        compiler_params=pltpu.CompilerParams(
            dimension_semantics=("arbitrary", "arbitrary"), vmem_limit_bytes=VMEM_LIMIT),
        name="short_conv",
    )(h, *([w_t] * 8), conv_w, state)
    tails = tails.reshape(t // (tm * tps), tps, nseq, SHORT_K - 1, D_SHORT)[:, -1]
    return y, tails.reshape(state.shape)


def _dot_split(x, w, pieces):
    out = None
    r = x
    for _ in range(pieces):
        p = r.astype(BF16)
        d = jnp.dot(p, w, preferred_element_type=F32)
        out = d if out is None else out + d
        r = r - p.astype(F32)
    return out


def _prefix_sum_rows(x, period):
    t = lax.broadcasted_iota(jnp.int32, x.shape, 0) % period
    k = 1
    while k < period:
        x = x + jnp.where(t >= k, pltpu.roll(x, k, axis=0), 0.0)
        k *= 2
    return x


def _gated_norm_store(y, zg_ref, gnw_ref, y_ref):
    y = y * zg_ref[...].astype(F32)
    for g in range(GROUPS):
        cs = slice(g * GROUP_W, (g + 1) * GROUP_W)
        blk = y[:, cs]
        ms = jnp.mean(blk * blk, axis=-1, keepdims=True)
        y_ref[:, cs] = (blk * lax.rsqrt(ms + EPS) * gnw_ref[:, cs]).astype(y_ref.dtype)


def _ssd_chunk_body(zg_ref, xc_ref, dt_ref, h0_ref, alr_ref, expand_ref, dexp_ref, gnw_ref, wsrc_ref,
                    y_ref, hout_ref, wdst_ref, hst_ref, ysc_ref, *, lc, nc):
    c = pl.program_id(1)
    wdst_ref[...] = wsrc_ref[...].astype(BF16)

    @pl.when(c == 0)
    def _():
        hst_ref[...] = h0_ref[0].T

    row_i = lax.broadcasted_iota(jnp.int32, (lc, lc), 0)
    col_i = lax.broadcasted_iota(jnp.int32, (lc, lc), 1)
    causal = row_i >= col_i
    lane_lo = lax.broadcasted_iota(jnp.int32, (lc, LANES), 1) < HEAD_DIM

    dt = dt_ref[...]
    la = dt * (-jnp.exp(alr_ref[...]))
    a_cum = _prefix_sum_rows(la, lc)
    a_last = a_cum[lc - 1:lc, :]
    factors = jnp.concatenate([jnp.exp(a_cum), jnp.exp(a_last - a_cum) * dt], axis=0).astype(BF16)
    a2 = a_cum * LOG2E
    r_t = (a2 - jnp.log2(dt)).T

    for g in range(GROUPS):
        bg = xc_ref[:, D_SSM + g * D_STATE:D_SSM + (g + 1) * D_STATE]
        cg = xc_ref[:, D_SSM + GN + g * D_STATE:D_SSM + GN + (g + 1) * D_STATE].astype(BF16)
        cols = slice(g * GROUP_W, (g + 1) * GROUP_W)
        h_prev = hst_ref[:, cols]
        cb = lax.dot_general(cg, bg.astype(BF16), NT_DIMS, preferred_element_type=F32)
        y_off = jnp.dot(cg, h_prev.astype(BF16), preferred_element_type=F32)
        xs_g = xc_ref[:, cols]
        both = jnp.dot(factors, expand_ref[:, cols], preferred_element_type=F32)
        e_exp, w_exp = both[:lc], both[lc:]
        for pr in range(HPG // 2):
            mats = []
            for hh in (g * HPG + 2 * pr, g * HPG + 2 * pr + 1):
                seg = a2[:, hh:hh + 1] - r_t[hh:hh + 1, :]
                dec = jnp.exp2(jnp.where(causal, seg, -jnp.inf))
                mats.append((cb * dec).astype(BF16))
            xp = xs_g[:, pr * LANES:(pr + 1) * LANES].astype(BF16)
            rhs = jnp.concatenate([jnp.where(lane_lo, xp, 0), jnp.where(lane_lo, 0, xp)], axis=0)
            y_diag = jnp.dot(jnp.concatenate(mats, axis=1), rhs, preferred_element_type=F32)
            pc = slice(g * GROUP_W + pr * LANES, g * GROUP_W + (pr + 1) * LANES)
            ysc_ref[:, pc] = y_diag + (y_off * e_exp)[:, pr * LANES:(pr + 1) * LANES]
        xw = (xs_g * w_exp).astype(BF16)
        s_t = jnp.dot(bg.T.astype(BF16), xw, preferred_element_type=F32)
        hst_ref[:, cols] = h_prev * e_exp[lc - 1:lc, :] + s_t

    y = ysc_ref[...] + xc_ref[:, 0:D_SSM] * dexp_ref[...]
    _gated_norm_store(y, zg_ref, gnw_ref, y_ref)

    @pl.when(c == nc - 1)
    def _():
        hout_ref[0] = hst_ref[...].T


def _ssd_short_body(zg_ref, xc_ref, dt_ref, h0_ref, alr_ref, expand_ref, exps_ref, dexp_ref, gnw_ref,
                    y_ref, hout_ref, ysc_ref, *, lc, spb):
    rows = spb * lc
    hl = HEADS * lc
    dt = dt_ref[...]
    a_cum = _prefix_sum_rows(dt * (-jnp.exp(alr_ref[...])), lc)
    a_last = a_cum.reshape(spb, lc, LANES)[:, lc - 1:lc, :]
    a_last_b = jnp.broadcast_to(a_last, (spb, lc, LANES)).reshape(rows, LANES)
    expand = expand_ref[...]
    e_exp = _dot_split(jnp.exp(a_cum), expand, 2)
    w_exp = _dot_split(jnp.exp(a_last_b - a_cum) * dt, expand, 2)
    cd_t = jnp.exp(a_last.reshape(spb, LANES)).T

    z2 = _dot_split(jnp.concatenate([a_cum, dt], axis=0), exps_ref[...], 3)
    zc, zd = z2[:rows], z2[rows:]
    trow = lax.broadcasted_iota(jnp.int32, (rows, hl), 0) % lc
    tsrc = lax.broadcasted_iota(jnp.int32, (rows, hl), 1) % lc
    diag = trow == tsrc

    def per_source(z):
        d = jnp.where(diag, z, 0.0).reshape(spb, lc, hl).sum(axis=1, keepdims=True)
        return jnp.broadcast_to(d, (spb, lc, hl)).reshape(rows, hl)

    dec = jnp.exp(jnp.where(trow >= tsrc, zc - per_source(zc), -jnp.inf)) * per_source(zd)

    pair_w = HPG * lc
    blockdiag = (lax.broadcasted_iota(jnp.int32, (pair_w, GROUP_W), 0) // lc
                 == lax.broadcasted_iota(jnp.int32, (pair_w, GROUP_W), 1) // HEAD_DIM)

    for q in range(spb):
        tok = slice(q * lc, (q + 1) * lc)
        for g in range(GROUPS):
            bg = xc_ref[tok, D_SSM + g * D_STATE:D_SSM + (g + 1) * D_STATE]
            cg = xc_ref[tok, D_SSM + GN + g * D_STATE:D_SSM + GN + (g + 1) * D_STATE].astype(BF16)
            cols = slice(g * GROUP_W, (g + 1) * GROUP_W)
            h_prev = h0_ref[q, cols, :]
            xs_g = xc_ref[tok, cols]
            b_tiled = jnp.concatenate([bg] * HPG, axis=0).astype(BF16)
            cb = lax.dot_general(cg, b_tiled, NT_DIMS, preferred_element_type=F32)
            m = (cb * dec[tok, g * pair_w:(g + 1) * pair_w]).astype(BF16)
            x_diag = jnp.where(blockdiag, jnp.concatenate([xs_g] * HPG, axis=0), 0.0).astype(BF16)
            y_diag = jnp.dot(m, x_diag, preferred_element_type=F32)
            y_off = lax.dot_general(cg, h_prev.astype(BF16), NT_DIMS, preferred_element_type=F32)
            ysc_ref[tok, cols] = y_diag + y_off * e_exp[tok, cols]
            xw_t = (xs_g * w_exp[tok, cols]).T.astype(BF16)
            s_g = jnp.dot(xw_t, bg.astype(BF16), preferred_element_type=F32)
            cd = jnp.concatenate(
                [jnp.broadcast_to(cd_t[g * HPG + jh:g * HPG + jh + 1, q:q + 1], (HEAD_DIM, D_STATE))
                 for jh in range(HPG)], axis=0)
            hout_ref[q, cols, :] = h_prev * cd + s_g

    y = ysc_ref[...] + xc_ref[:, 0:D_SSM] * dexp_ref[...]
    _gated_norm_store(y, zg_ref, gnw_ref, y_ref)


def _ssd_chunked(zg, xc, dt, h0, alr, expand, dexp, gnw, w_f32, *, nb, nc):
    t = zg.shape[0]
    lc = SSD_CHUNK
    wrows = w_f32.shape[0] // (nb * nc)
    assert wrows * nb * nc == w_f32.shape[0] and wrows % BF16_SUBLANES == 0
    tok = lambda rows, width: pl.BlockSpec((rows, width), lambda b, c: (b * nc + c, 0))
    par = lambda r, width: pl.BlockSpec((r, width), lambda b, c: (0, 0))
    h_spec = pl.BlockSpec((1, D_SSM, D_STATE), lambda b, c: (b, 0, 0))
    return pl.pallas_call(
        functools.partial(_ssd_chunk_body, lc=lc, nc=nc),
        grid=(nb, nc),
        in_specs=[tok(lc, D_SSM), tok(lc, CONV_DIM), tok(lc, LANES), h_spec, par(1, LANES),
                  par(LANES, D_SSM), par(1, D_SSM), par(1, D_SSM), tok(wrows, w_f32.shape[1])],
        out_specs=[tok(lc, D_SSM), h_spec, tok(wrows, w_f32.shape[1])],
        out_shape=[jax.ShapeDtypeStruct((t, D_SSM), BF16),
                   jax.ShapeDtypeStruct((nb, D_SSM, D_STATE), F32),
                   jax.ShapeDtypeStruct(w_f32.shape, BF16)],
        scratch_shapes=[pltpu.VMEM((D_STATE, D_SSM), F32), pltpu.VMEM((lc, D_SSM), F32)],
        compiler_params=pltpu.CompilerParams(
            dimension_semantics=("arbitrary", "arbitrary"), vmem_limit_bytes=VMEM_LIMIT),
        name="ssd_scan",
    )(zg, xc, dt, h0, alr, expand, dexp, gnw, w_f32)


def _ssd_short(zg, xc, dt, h0, alr, expand, exps, dexp, gnw, *, nb, lc):
    t = zg.shape[0]
    spb = SHORT_SEQS_PER_STEP
    rows = spb * lc
    tok = lambda width: pl.BlockSpec((rows, width), lambda b: (b, 0))
    par = lambda r, width: pl.BlockSpec((r, width), lambda b: (0, 0))
    h_spec = pl.BlockSpec((spb, D_SSM, D_STATE), lambda b: (b, 0, 0))
    return pl.pallas_call(
        functools.partial(_ssd_short_body, lc=lc, spb=spb),
        grid=(nb // spb,),
        in_specs=[tok(D_SSM), tok(CONV_DIM), tok(LANES), h_spec, par(1, LANES), par(LANES, D_SSM),
                  par(LANES, HEADS * lc), par(1, D_SSM), par(1, D_SSM)],
        out_specs=[tok(D_SSM), h_spec],
        out_shape=[jax.ShapeDtypeStruct((t, D_SSM), BF16),
                   jax.ShapeDtypeStruct((nb, D_SSM, D_STATE), F32)],
        scratch_shapes=[pltpu.VMEM((rows, D_SSM), F32)],
        compiler_params=pltpu.CompilerParams(
            dimension_semantics=("arbitrary",), vmem_limit_bytes=VMEM_LIMIT),
        name="ssd_scan_short",
    )(zg, xc, dt, h0, alr, expand, exps, dexp, gnw)


def _out_body(ys_ref, yc_ref, x_ref, w1_ref, w2_ref, fnw_ref, o_ref):
    acc = jnp.dot(ys_ref[...], w1_ref[...], preferred_element_type=F32)
    acc = acc + jnp.dot(yc_ref[...], w2_ref[...], preferred_element_type=F32)
    r = x_ref[...] + acc
    var = jnp.mean(r * r, axis=-1, keepdims=True)
    o_ref[...] = r * lax.rsqrt(var + EPS) * fnw_ref[...]


def _out(ys, yc, x, w_o, fnw, *, tm):
    t = x.shape[0]
    row = lambda width: pl.BlockSpec((tm, width), lambda m: (m, 0))
    return pl.pallas_call(
        _out_body,
        grid=(t // tm,),
        in_specs=[row(D_SSM), row(D_SHORT), row(D_MODEL),
                  pl.BlockSpec((D_SSM, D_MODEL), lambda m: (0, 0), pipeline_mode=pl.Buffered(1)),
                  pl.BlockSpec((D_SHORT, D_MODEL), lambda m: (1, 0), pipeline_mode=pl.Buffered(1)),
                  pl.BlockSpec((1, D_MODEL), lambda m: (0, 0))],
        out_specs=row(D_MODEL),
        out_shape=jax.ShapeDtypeStruct((t, D_MODEL), F32),
        compiler_params=pltpu.CompilerParams(
            dimension_semantics=("arbitrary",), vmem_limit_bytes=VMEM_LIMIT),
        name="out_proj",
    )(ys, yc, x, w_o, w_o, fnw)


def _projections(x2d, conv_state, short_state, p, *, seqlen):
    tm = ROW_TILE
    nseq = max(tm // seqlen, 1)
    tps = max(seqlen // tm, 1)
    assert x2d.shape[0] % tm == 0 and (seqlen % tm == 0 or tm % seqlen == 0)
    h, zg, dt = _proj_z(x2d, p["nw"], p["w_t"], p["dtb"], tm=tm)
    xc, conv_new = _proj_xc(h, p["w_t"], p["conv_w"], p["conv_b"], conv_state,
                            tm=tm, nseq=nseq, tps=tps)
    tm_s = min(SHORT_ROW_TILE, x2d.shape[0])
    assert seqlen % tm_s == 0 or tm_s % seqlen == 0
    y_c, short_new = _short(h, p["w_t"], p["conv_short_w"], short_state,
                            tm=tm_s, tn=SHORT_TN, nseq=max(tm_s // seqlen, 1), tps=max(seqlen // tm_s, 1))
    return zg, xc, dt, y_c, conv_new, short_new


def kernel(x_prompt, x_sample, state_ssm, state_conv_ssd, state_conv_short, norm_w, w_in, conv_ssd_w,
           conv_ssd_b, dt_bias, a_log, d_skip, ssd_norm_w, conv_short_w, w_out, final_norm_w):
    depth = norm_w.shape[0]
    assert depth == 1, "the output projection fuses the final rmsnorm, valid for a single layer"
    bp, lp, _ = x_prompt.shape
    bs, ls, _ = x_sample.shape
    assert lp % SSD_CHUNK == 0 and ls < SSD_CHUNK
    hp = x_prompt.reshape(bp * lp, D_MODEL)
    hs = x_sample.reshape(bs * ls, D_MODEL)
    pad_h = LANES - HEADS
    expand = (jnp.arange(LANES)[:, None] == jnp.arange(D_SSM)[None, :] // HEAD_DIM).astype(BF16)
    expand_pairs = (jnp.arange(LANES)[:, None] == jnp.arange(HEADS * ls)[None, :] // ls).astype(BF16)
    outs = [[] for _ in range(6)]
    for layer in range(depth):
        p = dict(
            nw=norm_w[layer][None, :],
            w_t=w_in[layer].T,
            conv_short_w=conv_short_w[layer],
            conv_w=conv_ssd_w[layer],
            conv_b=conv_ssd_b[layer][None, :],
            dtb=jnp.pad(dt_bias[layer], (0, pad_h))[None, :],
        )
        alr = jnp.pad(a_log[layer], (0, pad_h))[None, :]
        dexp = jnp.repeat(d_skip[layer], HEAD_DIM)[None, :]
        gnw = ssd_norm_w[layer][None, :]

        zg, xc, dt, yc, a2, a3 = _projections(
            hp, jnp.zeros((bp, CONV_K - 1, CONV_DIM), F32), jnp.zeros((bp, SHORT_K - 1, D_SHORT), F32), p,
            seqlen=lp)
        ys, a1, w_o = _ssd_chunked(zg, xc, dt, jnp.zeros((bp, D_SSM, D_STATE), F32), alr, expand, dexp,
                                   gnw, w_out[layer], nb=bp, nc=lp // SSD_CHUNK)
        zg2, xc2, dt2, yc2, s2, s3 = _projections(
            hs, state_conv_ssd[layer], state_conv_short[layer], p, seqlen=ls)
        ys2, s1 = _ssd_short(zg2, xc2, dt2, state_ssm[layer].reshape(bs, D_SSM, D_STATE), alr, expand,
                             expand_pairs, dexp, gnw, nb=bs, lc=ls)
        fw = final_norm_w[None, :]
        hp = _out(ys, yc, hp, w_o, fw, tm=OUT_ROW_TILE)
        hs = _out(ys2, yc2, hs, w_o, fw, tm=OUT_ROW_TILE)
        for lst, val in zip(outs, (a1.reshape(bp, HEADS, HEAD_DIM, D_STATE), a2, a3,
                                   s1.reshape(bs, HEADS, HEAD_DIM, D_STATE), s2, s3)):
            lst.append(val)
    return (hp.reshape(bp, lp, D_MODEL), hs.reshape(bs, ls, D_MODEL),
            *(jnp.stack(v) for v in outs))
```

```python
import functools

import jax
import jax.numpy as jnp
from jax import lax
from jax.experimental import pallas as pl
from jax.experimental.pallas import tpu as pltpu

F32 = jnp.float32
BF16 = jnp.bfloat16

D_MODEL = 2048
D_SSM = 2048
D_SHORT = 2048
HEADS = 32
HEAD_DIM = 64
D_STATE = 128
GROUPS = 4
HPG = HEADS // GROUPS
GROUP_W = HPG * HEAD_DIM
GN = GROUPS * D_STATE
CONV_DIM = D_SSM + 2 * GN
CONV_K = 4
SHORT_K = 3
SSD_CHUNK = 128
EPS = 1e-6
LANES = 128
SUBLANES = 8
BF16_SUBLANES = 16
VMEM_LIMIT = 56 * 1024 * 1024

ROW_TILE = 1024
TN = 1024
SHORT_ROW_TILE = 2048
SHORT_TN = 256
OUT_ROW_TILE = 512
NORM_ROWS = 32
NORM_UNROLL = 8
Z_TILES = D_SSM // TN
XC_TILES = CONV_DIM // TN
MAIN_VALID = D_SSM + CONV_DIM + HEADS
DT_BLOCK = (D_SSM + CONV_DIM) // LANES

NT_DIMS = (((1,), (1,)), ((), ()))
LOG2E = 1.4426950408889634
SHORT_SEQS_PER_STEP = 8


def _silu(x):
    half = 0.5 * x
    return half + half * jnp.tanh(half)


def _softplus(x):
    return jnp.maximum(x, 0.0) + jnp.log1p(jnp.exp(-jnp.abs(x)))


def _proj_z_body(x_ref, nw_ref, w_ref, wdt_ref, dtb_ref, h_ref, zg_ref, dt_ref, wbf_ref, *, tm):
    j = pl.program_id(1)

    @pl.when(pl.program_id(0) == 0)
    def _():
        wbf_ref[j] = w_ref[...].astype(BF16)

    @pl.when(j == 0)
    def _():
        nw = nw_ref[...]

        def body(i, carry):
            r = pl.ds(pl.multiple_of(i * NORM_ROWS, NORM_ROWS), NORM_ROWS)
            xf = x_ref[r, :]
            var = jnp.mean(xf * xf, axis=-1, keepdims=True)
            h_ref[r, :] = (xf * lax.rsqrt(var + EPS) * nw).astype(BF16)
            return carry

        lax.fori_loop(0, tm // NORM_ROWS, body, 0, unroll=NORM_UNROLL)
        dt_raw = lax.dot_general(h_ref[...], wdt_ref[...].astype(BF16), NT_DIMS,
                                 preferred_element_type=F32)
        lane = lax.broadcasted_iota(jnp.int32, dt_raw.shape, 1)
        dt_ref[...] = jnp.where(lane < HEADS, _softplus(dt_raw + dtb_ref[...]), 0.0)

    acc = lax.dot_general(h_ref[...], wbf_ref[j], NT_DIMS, preferred_element_type=F32)
    zg_ref[...] = _silu(acc).astype(BF16)


def _proj_z(x, nw, w_t, dtb, *, tm):
    t = x.shape[0]
    return pl.pallas_call(
        functools.partial(_proj_z_body, tm=tm),
        grid=(t // tm, Z_TILES),
        in_specs=[
            pl.BlockSpec((tm, D_MODEL), lambda m, j: (m, 0)),
            pl.BlockSpec((1, D_MODEL), lambda m, j: (0, 0)),
            pl.BlockSpec((TN, D_MODEL), lambda m, j: (jnp.where(m == 0, j, Z_TILES - 1), 0),
                         pipeline_mode=pl.Buffered(1)),
            pl.BlockSpec((LANES, D_MODEL), lambda m, j: (DT_BLOCK, 0)),
            pl.BlockSpec((1, LANES), lambda m, j: (0, 0)),
        ],
        out_specs=[pl.BlockSpec((tm, D_MODEL), lambda m, j: (m, 0)),
                   pl.BlockSpec((tm, TN), lambda m, j: (m, j)),
                   pl.BlockSpec((tm, LANES), lambda m, j: (m, 0))],
        out_shape=[jax.ShapeDtypeStruct((t, D_MODEL), BF16),
                   jax.ShapeDtypeStruct((t, D_SSM), BF16),
                   jax.ShapeDtypeStruct((t, LANES), F32)],
        scratch_shapes=[pltpu.VMEM((Z_TILES, TN, D_MODEL), BF16)],
        compiler_params=pltpu.CompilerParams(
            dimension_semantics=("arbitrary", "arbitrary"), vmem_limit_bytes=VMEM_LIMIT),
        name="in_proj_z",
    )(x, nw, w_t, w_t, dtb)


def _load_history(pad_ref, st_ref, carry, j, *, first, hist, tps):
    def from_state():
        pad_ref[:, 0:SUBLANES, :] = jnp.zeros((pad_ref.shape[0], SUBLANES, pad_ref.shape[2]), F32)
        pad_ref[:, SUBLANES - hist:SUBLANES, :] = st_ref[...]

    if tps > 1:
        (carry_ref,) = carry
        pl.when(first)(from_state)

        @pl.when(jnp.logical_not(first))
        def _():
            pad_ref[:, 0:SUBLANES, :] = carry_ref[j]
    else:
        from_state()


def _causal_conv(pad_ref, cw_ref, taps):
    xe = pad_ref[...]
    if taps == 4:
        s1 = pltpu.roll(xe, 1, axis=1)
        near = xe * cw_ref[3:4, :] + s1 * cw_ref[2:3, :]
        far = xe * cw_ref[1:2, :] + s1 * cw_ref[0:1, :]
        return near[:, SUBLANES:, :] + pltpu.roll(far, 2, axis=1)[:, SUBLANES:, :]
    conv = xe[:, SUBLANES:, :] * cw_ref[taps - 1:taps, :]
    for s in range(1, taps):
        shifted = pltpu.roll(xe, s, axis=1)[:, SUBLANES:, :]
        conv = conv + shifted * cw_ref[taps - 1 - s:taps - s, :]
    return conv


def _proj_xc_body(h_ref, w_ref, cw_ref, cb_ref, st_ref, xc_ref, tails_ref, wbf_ref, pad_ref, *carry,
                  tm, nseq, tps):
    j = pl.program_id(1)

    @pl.when(pl.program_id(0) == 0)
    def _():
        wbf_ref[j] = w_ref[...].astype(BF16)

    lt = tm // nseq
    _load_history(pad_ref, st_ref, carry, j, first=(pl.program_id(0) % tps) == 0, hist=CONV_K - 1,
                  tps=tps)
    acc = lax.dot_general(h_ref[...], wbf_ref[j], NT_DIMS, preferred_element_type=F32)
    pad_ref[:, SUBLANES:SUBLANES + lt, :] = acc.reshape(nseq, lt, TN)
    conv = _causal_conv(pad_ref, cw_ref, CONV_K)
    xc_ref[...] = _silu(conv + cb_ref[...]).reshape(tm, TN)
    tails_ref[...] = pad_ref[:, lt + SUBLANES - (CONV_K - 1):lt + SUBLANES, :]
    if tps > 1:
        carry[0][j] = pad_ref[:, lt:lt + SUBLANES, :]


def _proj_xc(h, w_t, conv_w, conv_b, conv_state, *, tm, nseq, tps):
    t = h.shape[0]
    lt = tm // nseq
    scratch = [pltpu.VMEM((XC_TILES, TN, D_MODEL), BF16), pltpu.VMEM((nseq, SUBLANES + lt, TN), F32)]
    if tps > 1:
        scratch.append(pltpu.VMEM((XC_TILES, nseq, SUBLANES, TN), F32))
    xc, tails = pl.pallas_call(
        functools.partial(_proj_xc_body, tm=tm, nseq=nseq, tps=tps),
        grid=(t // tm, XC_TILES),
        in_specs=[
            pl.BlockSpec((tm, D_MODEL), lambda m, j: (m, 0)),
            pl.BlockSpec((TN, D_MODEL), lambda m, j: (Z_TILES + jnp.where(m == 0, j, XC_TILES - 1), 0),
                         pipeline_mode=pl.Buffered(1)),
            pl.BlockSpec((CONV_K, TN), lambda m, j: (0, j)),
            pl.BlockSpec((1, TN), lambda m, j: (0, j)),
            pl.BlockSpec((nseq, CONV_K - 1, TN), lambda m, j: (m // tps, 0, j)),
        ],
        out_specs=[pl.BlockSpec((tm, TN), lambda m, j: (m, j)),
                   pl.BlockSpec((nseq, CONV_K - 1, TN), lambda m, j: (m, 0, j))],
        out_shape=[jax.ShapeDtypeStruct((t, CONV_DIM), F32),
                   jax.ShapeDtypeStruct(((t // tm) * nseq, CONV_K - 1, CONV_DIM), F32)],
        scratch_shapes=scratch,
        compiler_params=pltpu.CompilerParams(
            dimension_semantics=("arbitrary", "arbitrary"), vmem_limit_bytes=VMEM_LIMIT),
        name="in_proj_xc",
    )(h, w_t, conv_w, conv_b, conv_state)
    tails = tails.reshape(t // (tm * tps), tps, nseq, CONV_K - 1, CONV_DIM)[:, -1]
    return xc, tails.reshape(conv_state.shape)


def _short_body(h_ref, *refs, tm, tn, nseq, tps, shift):
    w_refs, (cw_ref, st_ref, y_ref, ns_ref, wbf_ref, pad_ref), carry = refs[:8], refs[8:14], refs[14:]
    m = pl.program_id(1)
    lt = tm // nseq

    @pl.when(m == 0)
    def _():
        for k in range(4):
            a_ref, b_ref = w_refs[2 * k], w_refs[2 * k + 1]
            wbf_ref[k] = jnp.concatenate([a_ref[shift:, :], b_ref[...]], axis=0).astype(BF16)

    _load_history(pad_ref, st_ref, carry, 0, first=(m % tps) == 0, hist=SHORT_K - 1, tps=tps)
    h = h_ref[...]
    band = lambda k: lax.dot_general(h, wbf_ref[k], NT_DIMS, preferred_element_type=F32)
    c = band(2)
    v = band(3)
    pad_ref[:, SUBLANES:SUBLANES + lt, :] = (c * v).reshape(nseq, lt, tn)
    conv = _causal_conv(pad_ref, cw_ref, SHORT_K)
    b = band(1)
    z = band(0)
    y = b * conv.reshape(tm, tn) * _silu(z)
    y_ref[...] = y.astype(y_ref.dtype)
    ns_ref[...] = pad_ref[:, lt + SUBLANES - (SHORT_K - 1):lt + SUBLANES, :]
    if tps > 1:
        carry[0][0] = pad_ref[:, lt:lt + SUBLANES, :]


def _short(h, w_t, conv_w, state, *, tm, tn, nseq, tps):
    t = h.shape[0]
    nj = D_SHORT // tn
    lt = tm // nseq
    base = (MAIN_VALID // tn) * tn
    shift = MAIN_VALID - base
    assert shift % SUBLANES == 0 and tn % shift == 0 and base % shift == 0
    scratch = [pltpu.VMEM((4, tn, D_MODEL), BF16), pltpu.VMEM((nseq, SUBLANES + lt, tn), F32)]
    if tps > 1:
        scratch.append(pltpu.VMEM((1, nseq, SUBLANES, tn), F32))
    w_specs = []
    for k in range(4):
        r0 = base + k * D_SHORT
        w_specs.append(pl.BlockSpec((tn, D_MODEL), lambda j, m, r0=r0: (r0 // tn + j, 0)))
        w_specs.append(pl.BlockSpec((shift, D_MODEL), lambda j, m, r0=r0: ((r0 + tn * (j + 1)) // shift, 0)))

    y, tails = pl.pallas_call(
        functools.partial(_short_body, tm=tm, tn=tn, nseq=nseq, tps=tps, shift=shift),
        grid=(nj, t // tm),
        in_specs=[
            pl.BlockSpec((tm, D_MODEL), lambda j, m: (m, 0)),
            *w_specs,
            pl.BlockSpec((SHORT_K, tn), lambda j, m: (0, j)),
            pl.BlockSpec((nseq, SHORT_K - 1, tn), lambda j, m: (m // tps, 0, j)),
        ],
        out_specs=[pl.BlockSpec((tm, tn), lambda j, m: (m, j)),
                   pl.BlockSpec((nseq, SHORT_K - 1, tn), lambda j, m: (m, 0, j))],
        out_shape=[jax.ShapeDtypeStruct((t, D_SHORT), BF16),
                   jax.ShapeDtypeStruct(((t // tm) * nseq, SHORT_K - 1, D_SHORT), F32)],
        scratch_shapes=scratch,
        compiler_params=pltpu.CompilerParams(
            dimension_semantics=("arbitrary", "arbitrary"), vmem_limit_bytes=VMEM_LIMIT),
        name="short_conv",
    )(h, *([w_t] * 8), conv_w, state)
    tails = tails.reshape(t // (tm * tps), tps, nseq, SHORT_K - 1, D_SHORT)[:, -1]
    return y, tails.reshape(state.shape)


def _dot_split(x, w, pieces):
    out = None
    r = x
    for _ in range(pieces):
        p = r.astype(BF16)
        d = jnp.dot(p, w, preferred_element_type=F32)
        out = d if out is None else out + d
        r = r - p.astype(F32)
    return out


def _prefix_sum_rows(x, period):
    t = lax.broadcasted_iota(jnp.int32, x.shape, 0) % period
    k = 1
    while k < period:
        x = x + jnp.where(t >= k, pltpu.roll(x, k, axis=0), 0.0)
        k *= 2
    return x


def _gated_norm_store(y, zg_ref, gnw_ref, y_ref):
    y = y * zg_ref[...].astype(F32)
    for g in range(GROUPS):
        cs = slice(g * GROUP_W, (g + 1) * GROUP_W)
        blk = y[:, cs]
        ms = jnp.mean(blk * blk, axis=-1, keepdims=True)
        y_ref[:, cs] = (blk * lax.rsqrt(ms + EPS) * gnw_ref[:, cs]).astype(y_ref.dtype)


def _ssd_chunk_body(zg_ref, xc_ref, dt_ref, h0_ref, alr_ref, expand_ref, dexp_ref, gnw_ref, wsrc_ref,
                    y_ref, hout_ref, wdst_ref, hst_ref, ysc_ref, *, lc, nc):
    c = pl.program_id(1)
    wdst_ref[...] = wsrc_ref[...].astype(BF16)

    @pl.when(c == 0)
    def _():
        hst_ref[...] = h0_ref[0].T

    row_i = lax.broadcasted_iota(jnp.int32, (lc, lc), 0)
    col_i = lax.broadcasted_iota(jnp.int32, (lc, lc), 1)
    causal = row_i >= col_i
    lane_lo = lax.broadcasted_iota(jnp.int32, (lc, LANES), 1) < HEAD_DIM

    dt = dt_ref[...]
    la = dt * (-jnp.exp(alr_ref[...]))
    a_cum = _prefix_sum_rows(la, lc)
    a_last = a_cum[lc - 1:lc, :]
    factors = jnp.concatenate([jnp.exp(a_cum), jnp.exp(a_last - a_cum) * dt], axis=0).astype(BF16)
    a2 = a_cum * LOG2E
    r_t = (a2 - jnp.log2(dt)).T

    for g in range(GROUPS):
        bg = xc_ref[:, D_SSM + g * D_STATE:D_SSM + (g + 1) * D_STATE]
        cg = xc_ref[:, D_SSM + GN + g * D_STATE:D_SSM + GN + (g + 1) * D_STATE].astype(BF16)
        cols = slice(g * GROUP_W, (g + 1) * GROUP_W)
        h_prev = hst_ref[:, cols]
        cb = lax.dot_general(cg, bg.astype(BF16), NT_DIMS, preferred_element_type=F32)
        y_off = jnp.dot(cg, h_prev.astype(BF16), preferred_element_type=F32)
        xs_g = xc_ref[:, cols]
        both = jnp.dot(factors, expand_ref[:, cols], preferred_element_type=F32)
        e_exp, w_exp = both[:lc], both[lc:]
        for pr in range(HPG // 2):
            mats = []
            for hh in (g * HPG + 2 * pr, g * HPG + 2 * pr + 1):
                seg = a2[:, hh:hh + 1] - r_t[hh:hh + 1, :]
                dec = jnp.exp2(jnp.where(causal, seg, -jnp.inf))
                mats.append((cb * dec).astype(BF16))
            xp = xs_g[:, pr * LANES:(pr + 1) * LANES].astype(BF16)
            rhs = jnp.concatenate([jnp.where(lane_lo, xp, 0), jnp.where(lane_lo, 0, xp)], axis=0)
            y_diag = jnp.dot(jnp.concatenate(mats, axis=1), rhs, preferred_element_type=F32)
            pc = slice(g * GROUP_W + pr * LANES, g * GROUP_W + (pr + 1) * LANES)
            ysc_ref[:, pc] = y_diag + (y_off * e_exp)[:, pr * LANES:(pr + 1) * LANES]
        xw = (xs_g * w_exp).astype(BF16)
        s_t = jnp.dot(bg.T.astype(BF16), xw, preferred_element_type=F32)
        hst_ref[:, cols] = h_prev * e_exp[lc - 1:lc, :] + s_t

    y = ysc_ref[...] + xc_ref[:, 0:D_SSM] * dexp_ref[...]
    _gated_norm_store(y, zg_ref, gnw_ref, y_ref)

    @pl.when(c == nc - 1)
    def _():
        hout_ref[0] = hst_ref[...].T


def _ssd_short_body(zg_ref, xc_ref, dt_ref, h0_ref, alr_ref, expand_ref, exps_ref, dexp_ref, gnw_ref,
                    y_ref, hout_ref, ysc_ref, *, lc, spb):
    rows = spb * lc
    hl = HEADS * lc
    dt = dt_ref[...]
    a_cum = _prefix_sum_rows(dt * (-jnp.exp(alr_ref[...])), lc)
    a_last = a_cum.reshape(spb, lc, LANES)[:, lc - 1:lc, :]
    a_last_b = jnp.broadcast_to(a_last, (spb, lc, LANES)).reshape(rows, LANES)
    expand = expand_ref[...]
    e_exp = _dot_split(jnp.exp(a_cum), expand, 2)
    w_exp = _dot_split(jnp.exp(a_last_b - a_cum) * dt, expand, 2)
    cd_t = jnp.exp(a_last.reshape(spb, LANES)).T

    z2 = _dot_split(jnp.concatenate([a_cum, dt], axis=0), exps_ref[...], 3)
    zc, zd = z2[:rows], z2[rows:]
    trow = lax.broadcasted_iota(jnp.int32, (rows, hl), 0) % lc
    tsrc = lax.broadcasted_iota(jnp.int32, (rows, hl), 1) % lc
    diag = trow == tsrc

    def per_source(z):
        d = jnp.where(diag, z, 0.0).reshape(spb, lc, hl).sum(axis=1, keepdims=True)
        return jnp.broadcast_to(d, (spb, lc, hl)).reshape(rows, hl)

    dec = jnp.exp(jnp.where(trow >= tsrc, zc - per_source(zc), -jnp.inf)) * per_source(zd)

    pair_w = HPG * lc
    blockdiag = (lax.broadcasted_iota(jnp.int32, (pair_w, GROUP_W), 0) // lc
                 == lax.broadcasted_iota(jnp.int32, (pair_w, GROUP_W), 1) // HEAD_DIM)

    for q in range(spb):
        tok = slice(q * lc, (q + 1) * lc)
        for g in range(GROUPS):
            bg = xc_ref[tok, D_SSM + g * D_STATE:D_SSM + (g + 1) * D_STATE]
            cg = xc_ref[tok, D_SSM + GN + g * D_STATE:D_SSM + GN + (g + 1) * D_STATE].astype(BF16)
            cols = slice(g * GROUP_W, (g + 1) * GROUP_W)
            h_prev = h0_ref[q, cols, :]
            xs_g = xc_ref[tok, cols]
            b_tiled = jnp.concatenate([bg] * HPG, axis=0).astype(BF16)
            cb = lax.dot_general(cg, b_tiled, NT_DIMS, preferred_element_type=F32)
            m = (cb * dec[tok, g * pair_w:(g + 1) * pair_w]).astype(BF16)
            x_diag = jnp.where(blockdiag, jnp.concatenate([xs_g] * HPG, axis=0), 0.0).astype(BF16)
            y_diag = jnp.dot(m, x_diag, preferred_element_type=F32)
            y_off = lax.dot_general(cg, h_prev.astype(BF16), NT_DIMS, preferred_element_type=F32)
            ysc_ref[tok, cols] = y_diag + y_off * e_exp[tok, cols]
            xw_t = (xs_g * w_exp[tok, cols]).T.astype(BF16)
            s_g = jnp.dot(xw_t, bg.astype(BF16), preferred_element_type=F32)
            cd = jnp.concatenate(
                [jnp.broadcast_to(cd_t[g * HPG + jh:g * HPG + jh + 1, q:q + 1], (HEAD_DIM, D_STATE))
                 for jh in range(HPG)], axis=0)
            hout_ref[q, cols, :] = h_prev * cd + s_g

    y = ysc_ref[...] + xc_ref[:, 0:D_SSM] * dexp_ref[...]
    _gated_norm_store(y, zg_ref, gnw_ref, y_ref)


def _ssd_chunked(zg, xc, dt, h0, alr, expand, dexp, gnw, w_f32, *, nb, nc):
    t = zg.shape[0]
    lc = SSD_CHUNK
    wrows = w_f32.shape[0] // (nb * nc)
    assert wrows * nb * nc == w_f32.shape[0] and wrows % BF16_SUBLANES == 0
    tok = lambda rows, width: pl.BlockSpec((rows, width), lambda b, c: (b * nc + c, 0))
    par = lambda r, width: pl.BlockSpec((r, width), lambda b, c: (0, 0))
    h_spec = pl.BlockSpec((1, D_SSM, D_STATE), lambda b, c: (b, 0, 0))
    return pl.pallas_call(
        functools.partial(_ssd_chunk_body, lc=lc, nc=nc),
        grid=(nb, nc),
        in_specs=[tok(lc, D_SSM), tok(lc, CONV_DIM), tok(lc, LANES), h_spec, par(1, LANES),
                  par(LANES, D_SSM), par(1, D_SSM), par(1, D_SSM), tok(wrows, w_f32.shape[1])],
        out_specs=[tok(lc, D_SSM), h_spec, tok(wrows, w_f32.shape[1])],
        out_shape=[jax.ShapeDtypeStruct((t, D_SSM), BF16),
                   jax.ShapeDtypeStruct((nb, D_SSM, D_STATE), F32),
                   jax.ShapeDtypeStruct(w_f32.shape, BF16)],
        scratch_shapes=[pltpu.VMEM((D_STATE, D_SSM), F32), pltpu.VMEM((lc, D_SSM), F32)],
        compiler_params=pltpu.CompilerParams(
            dimension_semantics=("arbitrary", "arbitrary"), vmem_limit_bytes=VMEM_LIMIT),
        name="ssd_scan",
    )(zg, xc, dt, h0, alr, expand, dexp, gnw, w_f32)


def _ssd_short(zg, xc, dt, h0, alr, expand, exps, dexp, gnw, *, nb, lc):
    t = zg.shape[0]
    spb = SHORT_SEQS_PER_STEP
    rows = spb * lc
    tok = lambda width: pl.BlockSpec((rows, width), lambda b: (b, 0))
    par = lambda r, width: pl.BlockSpec((r, width), lambda b: (0, 0))
    h_spec = pl.BlockSpec((spb, D_SSM, D_STATE), lambda b: (b, 0, 0))
    return pl.pallas_call(
        functools.partial(_ssd_short_body, lc=lc, spb=spb),
        grid=(nb // spb,),
        in_specs=[tok(D_SSM), tok(CONV_DIM), tok(LANES), h_spec, par(1, LANES), par(LANES, D_SSM),
                  par(LANES, HEADS * lc), par(1, D_SSM), par(1, D_SSM)],
        out_specs=[tok(D_SSM), h_spec],
        out_shape=[jax.ShapeDtypeStruct((t, D_SSM), BF16),
                   jax.ShapeDtypeStruct((nb, D_SSM, D_STATE), F32)],
        scratch_shapes=[pltpu.VMEM((rows, D_SSM), F32)],
        compiler_params=pltpu.CompilerParams(
            dimension_semantics=("arbitrary",), vmem_limit_bytes=VMEM_LIMIT),
        name="ssd_scan_short",
    )(zg, xc, dt, h0, alr, expand, exps, dexp, gnw)


def _out_body(ys_ref, yc_ref, x_ref, w1_ref, w2_ref, fnw_ref, o_ref):
    acc = jnp.dot(ys_ref[...], w1_ref[...], preferred_element_type=F32)
    acc = acc + jnp.dot(yc_ref[...], w2_ref[...], preferred_element_type=F32)
    r = x_ref[...] + acc
    var = jnp.mean(r * r, axis=-1, keepdims=True)
    o_ref[...] = r * lax.rsqrt(var + EPS) * fnw_ref[...]


def _out(ys, yc, x, w_o, fnw, *, tm):
    t = x.shape[0]
    row = lambda width: pl.BlockSpec((tm, width), lambda m: (m, 0))
    return pl.pallas_call(
        _out_body,
        grid=(t // tm,),
        in_specs=[row(D_SSM), row(D_SHORT), row(D_MODEL),
                  pl.BlockSpec((D_SSM, D_MODEL), lambda m: (0, 0), pipeline_mode=pl.Buffered(1)),
                  pl.BlockSpec((D_SHORT, D_MODEL), lambda m: (1, 0), pipeline_mode=pl.Buffered(1)),
                  pl.BlockSpec((1, D_MODEL), lambda m: (0, 0))],
        out_specs=row(D_MODEL),
        out_shape=jax.ShapeDtypeStruct((t, D_MODEL), F32),
        compiler_params=pltpu.CompilerParams(
            dimension_semantics=("arbitrary",), vmem_limit_bytes=VMEM_LIMIT),
        name="out_proj",
    )(ys, yc, x, w_o, w_o, fnw)


def _projections(x2d, conv_state, short_state, p, *, seqlen):
    tm = ROW_TILE
    nseq = max(tm // seqlen, 1)
    tps = max(seqlen // tm, 1)
    assert x2d.shape[0] % tm == 0 and (seqlen % tm == 0 or tm % seqlen == 0)
    h, zg, dt = _proj_z(x2d, p["nw"], p["w_t"], p["dtb"], tm=tm)
    xc, conv_new = _proj_xc(h, p["w_t"], p["conv_w"], p["conv_b"], conv_state,
                            tm=tm, nseq=nseq, tps=tps)
    tm_s = min(SHORT_ROW_TILE, x2d.shape[0])
    assert seqlen % tm_s == 0 or tm_s % seqlen == 0
    y_c, short_new = _short(h, p["w_t"], p["conv_short_w"], short_state,
                            tm=tm_s, tn=SHORT_TN, nseq=max(tm_s // seqlen, 1), tps=max(seqlen // tm_s, 1))
    return zg, xc, dt, y_c, conv_new, short_new


def kernel(x_prompt, x_sample, state_ssm, state_conv_ssd, state_conv_short, norm_w, w_in, conv_ssd_w,
           conv_ssd_b, dt_bias, a_log, d_skip, ssd_norm_w, conv_short_w, w_out, final_norm_w):
    depth = norm_w.shape[0]
    assert depth == 1, "the output projection fuses the final rmsnorm, valid for a single layer"
    bp, lp, _ = x_prompt.shape
    bs, ls, _ = x_sample.shape
    assert lp % SSD_CHUNK == 0 and ls < SSD_CHUNK
    hp = x_prompt.reshape(bp * lp, D_MODEL)
    hs = x_sample.reshape(bs * ls, D_MODEL)
    pad_h = LANES - HEADS
    expand = (jnp.arange(LANES)[:, None] == jnp.arange(D_SSM)[None, :] // HEAD_DIM).astype(BF16)
    expand_pairs = (jnp.arange(LANES)[:, None] == jnp.arange(HEADS * ls)[None, :] // ls).astype(BF16)
    outs = [[] for _ in range(6)]
    for layer in range(depth):
        p = dict(
            nw=norm_w[layer][None, :],
            w_t=w_in[layer].T,
            conv_short_w=conv_short_w[layer],
            conv_w=conv_ssd_w[layer],
            conv_b=conv_ssd_b[layer][None, :],
            dtb=jnp.pad(dt_bias[layer], (0, pad_h))[None, :],
        )
        alr = jnp.pad(a_log[layer], (0, pad_h))[None, :]
        dexp = jnp.repeat(d_skip[layer], HEAD_DIM)[None, :]
        gnw = ssd_norm_w[layer][None, :]

        zg, xc, dt, yc, a2, a3 = _projections(
            hp, jnp.zeros((bp, CONV_K - 1, CONV_DIM), F32), jnp.zeros((bp, SHORT_K - 1, D_SHORT), F32), p,
            seqlen=lp)
        ys, a1, w_o = _ssd_chunked(zg, xc, dt, jnp.zeros((bp, D_SSM, D_STATE), F32), alr, expand, dexp,
                                   gnw, w_out[layer], nb=bp, nc=lp // SSD_CHUNK)
        zg2, xc2, dt2, yc2, s2, s3 = _projections(
            hs, state_conv_ssd[layer], state_conv_short[layer], p, seqlen=ls)
        ys2, s1 = _ssd_short(zg2, xc2, dt2, state_ssm[layer].reshape(bs, D_SSM, D_STATE), alr, expand,
                             expand_pairs, dexp, gnw, nb=bs, lc=ls)
        fw = final_norm_w[None, :]
        hp = _out(ys, yc, hp, w_o, fw, tm=OUT_ROW_TILE)
        hs = _out(ys2, yc2, hs, w_o, fw, tm=OUT_ROW_TILE)
        for lst, val in zip(outs, (a1.reshape(bp, HEADS, HEAD_DIM, D_STATE), a2, a3,
                                   s1.reshape(bs, HEADS, HEAD_DIM, D_STATE), s2, s3)):
            lst.append(val)
    return (hp.reshape(bp, lp, D_MODEL), hs.reshape(bs, ls, D_MODEL),
            *(jnp.stack(v) for v in outs))
```

```python
import functools

import jax
import jax.numpy as jnp
from jax import lax
from jax.experimental import pallas as pl
from jax.experimental.pallas import tpu as pltpu

F32 = jnp.float32
BF16 = jnp.bfloat16

D_MODEL = 2048
D_SSM = 2048
D_SHORT = 2048
HEADS = 32
HEAD_DIM = 64
D_STATE = 128
GROUPS = 4
HPG = HEADS // GROUPS
GROUP_W = HPG * HEAD_DIM
GN = GROUPS * D_STATE
CONV_DIM = D_SSM + 2 * GN
CONV_K = 4
SHORT_K = 3
SSD_CHUNK = 128
EPS = 1e-6
LANES = 128
SUBLANES = 8
BF16_SUBLANES = 16
VMEM_LIMIT = 56 * 1024 * 1024

ROW_TILE = 1024
TN = 1024
SHORT_ROW_TILE = 2048
SHORT_TN = 256
OUT_ROW_TILE = 512
NORM_ROWS = 32
NORM_UNROLL = 8
Z_TILES = D_SSM // TN
XC_TILES = CONV_DIM // TN
MAIN_VALID = D_SSM + CONV_DIM + HEADS
DT_BLOCK = (D_SSM + CONV_DIM) // LANES

NT_DIMS = (((1,), (1,)), ((), ()))
LOG2E = 1.4426950408889634
SHORT_SEQS_PER_STEP = 8


def _silu(x):
    half = 0.5 * x
    return half + half * jnp.tanh(half)


def _softplus(x):
    return jnp.maximum(x, 0.0) + jnp.log1p(jnp.exp(-jnp.abs(x)))


def _proj_z_body(x_ref, nw_ref, w_ref, wdt_ref, dtb_ref, h_ref, zg_ref, dt_ref, *wbf, tm):
    j = pl.program_id(1)
    if wbf:
        @pl.when(pl.program_id(0) == 0)
        def _():
            wbf[0][j] = w_ref[...].astype(BF16)

    @pl.when(j == 0)
    def _():
        nw = nw_ref[...]

        def body(i, carry):
            r = pl.ds(pl.multiple_of(i * NORM_ROWS, NORM_ROWS), NORM_ROWS)
            xf = x_ref[r, :]
            var = jnp.mean(xf * xf, axis=-1, keepdims=True)
            h_ref[r, :] = (xf * lax.rsqrt(var + EPS) * nw).astype(BF16)
            return carry

        lax.fori_loop(0, tm // NORM_ROWS, body, 0, unroll=NORM_UNROLL)
        dt_raw = lax.dot_general(h_ref[...], wdt_ref[...].astype(BF16), NT_DIMS,
                                 preferred_element_type=F32)
        lane = lax.broadcasted_iota(jnp.int32, dt_raw.shape, 1)
        dt_ref[...] = jnp.where(lane < HEADS, _softplus(dt_raw + dtb_ref[...]), 0.0)

    w = wbf[0][j] if wbf else w_ref[...].astype(BF16)
    acc = lax.dot_general(h_ref[...], w, NT_DIMS, preferred_element_type=F32)
    zg_ref[...] = _silu(acc).astype(BF16)


def _proj_z(x, nw, w_t, dtb, *, tm):
    t = x.shape[0]
    if t // tm > 1:
        w_spec = pl.BlockSpec((TN, D_MODEL), lambda m, j: (jnp.where(m == 0, j, Z_TILES - 1), 0),
                              pipeline_mode=pl.Buffered(1))
        scratch = [pltpu.VMEM((Z_TILES, TN, D_MODEL), BF16)]
    else:
        w_spec = pl.BlockSpec((TN, D_MODEL), lambda m, j: (j, 0))
        scratch = []
    return pl.pallas_call(
        functools.partial(_proj_z_body, tm=tm),
        grid=(t // tm, Z_TILES),
        in_specs=[
            pl.BlockSpec((tm, D_MODEL), lambda m, j: (m, 0)),
            pl.BlockSpec((1, D_MODEL), lambda m, j: (0, 0)),
            w_spec,
            pl.BlockSpec((LANES, D_MODEL), lambda m, j: (DT_BLOCK, 0)),
            pl.BlockSpec((1, LANES), lambda m, j: (0, 0)),
        ],
        out_specs=[pl.BlockSpec((tm, D_MODEL), lambda m, j: (m, 0)),
                   pl.BlockSpec((tm, TN), lambda m, j: (m, j)),
                   pl.BlockSpec((tm, LANES), lambda m, j: (m, 0))],
        out_shape=[jax.ShapeDtypeStruct((t, D_MODEL), BF16),
                   jax.ShapeDtypeStruct((t, D_SSM), BF16),
                   jax.ShapeDtypeStruct((t, LANES), F32)],
        scratch_shapes=scratch,
        compiler_params=pltpu.CompilerParams(
            dimension_semantics=("arbitrary", "arbitrary"), vmem_limit_bytes=VMEM_LIMIT),
        name="in_proj_z",
    )(x, nw, w_t, w_t, dtb)


def _load_history(pad_ref, st_ref, carry, j, *, first, hist, tps):
    def from_state():
        pad_ref[:, 0:SUBLANES, :] = jnp.zeros((pad_ref.shape[0], SUBLANES, pad_ref.shape[2]), F32)
        pad_ref[:, SUBLANES - hist:SUBLANES, :] = st_ref[...]

    if tps > 1:
        (carry_ref,) = carry
        pl.when(first)(from_state)

        @pl.when(jnp.logical_not(first))
        def _():
            pad_ref[:, 0:SUBLANES, :] = carry_ref[j]
    else:
        from_state()


def _causal_conv(pad_ref, cw_ref, taps):
    xe = pad_ref[...]
    if taps == 4:
        s1 = pltpu.roll(xe, 1, axis=1)
        near = xe * cw_ref[3:4, :] + s1 * cw_ref[2:3, :]
        far = xe * cw_ref[1:2, :] + s1 * cw_ref[0:1, :]
        return near[:, SUBLANES:, :] + pltpu.roll(far, 2, axis=1)[:, SUBLANES:, :]
    conv = xe[:, SUBLANES:, :] * cw_ref[taps - 1:taps, :]
    for s in range(1, taps):
        shifted = pltpu.roll(xe, s, axis=1)[:, SUBLANES:, :]
        conv = conv + shifted * cw_ref[taps - 1 - s:taps - s, :]
    return conv


def _proj_xc_body(h_ref, w_ref, cw_ref, cb_ref, st_ref, xc_ref, tails_ref, pad_ref, *carry,
                  tm, nseq, tps):
    j = pl.program_id(1)
    lt = tm // nseq
    _load_history(pad_ref, st_ref, carry, j, first=(pl.program_id(0) % tps) == 0, hist=CONV_K - 1,
                  tps=tps)
    acc = lax.dot_general(h_ref[...], w_ref[...].astype(BF16), NT_DIMS, preferred_element_type=F32)
    pad_ref[:, SUBLANES:SUBLANES + lt, :] = acc.reshape(nseq, lt, TN)
    conv = _causal_conv(pad_ref, cw_ref, CONV_K)
    xc_ref[...] = _silu(conv + cb_ref[...]).reshape(tm, TN)
    tails_ref[...] = pad_ref[:, lt + SUBLANES - (CONV_K - 1):lt + SUBLANES, :]
    if tps > 1:
        carry[0][j] = pad_ref[:, lt:lt + SUBLANES, :]


def _proj_xc(h, w_t, conv_w, conv_b, conv_state, *, tm, nseq, tps):
    t = h.shape[0]
    lt = tm // nseq
    scratch = [pltpu.VMEM((nseq, SUBLANES + lt, TN), F32)]
    if tps > 1:
        scratch.append(pltpu.VMEM((XC_TILES, nseq, SUBLANES, TN), F32))
    xc, tails = pl.pallas_call(
        functools.partial(_proj_xc_body, tm=tm, nseq=nseq, tps=tps),
        grid=(t // tm, XC_TILES),
        in_specs=[
            pl.BlockSpec((tm, D_MODEL), lambda m, j: (m, 0)),
            pl.BlockSpec((TN, D_MODEL), lambda m, j: (Z_TILES + j, 0)),
            pl.BlockSpec((CONV_K, TN), lambda m, j: (0, j)),
            pl.BlockSpec((1, TN), lambda m, j: (0, j)),
            pl.BlockSpec((nseq, CONV_K - 1, TN), lambda m, j: (m // tps, 0, j)),
        ],
        out_specs=[pl.BlockSpec((tm, TN), lambda m, j: (m, j)),
                   pl.BlockSpec((nseq, CONV_K - 1, TN), lambda m, j: (m, 0, j))],
        out_shape=[jax.ShapeDtypeStruct((t, CONV_DIM), F32),
                   jax.ShapeDtypeStruct(((t // tm) * nseq, CONV_K - 1, CONV_DIM), F32)],
        scratch_shapes=scratch,
        compiler_params=pltpu.CompilerParams(
            dimension_semantics=("arbitrary", "arbitrary"), vmem_limit_bytes=VMEM_LIMIT),
        name="in_proj_xc",
    )(h, w_t, conv_w, conv_b, conv_state)
    tails = tails.reshape(t // (tm * tps), tps, nseq, CONV_K - 1, CONV_DIM)[:, -1]
    return xc, tails.reshape(conv_state.shape)


def _short_body(h_ref, *refs, tm, tn, nseq, tps, shift):
    w_refs, (cw_ref, st_ref, y_ref, ns_ref, wbf_ref, pad_ref), carry = refs[:8], refs[8:14], refs[14:]
    m = pl.program_id(1)
    lt = tm // nseq

    @pl.when(m == 0)
    def _():
        for k in range(4):
            a_ref, b_ref = w_refs[2 * k], w_refs[2 * k + 1]
            wbf_ref[k] = jnp.concatenate([a_ref[shift:, :], b_ref[...]], axis=0).astype(BF16)

    _load_history(pad_ref, st_ref, carry, 0, first=(m % tps) == 0, hist=SHORT_K - 1, tps=tps)
    h = h_ref[...]
    band = lambda k: lax.dot_general(h, wbf_ref[k], NT_DIMS, preferred_element_type=F32)
    c = band(2)
    v = band(3)
    pad_ref[:, SUBLANES:SUBLANES + lt, :] = (c * v).reshape(nseq, lt, tn)
    conv = _causal_conv(pad_ref, cw_ref, SHORT_K)
    b = band(1)
    z = band(0)
    y = b * conv.reshape(tm, tn) * _silu(z)
    y_ref[...] = y.astype(y_ref.dtype)
    ns_ref[...] = pad_ref[:, lt + SUBLANES - (SHORT_K - 1):lt + SUBLANES, :]
    if tps > 1:
        carry[0][0] = pad_ref[:, lt:lt + SUBLANES, :]


def _short(h, w_t, conv_w, state, *, tm, tn, nseq, tps):
    t = h.shape[0]
    nj = D_SHORT // tn
    lt = tm // nseq
    base = (MAIN_VALID // tn) * tn
    shift = MAIN_VALID - base
    assert shift % SUBLANES == 0 and tn % shift == 0 and base % shift == 0
    scratch = [pltpu.VMEM((4, tn, D_MODEL), BF16), pltpu.VMEM((nseq, SUBLANES + lt, tn), F32)]
    if tps > 1:
        scratch.append(pltpu.VMEM((1, nseq, SUBLANES, tn), F32))
    w_specs = []
    for k in range(4):
        r0 = base + k * D_SHORT
        w_specs.append(pl.BlockSpec((tn, D_MODEL), lambda j, m, r0=r0: (r0 // tn + j, 0)))
        w_specs.append(pl.BlockSpec((shift, D_MODEL), lambda j, m, r0=r0: ((r0 + tn * (j + 1)) // shift, 0)))

    y, tails = pl.pallas_call(
        functools.partial(_short_body, tm=tm, tn=tn, nseq=nseq, tps=tps, shift=shift),
        grid=(nj, t // tm),
        in_specs=[
            pl.BlockSpec((tm, D_MODEL), lambda j, m: (m, 0)),
            *w_specs,
            pl.BlockSpec((SHORT_K, tn), lambda j, m: (0, j)),
            pl.BlockSpec((nseq, SHORT_K - 1, tn), lambda j, m: (m // tps, 0, j)),
        ],
        out_specs=[pl.BlockSpec((tm, tn), lambda j, m: (m, j)),
                   pl.BlockSpec((nseq, SHORT_K - 1, tn), lambda j, m: (m, 0, j))],
        out_shape=[jax.ShapeDtypeStruct((t, D_SHORT), BF16),
                   jax.ShapeDtypeStruct(((t // tm) * nseq, SHORT_K - 1, D_SHORT), F32)],
        scratch_shapes=scratch,
        compiler_params=pltpu.CompilerParams(
            dimension_semantics=("arbitrary", "arbitrary"), vmem_limit_bytes=VMEM_LIMIT),
        name="short_conv",
    )(h, *([w_t] * 8), conv_w, state)
    tails = tails.reshape(t // (tm * tps), tps, nseq, SHORT_K - 1, D_SHORT)[:, -1]
    return y, tails.reshape(state.shape)


def _dot_split(x, w, pieces):
    out = None
    r = x
    for _ in range(pieces):
        p = r.astype(BF16)
        d = jnp.dot(p, w, preferred_element_type=F32)
        out = d if out is None else out + d
        r = r - p.astype(F32)
    return out


def _prefix_sum_rows(x, period):
    t = lax.broadcasted_iota(jnp.int32, x.shape, 0) % period
    k = 1
    while k < period:
        x = x + jnp.where(t >= k, pltpu.roll(x, k, axis=0), 0.0)
        k *= 2
    return x


def _gated_norm_store(y, zg_ref, gnw_ref, y_ref):
    y = y * zg_ref[...].astype(F32)
    for g in range(GROUPS):
        cs = slice(g * GROUP_W, (g + 1) * GROUP_W)
        blk = y[:, cs]
        ms = jnp.mean(blk * blk, axis=-1, keepdims=True)
        y_ref[:, cs] = (blk * lax.rsqrt(ms + EPS) * gnw_ref[:, cs]).astype(y_ref.dtype)


def _ssd_chunk_body(zg_ref, xc_ref, dt_ref, h0_ref, alr_ref, expand_ref, dexp_ref, gnw_ref, wsrc_ref,
                    y_ref, hout_ref, wdst_ref, hst_ref, ysc_ref, *, lc, nc):
    c = pl.program_id(1)
    wdst_ref[...] = wsrc_ref[...].astype(BF16)

    @pl.when(c == 0)
    def _():
        hst_ref[...] = h0_ref[0].T

    row_i = lax.broadcasted_iota(jnp.int32, (lc, lc), 0)
    col_i = lax.broadcasted_iota(jnp.int32, (lc, lc), 1)
    causal = row_i >= col_i
    lane_lo = lax.broadcasted_iota(jnp.int32, (lc, LANES), 1) < HEAD_DIM

    dt = dt_ref[...]
    la = dt * (-jnp.exp(alr_ref[...]))
    a_cum = _prefix_sum_rows(la, lc)
    a_last = a_cum[lc - 1:lc, :]
    factors = jnp.concatenate([jnp.exp(a_cum), jnp.exp(a_last - a_cum) * dt], axis=0).astype(BF16)
    a2 = a_cum * LOG2E
    r_t = (a2 - jnp.log2(dt)).T

    for g in range(GROUPS):
        bg = xc_ref[:, D_SSM + g * D_STATE:D_SSM + (g + 1) * D_STATE]
        cg = xc_ref[:, D_SSM + GN + g * D_STATE:D_SSM + GN + (g + 1) * D_STATE].astype(BF16)
        cols = slice(g * GROUP_W, (g + 1) * GROUP_W)
        h_prev = hst_ref[:, cols]
        cb = lax.dot_general(cg, bg.astype(BF16), NT_DIMS, preferred_element_type=F32)
        y_off = jnp.dot(cg, h_prev.astype(BF16), preferred_element_type=F32)
        xs_g = xc_ref[:, cols]
        both = jnp.dot(factors, expand_ref[:, cols], preferred_element_type=F32)
        e_exp, w_exp = both[:lc], both[lc:]
        for pr in range(HPG // 2):
            mats = []
            for hh in (g * HPG + 2 * pr, g * HPG + 2 * pr + 1):
                seg = a2[:, hh:hh + 1] - r_t[hh:hh + 1, :]
                dec = jnp.exp2(jnp.where(causal, seg, -jnp.inf))
                mats.append((cb * dec).astype(BF16))
            xp = xs_g[:, pr * LANES:(pr + 1) * LANES].astype(BF16)
            rhs = jnp.concatenate([jnp.where(lane_lo, xp, 0), jnp.where(lane_lo, 0, xp)], axis=0)
            y_diag = jnp.dot(jnp.concatenate(mats, axis=1), rhs, preferred_element_type=F32)
            pc = slice(g * GROUP_W + pr * LANES, g * GROUP_W + (pr + 1) * LANES)
            ysc_ref[:, pc] = y_diag + (y_off * e_exp)[:, pr * LANES:(pr + 1) * LANES]
        xw = (xs_g * w_exp).astype(BF16)
        s_t = jnp.dot(bg.T.astype(BF16), xw, preferred_element_type=F32)
        hst_ref[:, cols] = h_prev * e_exp[lc - 1:lc, :] + s_t

    y = ysc_ref[...] + xc_ref[:, 0:D_SSM] * dexp_ref[...]
    _gated_norm_store(y, zg_ref, gnw_ref, y_ref)

    @pl.when(c == nc - 1)
    def _():
        hout_ref[0] = hst_ref[...].T


def _ssd_short_body(zg_ref, xc_ref, dt_ref, h0_ref, alr_ref, expand_ref, exps_ref, dexp_ref, gnw_ref,
                    y_ref, hout_ref, ysc_ref, *, lc, spb):
    rows = spb * lc
    hl = HEADS * lc
    dt = dt_ref[...]
    a_cum = _prefix_sum_rows(dt * (-jnp.exp(alr_ref[...])), lc)
    a_last = a_cum.reshape(spb, lc, LANES)[:, lc - 1:lc, :]
    a_last_b = jnp.broadcast_to(a_last, (spb, lc, LANES)).reshape(rows, LANES)
    expand = expand_ref[...]
    e_exp = _dot_split(jnp.exp(a_cum), expand, 2)
    w_exp = _dot_split(jnp.exp(a_last_b - a_cum) * dt, expand, 2)
    cd_t = jnp.exp(a_last.reshape(spb, LANES)).T

    z2 = _dot_split(jnp.concatenate([a_cum, dt], axis=0), exps_ref[...], 3)
    zc, zd = z2[:rows], z2[rows:]
    trow = lax.broadcasted_iota(jnp.int32, (rows, hl), 0) % lc
    tsrc = lax.broadcasted_iota(jnp.int32, (rows, hl), 1) % lc
    diag = trow == tsrc

    def per_source(z):
        d = jnp.where(diag, z, 0.0).reshape(spb, lc, hl).sum(axis=1, keepdims=True)
        return jnp.broadcast_to(d, (spb, lc, hl)).reshape(rows, hl)

    dec = jnp.exp(jnp.where(trow >= tsrc, zc - per_source(zc), -jnp.inf)) * per_source(zd)

    pair_w = HPG * lc
    blockdiag = (lax.broadcasted_iota(jnp.int32, (pair_w, GROUP_W), 0) // lc
                 == lax.broadcasted_iota(jnp.int32, (pair_w, GROUP_W), 1) // HEAD_DIM)

    for q in range(spb):
        tok = slice(q * lc, (q + 1) * lc)
        for g in range(GROUPS):
            bg = xc_ref[tok, D_SSM + g * D_STATE:D_SSM + (g + 1) * D_STATE]
            cg = xc_ref[tok, D_SSM + GN + g * D_STATE:D_SSM + GN + (g + 1) * D_STATE].astype(BF16)
            cols = slice(g * GROUP_W, (g + 1) * GROUP_W)
            h_prev = h0_ref[q, cols, :]
            xs_g = xc_ref[tok, cols]
            b_tiled = jnp.concatenate([bg] * HPG, axis=0).astype(BF16)
            cb = lax.dot_general(cg, b_tiled, NT_DIMS, preferred_element_type=F32)
            m = (cb * dec[tok, g * pair_w:(g + 1) * pair_w]).astype(BF16)
            x_diag = jnp.where(blockdiag, jnp.concatenate([xs_g] * HPG, axis=0), 0.0).astype(BF16)
            y_diag = jnp.dot(m, x_diag, preferred_element_type=F32)
            y_off = lax.dot_general(cg, h_prev.astype(BF16), NT_DIMS, preferred_element_type=F32)
            ysc_ref[tok, cols] = y_diag + y_off * e_exp[tok, cols]
            xw_t = (xs_g * w_exp[tok, cols]).T.astype(BF16)
            s_g = jnp.dot(xw_t, bg.astype(BF16), preferred_element_type=F32)
            cd = jnp.concatenate(
                [jnp.broadcast_to(cd_t[g * HPG + jh:g * HPG + jh + 1, q:q + 1], (HEAD_DIM, D_STATE))
                 for jh in range(HPG)], axis=0)
            hout_ref[q, cols, :] = h_prev * cd + s_g

    y = ysc_ref[...] + xc_ref[:, 0:D_SSM] * dexp_ref[...]
    _gated_norm_store(y, zg_ref, gnw_ref, y_ref)


def _ssd_chunked(zg, xc, dt, h0, alr, expand, dexp, gnw, w_f32, *, nb, nc):
    t = zg.shape[0]
    lc = SSD_CHUNK
    wrows = w_f32.shape[0] // (nb * nc)
    assert wrows * nb * nc == w_f32.shape[0] and wrows % BF16_SUBLANES == 0
    tok = lambda rows, width: pl.BlockSpec((rows, width), lambda b, c: (b * nc + c, 0))
    par = lambda r, width: pl.BlockSpec((r, width), lambda b, c: (0, 0))
    h_spec = pl.BlockSpec((1, D_SSM, D_STATE), lambda b, c: (b, 0, 0))
    return pl.pallas_call(
        functools.partial(_ssd_chunk_body, lc=lc, nc=nc),
        grid=(nb, nc),
        in_specs=[tok(lc, D_SSM), tok(lc, CONV_DIM), tok(lc, LANES), h_spec, par(1, LANES),
                  par(LANES, D_SSM), par(1, D_SSM), par(1, D_SSM), tok(wrows, w_f32.shape[1])],
        out_specs=[tok(lc, D_SSM), h_spec, tok(wrows, w_f32.shape[1])],
        out_shape=[jax.ShapeDtypeStruct((t, D_SSM), BF16),
                   jax.ShapeDtypeStruct((nb, D_SSM, D_STATE), F32),
                   jax.ShapeDtypeStruct(w_f32.shape, BF16)],
        scratch_shapes=[pltpu.VMEM((D_STATE, D_SSM), F32), pltpu.VMEM((lc, D_SSM), F32)],
        compiler_params=pltpu.CompilerParams(
            dimension_semantics=("arbitrary", "arbitrary"), vmem_limit_bytes=VMEM_LIMIT),
        name="ssd_scan",
    )(zg, xc, dt, h0, alr, expand, dexp, gnw, w_f32)


def _ssd_short(zg, xc, dt, h0, alr, expand, exps, dexp, gnw, *, nb, lc):
    t = zg.shape[0]
    spb = SHORT_SEQS_PER_STEP
    rows = spb * lc
    tok = lambda width: pl.BlockSpec((rows, width), lambda b: (b, 0))
    par = lambda r, width: pl.BlockSpec((r, width), lambda b: (0, 0))
    h_spec = pl.BlockSpec((spb, D_SSM, D_STATE), lambda b: (b, 0, 0))
    return pl.pallas_call(
        functools.partial(_ssd_short_body, lc=lc, spb=spb),
        grid=(nb // spb,),
        in_specs=[tok(D_SSM), tok(CONV_DIM), tok(LANES), h_spec, par(1, LANES), par(LANES, D_SSM),
                  par(LANES, HEADS * lc), par(1, D_SSM), par(1, D_SSM)],
        out_specs=[tok(D_SSM), h_spec],
        out_shape=[jax.ShapeDtypeStruct((t, D_SSM), BF16),
                   jax.ShapeDtypeStruct((nb, D_SSM, D_STATE), F32)],
        scratch_shapes=[pltpu.VMEM((rows, D_SSM), F32)],
        compiler_params=pltpu.CompilerParams(
            dimension_semantics=("arbitrary",), vmem_limit_bytes=VMEM_LIMIT),
        name="ssd_scan_short",
    )(zg, xc, dt, h0, alr, expand, exps, dexp, gnw)


def _out_body(ys_ref, yc_ref, x_ref, w1_ref, w2_ref, fnw_ref, o_ref):
    acc = jnp.dot(ys_ref[...], w1_ref[...], preferred_element_type=F32)
    acc = acc + jnp.dot(yc_ref[...], w2_ref[...], preferred_element_type=F32)
    r = x_ref[...] + acc
    var = jnp.mean(r * r, axis=-1, keepdims=True)
    o_ref[...] = r * lax.rsqrt(var + EPS) * fnw_ref[...]


def _out(ys, yc, x, w_o, fnw, *, tm):
    t = x.shape[0]
    row = lambda width: pl.BlockSpec((tm, width), lambda m: (m, 0))
    return pl.pallas_call(
        _out_body,
        grid=(t // tm,),
        in_specs=[row(D_SSM), row(D_SHORT), row(D_MODEL),
                  pl.BlockSpec((D_SSM, D_MODEL), lambda m: (0, 0), pipeline_mode=pl.Buffered(1)),
                  pl.BlockSpec((D_SHORT, D_MODEL), lambda m: (1, 0), pipeline_mode=pl.Buffered(1)),
                  pl.BlockSpec((1, D_MODEL), lambda m: (0, 0))],
        out_specs=row(D_MODEL),
        out_shape=jax.ShapeDtypeStruct((t, D_MODEL), F32),
        compiler_params=pltpu.CompilerParams(
            dimension_semantics=("arbitrary",), vmem_limit_bytes=VMEM_LIMIT),
        name="out_proj",
    )(ys, yc, x, w_o, w_o, fnw)


def _projections(x2d, conv_state, short_state, p, *, seqlen):
    tm = ROW_TILE
    nseq = max(tm // seqlen, 1)
    tps = max(seqlen // tm, 1)
    assert x2d.shape[0] % tm == 0 and (seqlen % tm == 0 or tm % seqlen == 0)
    h, zg, dt = _proj_z(x2d, p["nw"], p["w_t"], p["dtb"], tm=tm)
    xc, conv_new = _proj_xc(h, p["w_t"], p["conv_w"], p["conv_b"], conv_state,
                            tm=tm, nseq=nseq, tps=tps)
    tm_s = min(SHORT_ROW_TILE, x2d.shape[0])
    assert seqlen % tm_s == 0 or tm_s % seqlen == 0
    y_c, short_new = _short(h, p["w_t"], p["conv_short_w"], short_state,
                            tm=tm_s, tn=SHORT_TN, nseq=max(tm_s // seqlen, 1), tps=max(seqlen // tm_s, 1))
    return zg, xc, dt, y_c, conv_new, short_new


def kernel(x_prompt, x_sample, state_ssm, state_conv_ssd, state_conv_short, norm_w, w_in, conv_ssd_w,
           conv_ssd_b, dt_bias, a_log, d_skip, ssd_norm_w, conv_short_w, w_out, final_norm_w):
    depth = norm_w.shape[0]
    assert depth == 1, "the output projection fuses the final rmsnorm, valid for a single layer"
    bp, lp, _ = x_prompt.shape
    bs, ls, _ = x_sample.shape
    assert lp % SSD_CHUNK == 0 and ls < SSD_CHUNK
    hp = x_prompt.reshape(bp * lp, D_MODEL)
    hs = x_sample.reshape(bs * ls, D_MODEL)
    pad_h = LANES - HEADS
    expand = (jnp.arange(LANES)[:, None] == jnp.arange(D_SSM)[None, :] // HEAD_DIM).astype(BF16)
    expand_pairs = (jnp.arange(LANES)[:, None] == jnp.arange(HEADS * ls)[None, :] // ls).astype(BF16)
    outs = [[] for _ in range(6)]
    for layer in range(depth):
        p = dict(
            nw=norm_w[layer][None, :],
            w_t=w_in[layer].T,
            conv_short_w=conv_short_w[layer],
            conv_w=conv_ssd_w[layer],
            conv_b=conv_ssd_b[layer][None, :],
            dtb=jnp.pad(dt_bias[layer], (0, pad_h))[None, :],
        )
        alr = jnp.pad(a_log[layer], (0, pad_h))[None, :]
        dexp = jnp.repeat(d_skip[layer], HEAD_DIM)[None, :]
        gnw = ssd_norm_w[layer][None, :]

        zg, xc, dt, yc, a2, a3 = _projections(
            hp, jnp.zeros((bp, CONV_K - 1, CONV_DIM), F32), jnp.zeros((bp, SHORT_K - 1, D_SHORT), F32), p,
            seqlen=lp)
        ys, a1, w_o = _ssd_chunked(zg, xc, dt, jnp.zeros((bp, D_SSM, D_STATE), F32), alr, expand, dexp,
                                   gnw, w_out[layer], nb=bp, nc=lp // SSD_CHUNK)
        zg2, xc2, dt2, yc2, s2, s3 = _projections(
            hs, state_conv_ssd[layer], state_conv_short[layer], p, seqlen=ls)
        ys2, s1 = _ssd_short(zg2, xc2, dt2, state_ssm[layer].reshape(bs, D_SSM, D_STATE), alr, expand,
                             expand_pairs, dexp, gnw, nb=bs, lc=ls)
        fw = final_norm_w[None, :]
        hp = _out(ys, yc, hp, w_o, fw, tm=OUT_ROW_TILE)
        hs = _out(ys2, yc2, hs, w_o, fw, tm=OUT_ROW_TILE)
        for lst, val in zip(outs, (a1.reshape(bp, HEADS, HEAD_DIM, D_STATE), a2, a3,
                                   s1.reshape(bs, HEADS, HEAD_DIM, D_STATE), s2, s3)):
            lst.append(val)
    return (hp.reshape(bp, lp, D_MODEL), hs.reshape(bs, ls, D_MODEL),
            *(jnp.stack(v) for v in outs))
```

```python
import functools

import jax
import jax.numpy as jnp
from jax import lax
from jax.experimental import pallas as pl
from jax.experimental.pallas import tpu as pltpu

F32 = jnp.float32
BF16 = jnp.bfloat16

D_MODEL = 2048
D_SSM = 2048
D_SHORT = 2048
HEADS = 32
HEAD_DIM = 64
D_STATE = 128
GROUPS = 4
HPG = HEADS // GROUPS
GROUP_W = HPG * HEAD_DIM
GN = GROUPS * D_STATE
CONV_DIM = D_SSM + 2 * GN
CONV_K = 4
SHORT_K = 3
SSD_CHUNK = 128
EPS = 1e-6
LANES = 128
SUBLANES = 8
BF16_SUBLANES = 16
VMEM_LIMIT = 56 * 1024 * 1024

ROW_TILE = 1024
TN = 1024
SHORT_ROW_TILE = 2048
SHORT_TN = 256
OUT_ROW_TILE = 512
NORM_ROWS = 32
NORM_UNROLL = 8
Z_TILES = D_SSM // TN
XC_TILES = CONV_DIM // TN
MAIN_VALID = D_SSM + CONV_DIM + HEADS
DT_BLOCK = (D_SSM + CONV_DIM) // LANES

NT_DIMS = (((1,), (1,)), ((), ()))
LOG2E = 1.4426950408889634
SHORT_SEQS_PER_STEP = 8


def _silu(x):
    half = 0.5 * x
    return half + half * jnp.tanh(half)


def _softplus(x):
    return jnp.maximum(x, 0.0) + jnp.log1p(jnp.exp(-jnp.abs(x)))


def _proj_z_body(x_ref, nw_ref, w_ref, wdt_ref, dtb_ref, h_ref, zg_ref, dt_ref, *wbf, tm):
    j = pl.program_id(1)
    if wbf:
        @pl.when(pl.program_id(0) == 0)
        def _():
            wbf[0][j] = w_ref[...].astype(BF16)

    @pl.when(j == 0)
    def _():
        nw = nw_ref[...]

        def body(i, carry):
            r = pl.ds(pl.multiple_of(i * NORM_ROWS, NORM_ROWS), NORM_ROWS)
            xf = x_ref[r, :]
            var = jnp.mean(xf * xf, axis=-1, keepdims=True)
            h_ref[r, :] = (xf * lax.rsqrt(var + EPS) * nw).astype(BF16)
            return carry

        lax.fori_loop(0, tm // NORM_ROWS, body, 0, unroll=NORM_UNROLL)
        dt_raw = lax.dot_general(h_ref[...], wdt_ref[...].astype(BF16), NT_DIMS,
                                 preferred_element_type=F32)
        lane = lax.broadcasted_iota(jnp.int32, dt_raw.shape, 1)
        dt_ref[...] = jnp.where(lane < HEADS, _softplus(dt_raw + dtb_ref[...]), 0.0)

    w = wbf[0][j] if wbf else w_ref[...].astype(BF16)
    acc = lax.dot_general(h_ref[...], w, NT_DIMS, preferred_element_type=F32)
    zg_ref[...] = _silu(acc).astype(BF16)


def _proj_z(x, nw, w_t, dtb, *, tm):
    t = x.shape[0]
    if t // tm > 1:
        w_spec = pl.BlockSpec((TN, D_MODEL), lambda m, j: (jnp.where(m == 0, j, Z_TILES - 1), 0),
                              pipeline_mode=pl.Buffered(1))
        scratch = [pltpu.VMEM((Z_TILES, TN, D_MODEL), BF16)]
    else:
        w_spec = pl.BlockSpec((TN, D_MODEL), lambda m, j: (j, 0))
        scratch = []
    return pl.pallas_call(
        functools.partial(_proj_z_body, tm=tm),
        grid=(t // tm, Z_TILES),
        in_specs=[
            pl.BlockSpec((tm, D_MODEL), lambda m, j: (m, 0)),
            pl.BlockSpec((1, D_MODEL), lambda m, j: (0, 0)),
            w_spec,
            pl.BlockSpec((LANES, D_MODEL), lambda m, j: (DT_BLOCK, 0)),
            pl.BlockSpec((1, LANES), lambda m, j: (0, 0)),
        ],
        out_specs=[pl.BlockSpec((tm, D_MODEL), lambda m, j: (m, 0)),
                   pl.BlockSpec((tm, TN), lambda m, j: (m, j)),
                   pl.BlockSpec((tm, LANES), lambda m, j: (m, 0))],
        out_shape=[jax.ShapeDtypeStruct((t, D_MODEL), BF16),
                   jax.ShapeDtypeStruct((t, D_SSM), BF16),
                   jax.ShapeDtypeStruct((t, LANES), F32)],
        scratch_shapes=scratch,
        compiler_params=pltpu.CompilerParams(
            dimension_semantics=("arbitrary", "arbitrary"), vmem_limit_bytes=VMEM_LIMIT),
        name="in_proj_z",
    )(x, nw, w_t, w_t, dtb)


def _load_history(pad_ref, st_ref, carry, j, *, first, hist, tps):
    def from_state():
        pad_ref[:, 0:SUBLANES, :] = jnp.zeros((pad_ref.shape[0], SUBLANES, pad_ref.shape[2]), F32)
        pad_ref[:, SUBLANES - hist:SUBLANES, :] = st_ref[...]

    if tps > 1:
        (carry_ref,) = carry
        pl.when(first)(from_state)

        @pl.when(jnp.logical_not(first))
        def _():
            pad_ref[:, 0:SUBLANES, :] = carry_ref[j]
    else:
        from_state()


def _causal_conv(pad_ref, cw_ref, taps):
    xe = pad_ref[...]
    if taps == 4:
        s1 = pltpu.roll(xe, 1, axis=1)
        near = xe * cw_ref[3:4, :] + s1 * cw_ref[2:3, :]
        far = xe * cw_ref[1:2, :] + s1 * cw_ref[0:1, :]
        return near[:, SUBLANES:, :] + pltpu.roll(far, 2, axis=1)[:, SUBLANES:, :]
    conv = xe[:, SUBLANES:, :] * cw_ref[taps - 1:taps, :]
    for s in range(1, taps):
        shifted = pltpu.roll(xe, s, axis=1)[:, SUBLANES:, :]
        conv = conv + shifted * cw_ref[taps - 1 - s:taps - s, :]
    return conv


def _proj_xc_body(h_ref, w_ref, cw_ref, cb_ref, st_ref, xc_ref, tails_ref, pad_ref, *carry,
                  tm, nseq, tps):
    j = pl.program_id(1)
    lt = tm // nseq
    _load_history(pad_ref, st_ref, carry, j, first=(pl.program_id(0) % tps) == 0, hist=CONV_K - 1,
                  tps=tps)
    acc = lax.dot_general(h_ref[...], w_ref[...].astype(BF16), NT_DIMS, preferred_element_type=F32)
    pad_ref[:, SUBLANES:SUBLANES + lt, :] = acc.reshape(nseq, lt, TN)
    conv = _causal_conv(pad_ref, cw_ref, CONV_K)
    xc_ref[...] = _silu(conv + cb_ref[...]).reshape(tm, TN)
    tails_ref[...] = pad_ref[:, lt + SUBLANES - (CONV_K - 1):lt + SUBLANES, :]
    if tps > 1:
        carry[0][j] = pad_ref[:, lt:lt + SUBLANES, :]


def _proj_xc(h, w_t, conv_w, conv_b, conv_state, *, tm, nseq, tps):
    t = h.shape[0]
    lt = tm // nseq
    scratch = [pltpu.VMEM((nseq, SUBLANES + lt, TN), F32)]
    if tps > 1:
        scratch.append(pltpu.VMEM((XC_TILES, nseq, SUBLANES, TN), F32))
    xc, tails = pl.pallas_call(
        functools.partial(_proj_xc_body, tm=tm, nseq=nseq, tps=tps),
        grid=(t // tm, XC_TILES),
        in_specs=[
            pl.BlockSpec((tm, D_MODEL), lambda m, j: (m, 0)),
            pl.BlockSpec((TN, D_MODEL), lambda m, j: (Z_TILES + j, 0)),
            pl.BlockSpec((CONV_K, TN), lambda m, j: (0, j)),
            pl.BlockSpec((1, TN), lambda m, j: (0, j)),
            pl.BlockSpec((nseq, CONV_K - 1, TN), lambda m, j: (m // tps, 0, j)),
        ],
        out_specs=[pl.BlockSpec((tm, TN), lambda m, j: (m, j)),
                   pl.BlockSpec((nseq, CONV_K - 1, TN), lambda m, j: (m, 0, j))],
        out_shape=[jax.ShapeDtypeStruct((t, CONV_DIM), F32),
                   jax.ShapeDtypeStruct(((t // tm) * nseq, CONV_K - 1, CONV_DIM), F32)],
        scratch_shapes=scratch,
        compiler_params=pltpu.CompilerParams(
            dimension_semantics=("arbitrary", "arbitrary"), vmem_limit_bytes=VMEM_LIMIT),
        name="in_proj_xc",
    )(h, w_t, conv_w, conv_b, conv_state)
    tails = tails.reshape(t // (tm * tps), tps, nseq, CONV_K - 1, CONV_DIM)[:, -1]
    return xc, tails.reshape(conv_state.shape)


def _short_body(h_ref, *refs, tm, tn, nseq, tps, shift):
    w_refs, (cw_ref, st_ref, y_ref, ns_ref, wbf_ref, pad_ref), carry = refs[:8], refs[8:14], refs[14:]
    m = pl.program_id(1)
    lt = tm // nseq

    @pl.when(m == 0)
    def _():
        for k in range(4):
            a_ref, b_ref = w_refs[2 * k], w_refs[2 * k + 1]
            wbf_ref[k] = jnp.concatenate([a_ref[shift:, :], b_ref[...]], axis=0).astype(BF16)

    _load_history(pad_ref, st_ref, carry, 0, first=(m % tps) == 0, hist=SHORT_K - 1, tps=tps)
    h = h_ref[...]
    band = lambda k: lax.dot_general(h, wbf_ref[k], NT_DIMS, preferred_element_type=F32)
    c = band(2)
    v = band(3)
    pad_ref[:, SUBLANES:SUBLANES + lt, :] = (c * v).reshape(nseq, lt, tn)
    conv = _causal_conv(pad_ref, cw_ref, SHORT_K)
    b = band(1)
    z = band(0)
    y = b * conv.reshape(tm, tn) * _silu(z)
    y_ref[...] = y.astype(y_ref.dtype)
    ns_ref[...] = pad_ref[:, lt + SUBLANES - (SHORT_K - 1):lt + SUBLANES, :]
    if tps > 1:
        carry[0][0] = pad_ref[:, lt:lt + SUBLANES, :]


def _short(h, w_t, conv_w, state, *, tm, tn, nseq, tps):
    t = h.shape[0]
    nj = D_SHORT // tn
    lt = tm // nseq
    base = (MAIN_VALID // tn) * tn
    shift = MAIN_VALID - base
    assert shift % SUBLANES == 0 and tn % shift == 0 and base % shift == 0
    scratch = [pltpu.VMEM((4, tn, D_MODEL), BF16), pltpu.VMEM((nseq, SUBLANES + lt, tn), F32)]
    if tps > 1:
        scratch.append(pltpu.VMEM((1, nseq, SUBLANES, tn), F32))
    w_specs = []
    for k in range(4):
        r0 = base + k * D_SHORT
        w_specs.append(pl.BlockSpec((tn, D_MODEL), lambda j, m, r0=r0: (r0 // tn + j, 0)))
        w_specs.append(pl.BlockSpec((shift, D_MODEL), lambda j, m, r0=r0: ((r0 + tn * (j + 1)) // shift, 0)))

    y, tails = pl.pallas_call(
        functools.partial(_short_body, tm=tm, tn=tn, nseq=nseq, tps=tps, shift=shift),
        grid=(nj, t // tm),
        in_specs=[
            pl.BlockSpec((tm, D_MODEL), lambda j, m: (m, 0)),
            *w_specs,
            pl.BlockSpec((SHORT_K, tn), lambda j, m: (0, j)),
            pl.BlockSpec((nseq, SHORT_K - 1, tn), lambda j, m: (m // tps, 0, j)),
        ],
        out_specs=[pl.BlockSpec((tm, tn), lambda j, m: (m, j)),
                   pl.BlockSpec((nseq, SHORT_K - 1, tn), lambda j, m: (m, 0, j))],
        out_shape=[jax.ShapeDtypeStruct((t, D_SHORT), BF16),
                   jax.ShapeDtypeStruct(((t // tm) * nseq, SHORT_K - 1, D_SHORT), F32)],
        scratch_shapes=scratch,
        compiler_params=pltpu.CompilerParams(
            dimension_semantics=("arbitrary", "arbitrary"), vmem_limit_bytes=VMEM_LIMIT),
        name="short_conv",
    )(h, *([w_t] * 8), conv_w, state)
    tails = tails.reshape(t // (tm * tps), tps, nseq, SHORT_K - 1, D_SHORT)[:, -1]
    return y, tails.reshape(state.shape)


def _dot_split(x, w, pieces):
    out = None
    r = x
    for _ in range(pieces):
        p = r.astype(BF16)
        d = jnp.dot(p, w, preferred_element_type=F32)
        out = d if out is None else out + d
        r = r - p.astype(F32)
    return out


def _prefix_sum_rows(x, period):
    t = lax.broadcasted_iota(jnp.int32, x.shape, 0) % period
    k = 1
    while k < period:
        x = x + jnp.where(t >= k, pltpu.roll(x, k, axis=0), 0.0)
        k *= 2
    return x


def _gated_norm_store(y, zg_ref, gnw_ref, y_ref):
    y = y * zg_ref[...].astype(F32)
    for g in range(GROUPS):
        cs = slice(g * GROUP_W, (g + 1) * GROUP_W)
        blk = y[:, cs]
        ms = jnp.mean(blk * blk, axis=-1, keepdims=True)
        y_ref[:, cs] = (blk * lax.rsqrt(ms + EPS) * gnw_ref[:, cs]).astype(y_ref.dtype)


def _ssd_chunk_body(zg_ref, xc_ref, dt_ref, h0_ref, alr_ref, expand_ref, dexp_ref, gnw_ref, wsrc_ref,
                    y_ref, hout_ref, wdst_ref, hst_ref, ysc_ref, *, lc, nc):
    c = pl.program_id(1)

    @pl.when(c == 0)
    def _():
        hst_ref[...] = h0_ref[0].T
        wdst_ref[...] = wsrc_ref[...].astype(BF16)

    row_i = lax.broadcasted_iota(jnp.int32, (lc, lc), 0)
    col_i = lax.broadcasted_iota(jnp.int32, (lc, lc), 1)
    causal = row_i >= col_i
    lane_lo = lax.broadcasted_iota(jnp.int32, (lc, LANES), 1) < HEAD_DIM

    dt = dt_ref[...]
    la = dt * (-jnp.exp(alr_ref[...]))
    a_cum = _prefix_sum_rows(la, lc)
    a_last = a_cum[lc - 1:lc, :]
    factors = jnp.concatenate([jnp.exp(a_cum), jnp.exp(a_last - a_cum) * dt], axis=0).astype(BF16)
    a2 = a_cum * LOG2E
    r_t = (a2 - jnp.log2(dt)).T

    for g in range(GROUPS):
        bg = xc_ref[:, D_SSM + g * D_STATE:D_SSM + (g + 1) * D_STATE]
        cg = xc_ref[:, D_SSM + GN + g * D_STATE:D_SSM + GN + (g + 1) * D_STATE].astype(BF16)
        cols = slice(g * GROUP_W, (g + 1) * GROUP_W)
        h_prev = hst_ref[:, cols]
        cb = lax.dot_general(cg, bg.astype(BF16), NT_DIMS, preferred_element_type=F32)
        y_off = jnp.dot(cg, h_prev.astype(BF16), preferred_element_type=F32)
        xs_g = xc_ref[:, cols]
        both = jnp.dot(factors, expand_ref[:, cols], preferred_element_type=F32)
        e_exp, w_exp = both[:lc], both[lc:]
        for pr in range(HPG // 2):
            mats = []
            for hh in (g * HPG + 2 * pr, g * HPG + 2 * pr + 1):
                seg = a2[:, hh:hh + 1] - r_t[hh:hh + 1, :]
                dec = jnp.exp2(jnp.where(causal, seg, -jnp.inf))
                mats.append((cb * dec).astype(BF16))
            xp = xs_g[:, pr * LANES:(pr + 1) * LANES].astype(BF16)
            rhs = jnp.concatenate([jnp.where(lane_lo, xp, 0), jnp.where(lane_lo, 0, xp)], axis=0)
            y_diag = jnp.dot(jnp.concatenate(mats, axis=1), rhs, preferred_element_type=F32)
            pc = slice(g * GROUP_W + pr * LANES, g * GROUP_W + (pr + 1) * LANES)
            ysc_ref[:, pc] = y_diag + (y_off * e_exp)[:, pr * LANES:(pr + 1) * LANES]
        xw = (xs_g * w_exp).astype(BF16)
        s_t = jnp.dot(bg.T.astype(BF16), xw, preferred_element_type=F32)
        hst_ref[:, cols] = h_prev * e_exp[lc - 1:lc, :] + s_t

    y = ysc_ref[...] + xc_ref[:, 0:D_SSM] * dexp_ref[...]
    _gated_norm_store(y, zg_ref, gnw_ref, y_ref)

    @pl.when(c == nc - 1)
    def _():
        hout_ref[0] = hst_ref[...].T


def _ssd_short_body(zg_ref, xc_ref, dt_ref, h0_ref, alr_ref, expand_ref, exps_ref, dexp_ref, gnw_ref,
                    y_ref, hout_ref, ysc_ref, *, lc, spb):
    rows = spb * lc
    hl = HEADS * lc
    dt = dt_ref[...]
    a_cum = _prefix_sum_rows(dt * (-jnp.exp(alr_ref[...])), lc)
    a_last = a_cum.reshape(spb, lc, LANES)[:, lc - 1:lc, :]
    a_last_b = jnp.broadcast_to(a_last, (spb, lc, LANES)).reshape(rows, LANES)
    expand = expand_ref[...]
    e_exp = _dot_split(jnp.exp(a_cum), expand, 2)
    w_exp = _dot_split(jnp.exp(a_last_b - a_cum) * dt, expand, 2)
    cd_t = jnp.exp(a_last.reshape(spb, LANES)).T

    z2 = _dot_split(jnp.concatenate([a_cum, dt], axis=0), exps_ref[...], 3)
    zc, zd = z2[:rows], z2[rows:]
    trow = lax.broadcasted_iota(jnp.int32, (rows, hl), 0) % lc
    tsrc = lax.broadcasted_iota(jnp.int32, (rows, hl), 1) % lc
    diag = trow == tsrc

    def per_source(z):
        d = jnp.where(diag, z, 0.0).reshape(spb, lc, hl).sum(axis=1, keepdims=True)
        return jnp.broadcast_to(d, (spb, lc, hl)).reshape(rows, hl)

    dec = jnp.exp(jnp.where(trow >= tsrc, zc - per_source(zc), -jnp.inf)) * per_source(zd)

    pair_w = HPG * lc
    blockdiag = (lax.broadcasted_iota(jnp.int32, (pair_w, GROUP_W), 0) // lc
                 == lax.broadcasted_iota(jnp.int32, (pair_w, GROUP_W), 1) // HEAD_DIM)

    for q in range(spb):
        tok = slice(q * lc, (q + 1) * lc)
        for g in range(GROUPS):
            bg = xc_ref[tok, D_SSM + g * D_STATE:D_SSM + (g + 1) * D_STATE]
            cg = xc_ref[tok, D_SSM + GN + g * D_STATE:D_SSM + GN + (g + 1) * D_STATE].astype(BF16)
            cols = slice(g * GROUP_W, (g + 1) * GROUP_W)
            h_prev = h0_ref[q, cols, :]
            xs_g = xc_ref[tok, cols]
            b_tiled = jnp.concatenate([bg] * HPG, axis=0).astype(BF16)
            cb = lax.dot_general(cg, b_tiled, NT_DIMS, preferred_element_type=F32)
            m = (cb * dec[tok, g * pair_w:(g + 1) * pair_w]).astype(BF16)
            x_diag = jnp.where(blockdiag, jnp.concatenate([xs_g] * HPG, axis=0), 0.0).astype(BF16)
            y_diag = jnp.dot(m, x_diag, preferred_element_type=F32)
            y_off = lax.dot_general(cg, h_prev.astype(BF16), NT_DIMS, preferred_element_type=F32)
            ysc_ref[tok, cols] = y_diag + y_off * e_exp[tok, cols]
            xw_t = (xs_g * w_exp[tok, cols]).T.astype(BF16)
            s_g = jnp.dot(xw_t, bg.astype(BF16), preferred_element_type=F32)
            cd = jnp.concatenate(
                [jnp.broadcast_to(cd_t[g * HPG + jh:g * HPG + jh + 1, q:q + 1], (HEAD_DIM, D_STATE))
                 for jh in range(HPG)], axis=0)
            hout_ref[q, cols, :] = h_prev * cd + s_g

    y = ysc_ref[...] + xc_ref[:, 0:D_SSM] * dexp_ref[...]
    _gated_norm_store(y, zg_ref, gnw_ref, y_ref)


def _ssd_chunked(zg, xc, dt, h0, alr, expand, dexp, gnw, w_f32, *, nb, nc):
    t = zg.shape[0]
    lc = SSD_CHUNK
    wrows = w_f32.shape[0] // nb
    assert wrows * nb == w_f32.shape[0] and wrows % BF16_SUBLANES == 0
    tok = lambda rows, width: pl.BlockSpec((rows, width), lambda b, c: (b * nc + c, 0))
    wspec = pl.BlockSpec((wrows, w_f32.shape[1]), lambda b, c: (b, 0))
    par = lambda r, width: pl.BlockSpec((r, width), lambda b, c: (0, 0))
    h_spec = pl.BlockSpec((1, D_SSM, D_STATE), lambda b, c: (b, 0, 0))
    return pl.pallas_call(
        functools.partial(_ssd_chunk_body, lc=lc, nc=nc),
        grid=(nb, nc),
        in_specs=[tok(lc, D_SSM), tok(lc, CONV_DIM), tok(lc, LANES), h_spec, par(1, LANES),
                  par(LANES, D_SSM), par(1, D_SSM), par(1, D_SSM), wspec],
        out_specs=[tok(lc, D_SSM), h_spec, wspec],
        out_shape=[jax.ShapeDtypeStruct((t, D_SSM), BF16),
                   jax.ShapeDtypeStruct((nb, D_SSM, D_STATE), F32),
                   jax.ShapeDtypeStruct(w_f32.shape, BF16)],
        scratch_shapes=[pltpu.VMEM((D_STATE, D_SSM), F32), pltpu.VMEM((lc, D_SSM), F32)],
        compiler_params=pltpu.CompilerParams(
            dimension_semantics=("arbitrary", "arbitrary"), vmem_limit_bytes=VMEM_LIMIT),
        name="ssd_scan",
    )(zg, xc, dt, h0, alr, expand, dexp, gnw, w_f32)


def _ssd_short(zg, xc, dt, h0, alr, expand, exps, dexp, gnw, *, nb, lc):
    t = zg.shape[0]
    spb = SHORT_SEQS_PER_STEP
    rows = spb * lc
    tok = lambda width: pl.BlockSpec((rows, width), lambda b: (b, 0))
    par = lambda r, width: pl.BlockSpec((r, width), lambda b: (0, 0))
    h_spec = pl.BlockSpec((spb, D_SSM, D_STATE), lambda b: (b, 0, 0))
    return pl.pallas_call(
        functools.partial(_ssd_short_body, lc=lc, spb=spb),
        grid=(nb // spb,),
        in_specs=[tok(D_SSM), tok(CONV_DIM), tok(LANES), h_spec, par(1, LANES), par(LANES, D_SSM),
                  par(LANES, HEADS * lc), par(1, D_SSM), par(1, D_SSM)],
        out_specs=[tok(D_SSM), h_spec],
        out_shape=[jax.ShapeDtypeStruct((t, D_SSM), BF16),
                   jax.ShapeDtypeStruct((nb, D_SSM, D_STATE), F32)],
        scratch_shapes=[pltpu.VMEM((rows, D_SSM), F32)],
        compiler_params=pltpu.CompilerParams(
            dimension_semantics=("arbitrary",), vmem_limit_bytes=VMEM_LIMIT),
        name="ssd_scan_short",
    )(zg, xc, dt, h0, alr, expand, exps, dexp, gnw)


def _out_body(ys_ref, yc_ref, x_ref, w1_ref, w2_ref, fnw_ref, o_ref):
    acc = jnp.dot(ys_ref[...], w1_ref[...], preferred_element_type=F32)
    acc = acc + jnp.dot(yc_ref[...], w2_ref[...], preferred_element_type=F32)
    r = x_ref[...] + acc
    var = jnp.mean(r * r, axis=-1, keepdims=True)
    o_ref[...] = r * lax.rsqrt(var + EPS) * fnw_ref[...]


def _out(ys, yc, x, w_o, fnw, *, tm):
    t = x.shape[0]
    row = lambda width: pl.BlockSpec((tm, width), lambda m: (m, 0))
    return pl.pallas_call(
        _out_body,
        grid=(t // tm,),
        in_specs=[row(D_SSM), row(D_SHORT), row(D_MODEL),
                  pl.BlockSpec((D_SSM, D_MODEL), lambda m: (0, 0), pipeline_mode=pl.Buffered(1)),
                  pl.BlockSpec((D_SHORT, D_MODEL), lambda m: (1, 0), pipeline_mode=pl.Buffered(1)),
                  pl.BlockSpec((1, D_MODEL), lambda m: (0, 0))],
        out_specs=row(D_MODEL),
        out_shape=jax.ShapeDtypeStruct((t, D_MODEL), F32),
        compiler_params=pltpu.CompilerParams(
            dimension_semantics=("arbitrary",), vmem_limit_bytes=VMEM_LIMIT),
        name="out_proj",
    )(ys, yc, x, w_o, w_o, fnw)


def _projections(x2d, conv_state, short_state, p, *, seqlen):
    tm = ROW_TILE
    nseq = max(tm // seqlen, 1)
    tps = max(seqlen // tm, 1)
    assert x2d.shape[0] % tm == 0 and (seqlen % tm == 0 or tm % seqlen == 0)
    h, zg, dt = _proj_z(x2d, p["nw"], p["w_t"], p["dtb"], tm=tm)
    xc, conv_new = _proj_xc(h, p["w_t"], p["conv_w"], p["conv_b"], conv_state,
                            tm=tm, nseq=nseq, tps=tps)
    tm_s = min(SHORT_ROW_TILE, x2d.shape[0])
    assert seqlen % tm_s == 0 or tm_s % seqlen == 0
    y_c, short_new = _short(h, p["w_t"], p["conv_short_w"], short_state,
                            tm=tm_s, tn=SHORT_TN, nseq=max(tm_s // seqlen, 1), tps=max(seqlen // tm_s, 1))
    return zg, xc, dt, y_c, conv_new, short_new


def kernel(x_prompt, x_sample, state_ssm, state_conv_ssd, state_conv_short, norm_w, w_in, conv_ssd_w,
           conv_ssd_b, dt_bias, a_log, d_skip, ssd_norm_w, conv_short_w, w_out, final_norm_w):
    depth = norm_w.shape[0]
    assert depth == 1, "the output projection fuses the final rmsnorm, valid for a single layer"
    bp, lp, _ = x_prompt.shape
    bs, ls, _ = x_sample.shape
    assert lp % SSD_CHUNK == 0 and ls < SSD_CHUNK
    hp = x_prompt.reshape(bp * lp, D_MODEL)
    hs = x_sample.reshape(bs * ls, D_MODEL)
    pad_h = LANES - HEADS
    expand = (jnp.arange(LANES)[:, None] == jnp.arange(D_SSM)[None, :] // HEAD_DIM).astype(BF16)
    expand_pairs = (jnp.arange(LANES)[:, None] == jnp.arange(HEADS * ls)[None, :] // ls).astype(BF16)
    outs = [[] for _ in range(6)]
    for layer in range(depth):
        p = dict(
            nw=norm_w[layer][None, :],
            w_t=w_in[layer].T,
            conv_short_w=conv_short_w[layer],
            conv_w=conv_ssd_w[layer],
            conv_b=conv_ssd_b[layer][None, :],
            dtb=jnp.pad(dt_bias[layer], (0, pad_h))[None, :],
        )
        alr = jnp.pad(a_log[layer], (0, pad_h))[None, :]
        dexp = jnp.repeat(d_skip[layer], HEAD_DIM)[None, :]
        gnw = ssd_norm_w[layer][None, :]

        zg, xc, dt, yc, a2, a3 = _projections(
            hp, jnp.zeros((bp, CONV_K - 1, CONV_DIM), F32), jnp.zeros((bp, SHORT_K - 1, D_SHORT), F32), p,
            seqlen=lp)
        ys, a1, w_o = _ssd_chunked(zg, xc, dt, jnp.zeros((bp, D_SSM, D_STATE), F32), alr, expand, dexp,
                                   gnw, w_out[layer], nb=bp, nc=lp // SSD_CHUNK)
        zg2, xc2, dt2, yc2, s2, s3 = _projections(
            hs, state_conv_ssd[layer], state_conv_short[layer], p, seqlen=ls)
        ys2, s1 = _ssd_short(zg2, xc2, dt2, state_ssm[layer].reshape(bs, D_SSM, D_STATE), alr, expand,
                             expand_pairs, dexp, gnw, nb=bs, lc=ls)
        fw = final_norm_w[None, :]
        hp = _out(ys, yc, hp, w_o, fw, tm=OUT_ROW_TILE)
        hs = _out(ys2, yc2, hs, w_o, fw, tm=OUT_ROW_TILE)
        for lst, val in zip(outs, (a1.reshape(bp, HEADS, HEAD_DIM, D_STATE), a2, a3,
                                   s1.reshape(bs, HEADS, HEAD_DIM, D_STATE), s2, s3)):
            lst.append(val)
    return (hp.reshape(bp, lp, D_MODEL), hs.reshape(bs, ls, D_MODEL),
            *(jnp.stack(v) for v in outs))
```

```python
import functools

import jax
import jax.numpy as jnp
from jax import lax
from jax.experimental import pallas as pl
from jax.experimental.pallas import tpu as pltpu

F32 = jnp.float32
BF16 = jnp.bfloat16

D_MODEL = 2048
D_SSM = 2048
D_SHORT = 2048
HEADS = 32
HEAD_DIM = 64
D_STATE = 128
GROUPS = 4
HPG = HEADS // GROUPS
GROUP_W = HPG * HEAD_DIM
GN = GROUPS * D_STATE
CONV_DIM = D_SSM + 2 * GN
CONV_K = 4
SHORT_K = 3
SSD_CHUNK = 128
EPS = 1e-6
LANES = 128
SUBLANES = 8
BF16_SUBLANES = 16
VMEM_LIMIT = 56 * 1024 * 1024

ROW_TILE = 1024
TN = 1024
SHORT_ROW_TILE = 2048
SHORT_TN = 256
OUT_ROW_TILE = 512
NORM_ROWS = 32
NORM_UNROLL = 8
Z_TILES = D_SSM // TN
XC_TILES = CONV_DIM // TN
MAIN_VALID = D_SSM + CONV_DIM + HEADS
DT_BLOCK = (D_SSM + CONV_DIM) // LANES

NT_DIMS = (((1,), (1,)), ((), ()))
LOG2E = 1.4426950408889634
CHUNKS_PER_STEP = 2
SHORT_SEQS_PER_STEP = 8


def _silu(x):
    half = 0.5 * x
    return half + half * jnp.tanh(half)


def _softplus(x):
    return jnp.maximum(x, 0.0) + jnp.log1p(jnp.exp(-jnp.abs(x)))


def _proj_z_body(x_ref, nw_ref, w_ref, wdt_ref, dtb_ref, h_ref, zg_ref, dt_ref, *wbf, tm):
    j = pl.program_id(1)
    if wbf:
        @pl.when(pl.program_id(0) == 0)
        def _():
            wbf[0][j] = w_ref[...].astype(BF16)

    @pl.when(j == 0)
    def _():
        nw = nw_ref[...]

        def body(i, carry):
            r = pl.ds(pl.multiple_of(i * NORM_ROWS, NORM_ROWS), NORM_ROWS)
            xf = x_ref[r, :]
            var = jnp.mean(xf * xf, axis=-1, keepdims=True)
            h_ref[r, :] = (xf * lax.rsqrt(var + EPS) * nw).astype(BF16)
            return carry

        lax.fori_loop(0, tm // NORM_ROWS, body, 0, unroll=NORM_UNROLL)
        dt_raw = lax.dot_general(h_ref[...], wdt_ref[...].astype(BF16), NT_DIMS,
                                 preferred_element_type=F32)
        lane = lax.broadcasted_iota(jnp.int32, dt_raw.shape, 1)
        dt_ref[...] = jnp.where(lane < HEADS, _softplus(dt_raw + dtb_ref[...]), 0.0)

    w = wbf[0][j] if wbf else w_ref[...].astype(BF16)
    acc = lax.dot_general(h_ref[...], w, NT_DIMS, preferred_element_type=F32)
    zg_ref[...] = _silu(acc).astype(BF16)


def _proj_z(x, nw, w_t, dtb, *, tm):
    t = x.shape[0]
    if t // tm > 1:
        w_spec = pl.BlockSpec((TN, D_MODEL), lambda m, j: (jnp.where(m == 0, j, Z_TILES - 1), 0),
                              pipeline_mode=pl.Buffered(1))
        scratch = [pltpu.VMEM((Z_TILES, TN, D_MODEL), BF16)]
    else:
        w_spec = pl.BlockSpec((TN, D_MODEL), lambda m, j: (j, 0))
        scratch = []
    return pl.pallas_call(
        functools.partial(_proj_z_body, tm=tm),
        grid=(t // tm, Z_TILES),
        in_specs=[
            pl.BlockSpec((tm, D_MODEL), lambda m, j: (m, 0)),
            pl.BlockSpec((1, D_MODEL), lambda m, j: (0, 0)),
            w_spec,
            pl.BlockSpec((LANES, D_MODEL), lambda m, j: (DT_BLOCK, 0)),
            pl.BlockSpec((1, LANES), lambda m, j: (0, 0)),
        ],
        out_specs=[pl.BlockSpec((tm, D_MODEL), lambda m, j: (m, 0)),
                   pl.BlockSpec((tm, TN), lambda m, j: (m, j)),
                   pl.BlockSpec((tm, LANES), lambda m, j: (m, 0))],
        out_shape=[jax.ShapeDtypeStruct((t, D_MODEL), BF16),
                   jax.ShapeDtypeStruct((t, D_SSM), BF16),
                   jax.ShapeDtypeStruct((t, LANES), F32)],
        scratch_shapes=scratch,
        compiler_params=pltpu.CompilerParams(
            dimension_semantics=("arbitrary", "arbitrary"), vmem_limit_bytes=VMEM_LIMIT),
        name="in_proj_z",
    )(x, nw, w_t, w_t, dtb)


def _load_history(pad_ref, st_ref, carry, j, *, first, hist, tps):
    def from_state():
        pad_ref[:, 0:SUBLANES, :] = jnp.zeros((pad_ref.shape[0], SUBLANES, pad_ref.shape[2]), F32)
        pad_ref[:, SUBLANES - hist:SUBLANES, :] = st_ref[...]

    if tps > 1:
        (carry_ref,) = carry
        pl.when(first)(from_state)

        @pl.when(jnp.logical_not(first))
        def _():
            pad_ref[:, 0:SUBLANES, :] = carry_ref[j]
    else:
        from_state()


def _causal_conv(pad_ref, cw_ref, taps):
    xe = pad_ref[...]
    if taps == 4:
        s1 = pltpu.roll(xe, 1, axis=1)
        near = xe * cw_ref[3:4, :] + s1 * cw_ref[2:3, :]
        far = xe * cw_ref[1:2, :] + s1 * cw_ref[0:1, :]
        return near[:, SUBLANES:, :] + pltpu.roll(far, 2, axis=1)[:, SUBLANES:, :]
    conv = xe[:, SUBLANES:, :] * cw_ref[taps - 1:taps, :]
    for s in range(1, taps):
        shifted = pltpu.roll(xe, s, axis=1)[:, SUBLANES:, :]
        conv = conv + shifted * cw_ref[taps - 1 - s:taps - s, :]
    return conv


def _proj_xc_body(h_ref, w_ref, cw_ref, cb_ref, st_ref, xc_ref, tails_ref, pad_ref, *carry,
                  tm, nseq, tps):
    j = pl.program_id(1)
    lt = tm // nseq
    _load_history(pad_ref, st_ref, carry, j, first=(pl.program_id(0) % tps) == 0, hist=CONV_K - 1,
                  tps=tps)
    acc = lax.dot_general(h_ref[...], w_ref[...].astype(BF16), NT_DIMS, preferred_element_type=F32)
    pad_ref[:, SUBLANES:SUBLANES + lt, :] = acc.reshape(nseq, lt, TN)
    conv = _causal_conv(pad_ref, cw_ref, CONV_K)
    xc_ref[...] = _silu(conv + cb_ref[...]).reshape(tm, TN)
    tails_ref[...] = pad_ref[:, lt + SUBLANES - (CONV_K - 1):lt + SUBLANES, :]
    if tps > 1:
        carry[0][j] = pad_ref[:, lt:lt + SUBLANES, :]


def _proj_xc(h, w_t, conv_w, conv_b, conv_state, *, tm, nseq, tps):
    t = h.shape[0]
    lt = tm // nseq
    scratch = [pltpu.VMEM((nseq, SUBLANES + lt, TN), F32)]
    if tps > 1:
        scratch.append(pltpu.VMEM((XC_TILES, nseq, SUBLANES, TN), F32))
    xc, tails = pl.pallas_call(
        functools.partial(_proj_xc_body, tm=tm, nseq=nseq, tps=tps),
        grid=(t // tm, XC_TILES),
        in_specs=[
            pl.BlockSpec((tm, D_MODEL), lambda m, j: (m, 0)),
            pl.BlockSpec((TN, D_MODEL), lambda m, j: (Z_TILES + j, 0)),
            pl.BlockSpec((CONV_K, TN), lambda m, j: (0, j)),
            pl.BlockSpec((1, TN), lambda m, j: (0, j)),
            pl.BlockSpec((nseq, CONV_K - 1, TN), lambda m, j: (m // tps, 0, j)),
        ],
        out_specs=[pl.BlockSpec((tm, TN), lambda m, j: (m, j)),
                   pl.BlockSpec((nseq, CONV_K - 1, TN), lambda m, j: (m, 0, j))],
        out_shape=[jax.ShapeDtypeStruct((t, CONV_DIM), F32),
                   jax.ShapeDtypeStruct(((t // tm) * nseq, CONV_K - 1, CONV_DIM), F32)],
        scratch_shapes=scratch,
        compiler_params=pltpu.CompilerParams(
            dimension_semantics=("arbitrary", "arbitrary"), vmem_limit_bytes=VMEM_LIMIT),
        name="in_proj_xc",
    )(h, w_t, conv_w, conv_b, conv_state)
    tails = tails.reshape(t // (tm * tps), tps, nseq, CONV_K - 1, CONV_DIM)[:, -1]
    return xc, tails.reshape(conv_state.shape)


def _short_body(h_ref, *refs, tm, tn, nseq, tps, shift):
    w_refs, (cw_ref, st_ref, y_ref, ns_ref, wbf_ref, pad_ref), carry = refs[:8], refs[8:14], refs[14:]
    m = pl.program_id(1)
    lt = tm // nseq

    @pl.when(m == 0)
    def _():
        for k in range(4):
            a_ref, b_ref = w_refs[2 * k], w_refs[2 * k + 1]
            wbf_ref[k] = jnp.concatenate([a_ref[shift:, :], b_ref[...]], axis=0).astype(BF16)

    _load_history(pad_ref, st_ref, carry, 0, first=(m % tps) == 0, hist=SHORT_K - 1, tps=tps)
    h = h_ref[...]
    band = lambda k: lax.dot_general(h, wbf_ref[k], NT_DIMS, preferred_element_type=F32)
    c = band(2)
    v = band(3)
    pad_ref[:, SUBLANES:SUBLANES + lt, :] = (c * v).reshape(nseq, lt, tn)
    conv = _causal_conv(pad_ref, cw_ref, SHORT_K)
    b = band(1)
    z = band(0)
    y = b * conv.reshape(tm, tn) * _silu(z)
    y_ref[...] = y.astype(y_ref.dtype)
    ns_ref[...] = pad_ref[:, lt + SUBLANES - (SHORT_K - 1):lt + SUBLANES, :]
    if tps > 1:
        carry[0][0] = pad_ref[:, lt:lt + SUBLANES, :]


def _short(h, w_t, conv_w, state, *, tm, tn, nseq, tps):
    t = h.shape[0]
    nj = D_SHORT // tn
    lt = tm // nseq
    base = (MAIN_VALID // tn) * tn
    shift = MAIN_VALID - base
    assert shift % SUBLANES == 0 and tn % shift == 0 and base % shift == 0
    scratch = [pltpu.VMEM((4, tn, D_MODEL), BF16), pltpu.VMEM((nseq, SUBLANES + lt, tn), F32)]
    if tps > 1:
        scratch.append(pltpu.VMEM((1, nseq, SUBLANES, tn), F32))
    w_specs = []
    for k in range(4):
        r0 = base + k * D_SHORT
        w_specs.append(pl.BlockSpec((tn, D_MODEL), lambda j, m, r0=r0: (r0 // tn + j, 0)))
        w_specs.append(pl.BlockSpec((shift, D_MODEL), lambda j, m, r0=r0: ((r0 + tn * (j + 1)) // shift, 0)))

    y, tails = pl.pallas_call(
        functools.partial(_short_body, tm=tm, tn=tn, nseq=nseq, tps=tps, shift=shift),
        grid=(nj, t // tm),
        in_specs=[
            pl.BlockSpec((tm, D_MODEL), lambda j, m: (m, 0)),
            *w_specs,
            pl.BlockSpec((SHORT_K, tn), lambda j, m: (0, j)),
            pl.BlockSpec((nseq, SHORT_K - 1, tn), lambda j, m: (m // tps, 0, j)),
        ],
        out_specs=[pl.BlockSpec((tm, tn), lambda j, m: (m, j)),
                   pl.BlockSpec((nseq, SHORT_K - 1, tn), lambda j, m: (m, 0, j))],
        out_shape=[jax.ShapeDtypeStruct((t, D_SHORT), BF16),
                   jax.ShapeDtypeStruct(((t // tm) * nseq, SHORT_K - 1, D_SHORT), F32)],
        scratch_shapes=scratch,
        compiler_params=pltpu.CompilerParams(
            dimension_semantics=("arbitrary", "arbitrary"), vmem_limit_bytes=VMEM_LIMIT),
        name="short_conv",
    )(h, *([w_t] * 8), conv_w, state)
    tails = tails.reshape(t // (tm * tps), tps, nseq, SHORT_K - 1, D_SHORT)[:, -1]
    return y, tails.reshape(state.shape)


def _dot_split(x, w, pieces):
    out = None
    r = x
    for _ in range(pieces):
        p = r.astype(BF16)
        d = jnp.dot(p, w, preferred_element_type=F32)
        out = d if out is None else out + d
        r = r - p.astype(F32)
    return out


def _prefix_sum_rows(x, period):
    t = lax.broadcasted_iota(jnp.int32, x.shape, 0) % period
    k = 1
    while k < period:
        x = x + jnp.where(t >= k, pltpu.roll(x, k, axis=0), 0.0)
        k *= 2
    return x


def _gated_norm_store(y, zg_ref, gnw_ref, y_ref):
    y = y * zg_ref[...].astype(F32)
    for g in range(GROUPS):
        cs = slice(g * GROUP_W, (g + 1) * GROUP_W)
        blk = y[:, cs]
        ms = jnp.mean(blk * blk, axis=-1, keepdims=True)
        y_ref[:, cs] = (blk * lax.rsqrt(ms + EPS) * gnw_ref[:, cs]).astype(y_ref.dtype)


def _ssd_chunk_body(zg_ref, xc_ref, dt_ref, h0_ref, alr_ref, expand_ref, dexp_ref, gnw_ref, wsrc_ref,
                    y_ref, hout_ref, wdst_ref, hst_ref, ysc_ref, *, lc, cps, nsteps):
    c = pl.program_id(1)
    wdst_ref[...] = wsrc_ref[...].astype(BF16)

    @pl.when(c == 0)
    def _():
        hst_ref[...] = h0_ref[0].T

    row_i = lax.broadcasted_iota(jnp.int32, (lc, lc), 0)
    col_i = lax.broadcasted_iota(jnp.int32, (lc, lc), 1)
    causal = row_i >= col_i
    lane_lo = lax.broadcasted_iota(jnp.int32, (lc, LANES), 1) < HEAD_DIM

    a_row = -jnp.exp(alr_ref[...])

    for k in range(cps):
        tok = slice(k * lc, (k + 1) * lc)
        dt = dt_ref[tok, :]
        a_cum = _prefix_sum_rows(dt * a_row, lc)
        a_last = a_cum[lc - 1:lc, :]
        factors = jnp.concatenate([jnp.exp(a_cum), jnp.exp(a_last - a_cum) * dt], axis=0).astype(BF16)
        a2 = a_cum * LOG2E
        r_t = (a2 - jnp.log2(dt)).T

        for g in range(GROUPS):
            bg = xc_ref[tok, D_SSM + g * D_STATE:D_SSM + (g + 1) * D_STATE]
            cg = xc_ref[tok, D_SSM + GN + g * D_STATE:D_SSM + GN + (g + 1) * D_STATE].astype(BF16)
            cols = slice(g * GROUP_W, (g + 1) * GROUP_W)
            h_prev = hst_ref[:, cols]
            cb = lax.dot_general(cg, bg.astype(BF16), NT_DIMS, preferred_element_type=F32)
            y_off = jnp.dot(cg, h_prev.astype(BF16), preferred_element_type=F32)
            xs_g = xc_ref[tok, cols]
            both = jnp.dot(factors, expand_ref[:, cols], preferred_element_type=F32)
            e_exp, w_exp = both[:lc], both[lc:]
            for pr in range(HPG // 2):
                mats = []
                for hh in (g * HPG + 2 * pr, g * HPG + 2 * pr + 1):
                    seg = a2[:, hh:hh + 1] - r_t[hh:hh + 1, :]
                    dec = jnp.exp2(jnp.where(causal, seg, -jnp.inf))
                    mats.append((cb * dec).astype(BF16))
                xp = xs_g[:, pr * LANES:(pr + 1) * LANES].astype(BF16)
                rhs = jnp.concatenate([jnp.where(lane_lo, xp, 0), jnp.where(lane_lo, 0, xp)], axis=0)
                y_diag = jnp.dot(jnp.concatenate(mats, axis=1), rhs, preferred_element_type=F32)
                pc = slice(g * GROUP_W + pr * LANES, g * GROUP_W + (pr + 1) * LANES)
                ysc_ref[tok, pc] = y_diag + (y_off * e_exp)[:, pr * LANES:(pr + 1) * LANES]
            xw = (xs_g * w_exp).astype(BF16)
            s_t = jnp.dot(bg.T.astype(BF16), xw, preferred_element_type=F32)
            hst_ref[:, cols] = h_prev * e_exp[lc - 1:lc, :] + s_t

    y = ysc_ref[...] + xc_ref[:, 0:D_SSM] * dexp_ref[...]
    _gated_norm_store(y, zg_ref, gnw_ref, y_ref)

    @pl.when(c == nsteps - 1)
    def _():
        hout_ref[0] = hst_ref[...].T


def _ssd_short_body(zg_ref, xc_ref, dt_ref, h0_ref, alr_ref, expand_ref, exps_ref, dexp_ref, gnw_ref,
                    y_ref, hout_ref, ysc_ref, *, lc, spb):
    rows = spb * lc
    hl = HEADS * lc
    dt = dt_ref[...]
    a_cum = _prefix_sum_rows(dt * (-jnp.exp(alr_ref[...])), lc)
    a_last = a_cum.reshape(spb, lc, LANES)[:, lc - 1:lc, :]
    a_last_b = jnp.broadcast_to(a_last, (spb, lc, LANES)).reshape(rows, LANES)
    expand = expand_ref[...]
    e_exp = _dot_split(jnp.exp(a_cum), expand, 2)
    w_exp = _dot_split(jnp.exp(a_last_b - a_cum) * dt, expand, 2)
    cd_t = jnp.exp(a_last.reshape(spb, LANES)).T

    z2 = _dot_split(jnp.concatenate([a_cum, dt], axis=0), exps_ref[...], 3)
    zc, zd = z2[:rows], z2[rows:]
    trow = lax.broadcasted_iota(jnp.int32, (rows, hl), 0) % lc
    tsrc = lax.broadcasted_iota(jnp.int32, (rows, hl), 1) % lc
    diag = trow == tsrc

    def per_source(z):
        d = jnp.where(diag, z, 0.0).reshape(spb, lc, hl).sum(axis=1, keepdims=True)
        return jnp.broadcast_to(d, (spb, lc, hl)).reshape(rows, hl)

    dec = jnp.exp(jnp.where(trow >= tsrc, zc - per_source(zc), -jnp.inf)) * per_source(zd)

    pair_w = HPG * lc
    blockdiag = (lax.broadcasted_iota(jnp.int32, (pair_w, GROUP_W), 0) // lc
                 == lax.broadcasted_iota(jnp.int32, (pair_w, GROUP_W), 1) // HEAD_DIM)

    for q in range(spb):
        tok = slice(q * lc, (q + 1) * lc)
        for g in range(GROUPS):
            bg = xc_ref[tok, D_SSM + g * D_STATE:D_SSM + (g + 1) * D_STATE]
            cg = xc_ref[tok, D_SSM + GN + g * D_STATE:D_SSM + GN + (g + 1) * D_STATE].astype(BF16)
            cols = slice(g * GROUP_W, (g + 1) * GROUP_W)
            h_prev = h0_ref[q, cols, :]
            xs_g = xc_ref[tok, cols]
            b_tiled = jnp.concatenate([bg] * HPG, axis=0).astype(BF16)
            cb = lax.dot_general(cg, b_tiled, NT_DIMS, preferred_element_type=F32)
            m = (cb * dec[tok, g * pair_w:(g + 1) * pair_w]).astype(BF16)
            x_diag = jnp.where(blockdiag, jnp.concatenate([xs_g] * HPG, axis=0), 0.0).astype(BF16)
            y_diag = jnp.dot(m, x_diag, preferred_element_type=F32)
            y_off = lax.dot_general(cg, h_prev.astype(BF16), NT_DIMS, preferred_element_type=F32)
            ysc_ref[tok, cols] = y_diag + y_off * e_exp[tok, cols]
            xw_t = (xs_g * w_exp[tok, cols]).T.astype(BF16)
            s_g = jnp.dot(xw_t, bg.astype(BF16), preferred_element_type=F32)
            cd = jnp.concatenate(
                [jnp.broadcast_to(cd_t[g * HPG + jh:g * HPG + jh + 1, q:q + 1], (HEAD_DIM, D_STATE))
                 for jh in range(HPG)], axis=0)
            hout_ref[q, cols, :] = h_prev * cd + s_g

    y = ysc_ref[...] + xc_ref[:, 0:D_SSM] * dexp_ref[...]
    _gated_norm_store(y, zg_ref, gnw_ref, y_ref)


def _ssd_chunked(zg, xc, dt, h0, alr, expand, dexp, gnw, w_f32, *, nb, nc):
    t = zg.shape[0]
    lc = SSD_CHUNK
    cps = CHUNKS_PER_STEP
    nsteps = nc // cps
    rows = cps * lc
    wrows = w_f32.shape[0] // (nb * nsteps)
    assert nsteps * cps == nc and wrows * nb * nsteps == w_f32.shape[0] and wrows % BF16_SUBLANES == 0
    tok = lambda r, width: pl.BlockSpec((r, width), lambda b, c: (b * nsteps + c, 0))
    par = lambda r, width: pl.BlockSpec((r, width), lambda b, c: (0, 0))
    h_spec = pl.BlockSpec((1, D_SSM, D_STATE), lambda b, c: (b, 0, 0))
    return pl.pallas_call(
        functools.partial(_ssd_chunk_body, lc=lc, cps=cps, nsteps=nsteps),
        grid=(nb, nsteps),
        in_specs=[tok(rows, D_SSM), tok(rows, CONV_DIM), tok(rows, LANES), h_spec, par(1, LANES),
                  par(LANES, D_SSM), par(1, D_SSM), par(1, D_SSM), tok(wrows, w_f32.shape[1])],
        out_specs=[tok(rows, D_SSM), h_spec, tok(wrows, w_f32.shape[1])],
        out_shape=[jax.ShapeDtypeStruct((t, D_SSM), BF16),
                   jax.ShapeDtypeStruct((nb, D_SSM, D_STATE), F32),
                   jax.ShapeDtypeStruct(w_f32.shape, BF16)],
        scratch_shapes=[pltpu.VMEM((D_STATE, D_SSM), F32), pltpu.VMEM((rows, D_SSM), F32)],
        compiler_params=pltpu.CompilerParams(
            dimension_semantics=("arbitrary", "arbitrary"), vmem_limit_bytes=VMEM_LIMIT),
        name="ssd_scan",
    )(zg, xc, dt, h0, alr, expand, dexp, gnw, w_f32)


def _ssd_short(zg, xc, dt, h0, alr, expand, exps, dexp, gnw, *, nb, lc):
    t = zg.shape[0]
    spb = SHORT_SEQS_PER_STEP
    rows = spb * lc
    tok = lambda width: pl.BlockSpec((rows, width), lambda b: (b, 0))
    par = lambda r, width: pl.BlockSpec((r, width), lambda b: (0, 0))
    h_spec = pl.BlockSpec((spb, D_SSM, D_STATE), lambda b: (b, 0, 0))
    return pl.pallas_call(
        functools.partial(_ssd_short_body, lc=lc, spb=spb),
        grid=(nb // spb,),
        in_specs=[tok(D_SSM), tok(CONV_DIM), tok(LANES), h_spec, par(1, LANES), par(LANES, D_SSM),
                  par(LANES, HEADS * lc), par(1, D_SSM), par(1, D_SSM)],
        out_specs=[tok(D_SSM), h_spec],
        out_shape=[jax.ShapeDtypeStruct((t, D_SSM), BF16),
                   jax.ShapeDtypeStruct((nb, D_SSM, D_STATE), F32)],
        scratch_shapes=[pltpu.VMEM((rows, D_SSM), F32)],
        compiler_params=pltpu.CompilerParams(
            dimension_semantics=("arbitrary",), vmem_limit_bytes=VMEM_LIMIT),
        name="ssd_scan_short",
    )(zg, xc, dt, h0, alr, expand, exps, dexp, gnw)


def _out_body(ys_ref, yc_ref, x_ref, w1_ref, w2_ref, fnw_ref, o_ref):
    acc = jnp.dot(ys_ref[...], w1_ref[...], preferred_element_type=F32)
    acc = acc + jnp.dot(yc_ref[...], w2_ref[...], preferred_element_type=F32)
    r = x_ref[...] + acc
    var = jnp.mean(r * r, axis=-1, keepdims=True)
    o_ref[...] = r * lax.rsqrt(var + EPS) * fnw_ref[...]


def _out(ys, yc, x, w_o, fnw, *, tm):
    t = x.shape[0]
    row = lambda width: pl.BlockSpec((tm, width), lambda m: (m, 0))
    return pl.pallas_call(
        _out_body,
        grid=(t // tm,),
        in_specs=[row(D_SSM), row(D_SHORT), row(D_MODEL),
                  pl.BlockSpec((D_SSM, D_MODEL), lambda m: (0, 0), pipeline_mode=pl.Buffered(1)),
                  pl.BlockSpec((D_SHORT, D_MODEL), lambda m: (1, 0), pipeline_mode=pl.Buffered(1)),
                  pl.BlockSpec((1, D_MODEL), lambda m: (0, 0))],
        out_specs=row(D_MODEL),
        out_shape=jax.ShapeDtypeStruct((t, D_MODEL), F32),
        compiler_params=pltpu.CompilerParams(
            dimension_semantics=("arbitrary",), vmem_limit_bytes=VMEM_LIMIT),
        name="out_proj",
    )(ys, yc, x, w_o, w_o, fnw)


def _projections(x2d, conv_state, short_state, p, *, seqlen):
    tm = ROW_TILE
    nseq = max(tm // seqlen, 1)
    tps = max(seqlen // tm, 1)
    assert x2d.shape[0] % tm == 0 and (seqlen % tm == 0 or tm % seqlen == 0)
    h, zg, dt = _proj_z(x2d, p["nw"], p["w_t"], p["dtb"], tm=tm)
    xc, conv_new = _proj_xc(h, p["w_t"], p["conv_w"], p["conv_b"], conv_state,
                            tm=tm, nseq=nseq, tps=tps)
    tm_s = min(SHORT_ROW_TILE, x2d.shape[0])
    assert seqlen % tm_s == 0 or tm_s % seqlen == 0
    y_c, short_new = _short(h, p["w_t"], p["conv_short_w"], short_state,
                            tm=tm_s, tn=SHORT_TN, nseq=max(tm_s // seqlen, 1), tps=max(seqlen // tm_s, 1))
    return zg, xc, dt, y_c, conv_new, short_new


def kernel(x_prompt, x_sample, state_ssm, state_conv_ssd, state_conv_short, norm_w, w_in, conv_ssd_w,
           conv_ssd_b, dt_bias, a_log, d_skip, ssd_norm_w, conv_short_w, w_out, final_norm_w):
    depth = norm_w.shape[0]
    assert depth == 1, "the output projection fuses the final rmsnorm, valid for a single layer"
    bp, lp, _ = x_prompt.shape
    bs, ls, _ = x_sample.shape
    assert lp % SSD_CHUNK == 0 and ls < SSD_CHUNK
    hp = x_prompt.reshape(bp * lp, D_MODEL)
    hs = x_sample.reshape(bs * ls, D_MODEL)
    pad_h = LANES - HEADS
    expand = (jnp.arange(LANES)[:, None] == jnp.arange(D_SSM)[None, :] // HEAD_DIM).astype(BF16)
    expand_pairs = (jnp.arange(LANES)[:, None] == jnp.arange(HEADS * ls)[None, :] // ls).astype(BF16)
    outs = [[] for _ in range(6)]
    for layer in range(depth):
        p = dict(
            nw=norm_w[layer][None, :],
            w_t=w_in[layer].T,
            conv_short_w=conv_short_w[layer],
            conv_w=conv_ssd_w[layer],
            conv_b=conv_ssd_b[layer][None, :],
            dtb=jnp.pad(dt_bias[layer], (0, pad_h))[None, :],
        )
        alr = jnp.pad(a_log[layer], (0, pad_h))[None, :]
        dexp = jnp.repeat(d_skip[layer], HEAD_DIM)[None, :]
        gnw = ssd_norm_w[layer][None, :]

        zg, xc, dt, yc, a2, a3 = _projections(
            hp, jnp.zeros((bp, CONV_K - 1, CONV_DIM), F32), jnp.zeros((bp, SHORT_K - 1, D_SHORT), F32), p,
            seqlen=lp)
        ys, a1, w_o = _ssd_chunked(zg, xc, dt, jnp.zeros((bp, D_SSM, D_STATE), F32), alr, expand, dexp,
                                   gnw, w_out[layer], nb=bp, nc=lp // SSD_CHUNK)
        zg2, xc2, dt2, yc2, s2, s3 = _projections(
            hs, state_conv_ssd[layer], state_conv_short[layer], p, seqlen=ls)
        ys2, s1 = _ssd_short(zg2, xc2, dt2, state_ssm[layer].reshape(bs, D_SSM, D_STATE), alr, expand,
                             expand_pairs, dexp, gnw, nb=bs, lc=ls)
        fw = final_norm_w[None, :]
        hp = _out(ys, yc, hp, w_o, fw, tm=OUT_ROW_TILE)
        hs = _out(ys2, yc2, hs, w_o, fw, tm=OUT_ROW_TILE)
        for lst, val in zip(outs, (a1.reshape(bp, HEADS, HEAD_DIM, D_STATE), a2, a3,
                                   s1.reshape(bs, HEADS, HEAD_DIM, D_STATE), s2, s3)):
            lst.append(val)
    return (hp.reshape(bp, lp, D_MODEL), hs.reshape(bs, ls, D_MODEL),
            *(jnp.stack(v) for v in outs))
```

```python
import functools

import jax
import jax.numpy as jnp
from jax import lax
from jax.experimental import pallas as pl
from jax.experimental.pallas import tpu as pltpu

F32 = jnp.float32
BF16 = jnp.bfloat16

D_MODEL = 2048
D_SSM = 2048
D_SHORT = 2048
HEADS = 32
HEAD_DIM = 64
D_STATE = 128
GROUPS = 4
HPG = HEADS // GROUPS
GROUP_W = HPG * HEAD_DIM
GN = GROUPS * D_STATE
CONV_DIM = D_SSM + 2 * GN
CONV_K = 4
SHORT_K = 3
SSD_CHUNK = 128
EPS = 1e-6
LANES = 128
SUBLANES = 8
BF16_SUBLANES = 16
VMEM_LIMIT = 56 * 1024 * 1024

ROW_TILE = 1024
TN = 1024
SHORT_ROW_TILE = 2048
SHORT_TN = 256
OUT_ROW_TILE = 512
NORM_ROWS = 32
NORM_UNROLL = 8
Z_TILES = D_SSM // TN
XC_TILES = CONV_DIM // TN
MAIN_VALID = D_SSM + CONV_DIM + HEADS
DT_BLOCK = (D_SSM + CONV_DIM) // LANES

NT_DIMS = (((1,), (1,)), ((), ()))
LOG2E = 1.4426950408889634
CHUNKS_PER_STEP = 4
SHORT_SEQS_PER_STEP = 8


def _silu(x):
    half = 0.5 * x
    return half + half * jnp.tanh(half)


def _softplus(x):
    return jnp.maximum(x, 0.0) + jnp.log1p(jnp.exp(-jnp.abs(x)))


def _proj_z_body(x_ref, nw_ref, w_ref, wdt_ref, dtb_ref, h_ref, zg_ref, dt_ref, *wbf, tm):
    j = pl.program_id(1)
    if wbf:
        @pl.when(pl.program_id(0) == 0)
        def _():
            wbf[0][j] = w_ref[...].astype(BF16)

    @pl.when(j == 0)
    def _():
        nw = nw_ref[...]

        def body(i, carry):
            r = pl.ds(pl.multiple_of(i * NORM_ROWS, NORM_ROWS), NORM_ROWS)
            xf = x_ref[r, :]
            var = jnp.mean(xf * xf, axis=-1, keepdims=True)
            h_ref[r, :] = (xf * lax.rsqrt(var + EPS) * nw).astype(BF16)
            return carry

        lax.fori_loop(0, tm // NORM_ROWS, body, 0, unroll=NORM_UNROLL)
        dt_raw = lax.dot_general(h_ref[...], wdt_ref[...].astype(BF16), NT_DIMS,
                                 preferred_element_type=F32)
        lane = lax.broadcasted_iota(jnp.int32, dt_raw.shape, 1)
        dt_ref[...] = jnp.where(lane < HEADS, _softplus(dt_raw + dtb_ref[...]), 0.0)

    w = wbf[0][j] if wbf else w_ref[...].astype(BF16)
    acc = lax.dot_general(h_ref[...], w, NT_DIMS, preferred_element_type=F32)
    zg_ref[...] = _silu(acc).astype(BF16)


def _proj_z(x, nw, w_t, dtb, *, tm):
    t = x.shape[0]
    if t // tm > 1:
        w_spec = pl.BlockSpec((TN, D_MODEL), lambda m, j: (jnp.where(m == 0, j, Z_TILES - 1), 0),
                              pipeline_mode=pl.Buffered(1))
        scratch = [pltpu.VMEM((Z_TILES, TN, D_MODEL), BF16)]
    else:
        w_spec = pl.BlockSpec((TN, D_MODEL), lambda m, j: (j, 0))
        scratch = []
    return pl.pallas_call(
        functools.partial(_proj_z_body, tm=tm),
        grid=(t // tm, Z_TILES),
        in_specs=[
            pl.BlockSpec((tm, D_MODEL), lambda m, j: (m, 0)),
            pl.BlockSpec((1, D_MODEL), lambda m, j: (0, 0)),
            w_spec,
            pl.BlockSpec((LANES, D_MODEL), lambda m, j: (DT_BLOCK, 0)),
            pl.BlockSpec((1, LANES), lambda m, j: (0, 0)),
        ],
        out_specs=[pl.BlockSpec((tm, D_MODEL), lambda m, j: (m, 0)),
                   pl.BlockSpec((tm, TN), lambda m, j: (m, j)),
                   pl.BlockSpec((tm, LANES), lambda m, j: (m, 0))],
        out_shape=[jax.ShapeDtypeStruct((t, D_MODEL), BF16),
                   jax.ShapeDtypeStruct((t, D_SSM), BF16),
                   jax.ShapeDtypeStruct((t, LANES), F32)],
        scratch_shapes=scratch,
        compiler_params=pltpu.CompilerParams(
            dimension_semantics=("arbitrary", "arbitrary"), vmem_limit_bytes=VMEM_LIMIT),
        name="in_proj_z",
    )(x, nw, w_t, w_t, dtb)


def _load_history(pad_ref, st_ref, carry, j, *, first, hist, tps):
    def from_state():
        pad_ref[:, 0:SUBLANES, :] = jnp.zeros((pad_ref.shape[0], SUBLANES, pad_ref.shape[2]), F32)
        pad_ref[:, SUBLANES - hist:SUBLANES, :] = st_ref[...]

    if tps > 1:
        (carry_ref,) = carry
        pl.when(first)(from_state)

        @pl.when(jnp.logical_not(first))
        def _():
            pad_ref[:, 0:SUBLANES, :] = carry_ref[j]
    else:
        from_state()


def _causal_conv(pad_ref, cw_ref, taps):
    xe = pad_ref[...]
    if taps == 4:
        s1 = pltpu.roll(xe, 1, axis=1)
        near = xe * cw_ref[3:4, :] + s1 * cw_ref[2:3, :]
        far = xe * cw_ref[1:2, :] + s1 * cw_ref[0:1, :]
        return near[:, SUBLANES:, :] + pltpu.roll(far, 2, axis=1)[:, SUBLANES:, :]
    conv = xe[:, SUBLANES:, :] * cw_ref[taps - 1:taps, :]
    for s in range(1, taps):
        shifted = pltpu.roll(xe, s, axis=1)[:, SUBLANES:, :]
        conv = conv + shifted * cw_ref[taps - 1 - s:taps - s, :]
    return conv


def _proj_xc_body(h_ref, w_ref, cw_ref, cb_ref, st_ref, xc_ref, tails_ref, pad_ref, *scratch,
                  tm, nseq, tps, resident):
    j = pl.program_id(1)
    carry = scratch[1:] if resident else scratch
    if resident:
        @pl.when(pl.program_id(0) == 0)
        def _():
            scratch[0][j] = w_ref[...].astype(BF16)

    lt = tm // nseq
    _load_history(pad_ref, st_ref, carry, j, first=(pl.program_id(0) % tps) == 0, hist=CONV_K - 1,
                  tps=tps)
    w = scratch[0][j] if resident else w_ref[...].astype(BF16)
    acc = lax.dot_general(h_ref[...], w, NT_DIMS, preferred_element_type=F32)
    pad_ref[:, SUBLANES:SUBLANES + lt, :] = acc.reshape(nseq, lt, TN)
    conv = _causal_conv(pad_ref, cw_ref, CONV_K)
    xc_ref[...] = _silu(conv + cb_ref[...]).reshape(tm, TN)
    tails_ref[...] = pad_ref[:, lt + SUBLANES - (CONV_K - 1):lt + SUBLANES, :]
    if tps > 1:
        carry[0][j] = pad_ref[:, lt:lt + SUBLANES, :]


def _proj_xc(h, w_t, conv_w, conv_b, conv_state, *, tm, nseq, tps):
    t = h.shape[0]
    lt = tm // nseq
    resident = t // tm > 1
    scratch = [pltpu.VMEM((nseq, SUBLANES + lt, TN), F32)]
    if resident:
        scratch.append(pltpu.VMEM((XC_TILES, TN, D_MODEL), BF16))
        w_spec = pl.BlockSpec((TN, D_MODEL), lambda m, j: (Z_TILES + jnp.where(m == 0, j, XC_TILES - 1), 0))
    else:
        w_spec = pl.BlockSpec((TN, D_MODEL), lambda m, j: (Z_TILES + j, 0))
    if tps > 1:
        scratch.append(pltpu.VMEM((XC_TILES, nseq, SUBLANES, TN), F32))
    xc, tails = pl.pallas_call(
        functools.partial(_proj_xc_body, tm=tm, nseq=nseq, tps=tps, resident=resident),
        grid=(t // tm, XC_TILES),
        in_specs=[
            pl.BlockSpec((tm, D_MODEL), lambda m, j: (m, 0)),
            w_spec,
            pl.BlockSpec((CONV_K, TN), lambda m, j: (0, j)),
            pl.BlockSpec((1, TN), lambda m, j: (0, j)),
            pl.BlockSpec((nseq, CONV_K - 1, TN), lambda m, j: (m // tps, 0, j)),
        ],
        out_specs=[pl.BlockSpec((tm, TN), lambda m, j: (m, j)),
                   pl.BlockSpec((nseq, CONV_K - 1, TN), lambda m, j: (m, 0, j))],
        out_shape=[jax.ShapeDtypeStruct((t, CONV_DIM), F32),
                   jax.ShapeDtypeStruct(((t // tm) * nseq, CONV_K - 1, CONV_DIM), F32)],
        scratch_shapes=scratch,
        compiler_params=pltpu.CompilerParams(
            dimension_semantics=("arbitrary", "arbitrary"), vmem_limit_bytes=VMEM_LIMIT),
        name="in_proj_xc",
    )(h, w_t, conv_w, conv_b, conv_state)
    tails = tails.reshape(t // (tm * tps), tps, nseq, CONV_K - 1, CONV_DIM)[:, -1]
    return xc, tails.reshape(conv_state.shape)


def _short_body(h_ref, *refs, tm, tn, nseq, tps, shift):
    w_refs, (cw_ref, st_ref, y_ref, ns_ref, wbf_ref, pad_ref), carry = refs[:8], refs[8:14], refs[14:]
    m = pl.program_id(1)
    lt = tm // nseq

    @pl.when(m == 0)
    def _():
        for k in range(4):
            a_ref, b_ref = w_refs[2 * k], w_refs[2 * k + 1]
            wbf_ref[k] = jnp.concatenate([a_ref[shift:, :], b_ref[...]], axis=0).astype(BF16)

    _load_history(pad_ref, st_ref, carry, 0, first=(m % tps) == 0, hist=SHORT_K - 1, tps=tps)
    h = h_ref[...]
    band = lambda k: lax.dot_general(h, wbf_ref[k], NT_DIMS, preferred_element_type=F32)
    c = band(2)
    v = band(3)
    pad_ref[:, SUBLANES:SUBLANES + lt, :] = (c * v).reshape(nseq, lt, tn)
    conv = _causal_conv(pad_ref, cw_ref, SHORT_K)
    b = band(1)
    z = band(0)
    y = b * conv.reshape(tm, tn) * _silu(z)
    y_ref[...] = y.astype(y_ref.dtype)
    ns_ref[...] = pad_ref[:, lt + SUBLANES - (SHORT_K - 1):lt + SUBLANES, :]
    if tps > 1:
        carry[0][0] = pad_ref[:, lt:lt + SUBLANES, :]


def _short(h, w_t, conv_w, state, *, tm, tn, nseq, tps):
    t = h.shape[0]
    nj = D_SHORT // tn
    lt = tm // nseq
    base = (MAIN_VALID // tn) * tn
    shift = MAIN_VALID - base
    assert shift % SUBLANES == 0 and tn % shift == 0 and base % shift == 0
    scratch = [pltpu.VMEM((4, tn, D_MODEL), BF16), pltpu.VMEM((nseq, SUBLANES + lt, tn), F32)]
    if tps > 1:
        scratch.append(pltpu.VMEM((1, nseq, SUBLANES, tn), F32))
    w_specs = []
    for k in range(4):
        r0 = base + k * D_SHORT
        w_specs.append(pl.BlockSpec((tn, D_MODEL), lambda j, m, r0=r0: (r0 // tn + j, 0)))
        w_specs.append(pl.BlockSpec((shift, D_MODEL), lambda j, m, r0=r0: ((r0 + tn * (j + 1)) // shift, 0)))

    y, tails = pl.pallas_call(
        functools.partial(_short_body, tm=tm, tn=tn, nseq=nseq, tps=tps, shift=shift),
        grid=(nj, t // tm),
        in_specs=[
            pl.BlockSpec((tm, D_MODEL), lambda j, m: (m, 0)),
            *w_specs,
            pl.BlockSpec((SHORT_K, tn), lambda j, m: (0, j)),
            pl.BlockSpec((nseq, SHORT_K - 1, tn), lambda j, m: (m // tps, 0, j)),
        ],
        out_specs=[pl.BlockSpec((tm, tn), lambda j, m: (m, j)),
                   pl.BlockSpec((nseq, SHORT_K - 1, tn), lambda j, m: (m, 0, j))],
        out_shape=[jax.ShapeDtypeStruct((t, D_SHORT), BF16),
                   jax.ShapeDtypeStruct(((t // tm) * nseq, SHORT_K - 1, D_SHORT), F32)],
        scratch_shapes=scratch,
        compiler_params=pltpu.CompilerParams(
            dimension_semantics=("arbitrary", "arbitrary"), vmem_limit_bytes=VMEM_LIMIT),
        name="short_conv",
    )(h, *([w_t] * 8), conv_w, state)
    tails = tails.reshape(t // (tm * tps), tps, nseq, SHORT_K - 1, D_SHORT)[:, -1]
    return y, tails.reshape(state.shape)


def _dot_split(x, w, pieces):
    out = None
    r = x
    for _ in range(pieces):
        p = r.astype(BF16)
        d = jnp.dot(p, w, preferred_element_type=F32)
        out = d if out is None else out + d
        r = r - p.astype(F32)
    return out


def _prefix_sum_rows(x, period):
    t = lax.broadcasted_iota(jnp.int32, x.shape, 0) % period
    k = 1
    while k < period:
        x = x + jnp.where(t >= k, pltpu.roll(x, k, axis=0), 0.0)
        k *= 2
    return x


def _gated_norm_store(y, zg_ref, gnw_ref, y_ref):
    y = y * zg_ref[...].astype(F32)
    for g in range(GROUPS):
        cs = slice(g * GROUP_W, (g + 1) * GROUP_W)
        blk = y[:, cs]
        ms = jnp.mean(blk * blk, axis=-1, keepdims=True)
        y_ref[:, cs] = (blk * lax.rsqrt(ms + EPS) * gnw_ref[:, cs]).astype(y_ref.dtype)


def _ssd_chunk_body(zg_ref, xc_ref, dt_ref, h0_ref, alr_ref, expand_ref, dexp_ref, gnw_ref, wsrc_ref,
                    y_ref, hout_ref, wdst_ref, hst_ref, ysc_ref, *, lc, cps, nsteps):
    c = pl.program_id(1)
    wdst_ref[...] = wsrc_ref[...].astype(BF16)

    @pl.when(c == 0)
    def _():
        hst_ref[...] = h0_ref[0].T

    row_i = lax.broadcasted_iota(jnp.int32, (lc, lc), 0)
    col_i = lax.broadcasted_iota(jnp.int32, (lc, lc), 1)
    causal = row_i >= col_i
    lane_lo = lax.broadcasted_iota(jnp.int32, (lc, LANES), 1) < HEAD_DIM

    a_row = -jnp.exp(alr_ref[...])

    for k in range(cps):
        tok = slice(k * lc, (k + 1) * lc)
        dt = dt_ref[tok, :]
        a_cum = _prefix_sum_rows(dt * a_row, lc)
        a_last = a_cum[lc - 1:lc, :]
        factors = jnp.concatenate([jnp.exp(a_cum), jnp.exp(a_last - a_cum) * dt], axis=0).astype(BF16)
        a2 = a_cum * LOG2E
        r_t = (a2 - jnp.log2(dt)).T

        for g in range(GROUPS):
            bg = xc_ref[tok, D_SSM + g * D_STATE:D_SSM + (g + 1) * D_STATE]
            cg = xc_ref[tok, D_SSM + GN + g * D_STATE:D_SSM + GN + (g + 1) * D_STATE].astype(BF16)
            cols = slice(g * GROUP_W, (g + 1) * GROUP_W)
            h_prev = hst_ref[:, cols]
            cb = lax.dot_general(cg, bg.astype(BF16), NT_DIMS, preferred_element_type=F32)
            y_off = jnp.dot(cg, h_prev.astype(BF16), preferred_element_type=F32)
            xs_g = xc_ref[tok, cols]
            both = jnp.dot(factors, expand_ref[:, cols], preferred_element_type=F32)
            e_exp, w_exp = both[:lc], both[lc:]
            for pr in range(HPG // 2):
                mats = []
                for hh in (g * HPG + 2 * pr, g * HPG + 2 * pr + 1):
                    seg = a2[:, hh:hh + 1] - r_t[hh:hh + 1, :]
                    dec = jnp.exp2(jnp.where(causal, seg, -jnp.inf))
                    mats.append((cb * dec).astype(BF16))
                xp = xs_g[:, pr * LANES:(pr + 1) * LANES].astype(BF16)
                rhs = jnp.concatenate([jnp.where(lane_lo, xp, 0), jnp.where(lane_lo, 0, xp)], axis=0)
                y_diag = jnp.dot(jnp.concatenate(mats, axis=1), rhs, preferred_element_type=F32)
                pc = slice(g * GROUP_W + pr * LANES, g * GROUP_W + (pr + 1) * LANES)
                ysc_ref[tok, pc] = y_diag + (y_off * e_exp)[:, pr * LANES:(pr + 1) * LANES]
            xw = (xs_g * w_exp).astype(BF16)
            s_t = jnp.dot(bg.T.astype(BF16), xw, preferred_element_type=F32)
            hst_ref[:, cols] = h_prev * e_exp[lc - 1:lc, :] + s_t

    y = ysc_ref[...] + xc_ref[:, 0:D_SSM] * dexp_ref[...]
    _gated_norm_store(y, zg_ref, gnw_ref, y_ref)

    @pl.when(c == nsteps - 1)
    def _():
        hout_ref[0] = hst_ref[...].T


def _ssd_short_body(zg_ref, xc_ref, dt_ref, h0_ref, alr_ref, expand_ref, exps_ref, dexp_ref, gnw_ref,
                    y_ref, hout_ref, ysc_ref, *, lc, spb):
    rows = spb * lc
    hl = HEADS * lc
    dt = dt_ref[...]
    a_cum = _prefix_sum_rows(dt * (-jnp.exp(alr_ref[...])), lc)
    a_last = a_cum.reshape(spb, lc, LANES)[:, lc - 1:lc, :]
    a_last_b = jnp.broadcast_to(a_last, (spb, lc, LANES)).reshape(rows, LANES)
    expand = expand_ref[...]
    e_exp = _dot_split(jnp.exp(a_cum), expand, 2)
    w_exp = _dot_split(jnp.exp(a_last_b - a_cum) * dt, expand, 2)
    cd_t = jnp.exp(a_last.reshape(spb, LANES)).T

    z2 = _dot_split(jnp.concatenate([a_cum, dt], axis=0), exps_ref[...], 3)
    zc, zd = z2[:rows], z2[rows:]
    trow = lax.broadcasted_iota(jnp.int32, (rows, hl), 0) % lc
    tsrc = lax.broadcasted_iota(jnp.int32, (rows, hl), 1) % lc
    diag = trow == tsrc

    def per_source(z):
        d = jnp.where(diag, z, 0.0).reshape(spb, lc, hl).sum(axis=1, keepdims=True)
        return jnp.broadcast_to(d, (spb, lc, hl)).reshape(rows, hl)

    dec = jnp.exp(jnp.where(trow >= tsrc, zc - per_source(zc), -jnp.inf)) * per_source(zd)

    pair_w = HPG * lc
    blockdiag = (lax.broadcasted_iota(jnp.int32, (pair_w, GROUP_W), 0) // lc
                 == lax.broadcasted_iota(jnp.int32, (pair_w, GROUP_W), 1) // HEAD_DIM)

    for q in range(spb):
        tok = slice(q * lc, (q + 1) * lc)
        for g in range(GROUPS):
            bg = xc_ref[tok, D_SSM + g * D_STATE:D_SSM + (g + 1) * D_STATE]
            cg = xc_ref[tok, D_SSM + GN + g * D_STATE:D_SSM + GN + (g + 1) * D_STATE].astype(BF16)
            cols = slice(g * GROUP_W, (g + 1) * GROUP_W)
            h_prev = h0_ref[q, cols, :]
            xs_g = xc_ref[tok, cols]
            b_tiled = jnp.concatenate([bg] * HPG, axis=0).astype(BF16)
            cb = lax.dot_general(cg, b_tiled, NT_DIMS, preferred_element_type=F32)
            m = (cb * dec[tok, g * pair_w:(g + 1) * pair_w]).astype(BF16)
            x_diag = jnp.where(blockdiag, jnp.concatenate([xs_g] * HPG, axis=0), 0.0).astype(BF16)
            y_diag = jnp.dot(m, x_diag, preferred_element_type=F32)
            y_off = lax.dot_general(cg, h_prev.astype(BF16), NT_DIMS, preferred_element_type=F32)
            ysc_ref[tok, cols] = y_diag + y_off * e_exp[tok, cols]
            xw_t = (xs_g * w_exp[tok, cols]).T.astype(BF16)
            s_g = jnp.dot(xw_t, bg.astype(BF16), preferred_element_type=F32)
            cd = jnp.concatenate(
                [jnp.broadcast_to(cd_t[g * HPG + jh:g * HPG + jh + 1, q:q + 1], (HEAD_DIM, D_STATE))
                 for jh in range(HPG)], axis=0)
            hout_ref[q, cols, :] = h_prev * cd + s_g

    y = ysc_ref[...] + xc_ref[:, 0:D_SSM] * dexp_ref[...]
    _gated_norm_store(y, zg_ref, gnw_ref, y_ref)


def _ssd_chunked(zg, xc, dt, h0, alr, expand, dexp, gnw, w_f32, *, nb, nc):
    t = zg.shape[0]
    lc = SSD_CHUNK
    cps = CHUNKS_PER_STEP
    nsteps = nc // cps
    rows = cps * lc
    wrows = w_f32.shape[0] // (nb * nsteps)
    assert nsteps * cps == nc and wrows * nb * nsteps == w_f32.shape[0] and wrows % BF16_SUBLANES == 0
    tok = lambda r, width: pl.BlockSpec((r, width), lambda b, c: (b * nsteps + c, 0))
    par = lambda r, width: pl.BlockSpec((r, width), lambda b, c: (0, 0))
    h_spec = pl.BlockSpec((1, D_SSM, D_STATE), lambda b, c: (b, 0, 0))
    return pl.pallas_call(
        functools.partial(_ssd_chunk_body, lc=lc, cps=cps, nsteps=nsteps),
        grid=(nb, nsteps),
        in_specs=[tok(rows, D_SSM), tok(rows, CONV_DIM), tok(rows, LANES), h_spec, par(1, LANES),
                  par(LANES, D_SSM), par(1, D_SSM), par(1, D_SSM), tok(wrows, w_f32.shape[1])],
        out_specs=[tok(rows, D_SSM), h_spec, tok(wrows, w_f32.shape[1])],
        out_shape=[jax.ShapeDtypeStruct((t, D_SSM), BF16),
                   jax.ShapeDtypeStruct((nb, D_SSM, D_STATE), F32),
                   jax.ShapeDtypeStruct(w_f32.shape, BF16)],
        scratch_shapes=[pltpu.VMEM((D_STATE, D_SSM), F32), pltpu.VMEM((rows, D_SSM), F32)],
        compiler_params=pltpu.CompilerParams(
            dimension_semantics=("arbitrary", "arbitrary"), vmem_limit_bytes=VMEM_LIMIT),
        name="ssd_scan",
    )(zg, xc, dt, h0, alr, expand, dexp, gnw, w_f32)


def _ssd_short(zg, xc, dt, h0, alr, expand, exps, dexp, gnw, *, nb, lc):
    t = zg.shape[0]
    spb = SHORT_SEQS_PER_STEP
    rows = spb * lc
    tok = lambda width: pl.BlockSpec((rows, width), lambda b: (b, 0))
    par = lambda r, width: pl.BlockSpec((r, width), lambda b: (0, 0))
    h_spec = pl.BlockSpec((spb, D_SSM, D_STATE), lambda b: (b, 0, 0))
    return pl.pallas_call(
        functools.partial(_ssd_short_body, lc=lc, spb=spb),
        grid=(nb // spb,),
        in_specs=[tok(D_SSM), tok(CONV_DIM), tok(LANES), h_spec, par(1, LANES), par(LANES, D_SSM),
                  par(LANES, HEADS * lc), par(1, D_SSM), par(1, D_SSM)],
        out_specs=[tok(D_SSM), h_spec],
        out_shape=[jax.ShapeDtypeStruct((t, D_SSM), BF16),
                   jax.ShapeDtypeStruct((nb, D_SSM, D_STATE), F32)],
        scratch_shapes=[pltpu.VMEM((rows, D_SSM), F32)],
        compiler_params=pltpu.CompilerParams(
            dimension_semantics=("arbitrary",), vmem_limit_bytes=VMEM_LIMIT),
        name="ssd_scan_short",
    )(zg, xc, dt, h0, alr, expand, exps, dexp, gnw)


def _out_body(ys_ref, yc_ref, x_ref, w1_ref, w2_ref, fnw_ref, o_ref):
    acc = jnp.dot(ys_ref[...], w1_ref[...], preferred_element_type=F32)
    acc = acc + jnp.dot(yc_ref[...], w2_ref[...], preferred_element_type=F32)
    r = x_ref[...] + acc
    var = jnp.mean(r * r, axis=-1, keepdims=True)
    o_ref[...] = r * lax.rsqrt(var + EPS) * fnw_ref[...]


def _out(ys, yc, x, w_o, fnw, *, tm):
    t = x.shape[0]
    row = lambda width: pl.BlockSpec((tm, width), lambda m: (m, 0))
    return pl.pallas_call(
        _out_body,
        grid=(t // tm,),
        in_specs=[row(D_SSM), row(D_SHORT), row(D_MODEL),
                  pl.BlockSpec((D_SSM, D_MODEL), lambda m: (0, 0), pipeline_mode=pl.Buffered(1)),
                  pl.BlockSpec((D_SHORT, D_MODEL), lambda m: (1, 0), pipeline_mode=pl.Buffered(1)),
                  pl.BlockSpec((1, D_MODEL), lambda m: (0, 0))],
        out_specs=row(D_MODEL),
        out_shape=jax.ShapeDtypeStruct((t, D_MODEL), F32),
        compiler_params=pltpu.CompilerParams(
            dimension_semantics=("arbitrary",), vmem_limit_bytes=VMEM_LIMIT),
        name="out_proj",
    )(ys, yc, x, w_o, w_o, fnw)


def _projections(x2d, conv_state, short_state, p, *, seqlen):
    tm = ROW_TILE
    nseq = max(tm // seqlen, 1)
    tps = max(seqlen // tm, 1)
    assert x2d.shape[0] % tm == 0 and (seqlen % tm == 0 or tm % seqlen == 0)
    h, zg, dt = _proj_z(x2d, p["nw"], p["w_t"], p["dtb"], tm=tm)
    xc, conv_new = _proj_xc(h, p["w_t"], p["conv_w"], p["conv_b"], conv_state,
                            tm=tm, nseq=nseq, tps=tps)
    tm_s = min(SHORT_ROW_TILE, x2d.shape[0])
    assert seqlen % tm_s == 0 or tm_s % seqlen == 0
    y_c, short_new = _short(h, p["w_t"], p["conv_short_w"], short_state,
                            tm=tm_s, tn=SHORT_TN, nseq=max(tm_s // seqlen, 1), tps=max(seqlen // tm_s, 1))
    return zg, xc, dt, y_c, conv_new, short_new


def kernel(x_prompt, x_sample, state_ssm, state_conv_ssd, state_conv_short, norm_w, w_in, conv_ssd_w,
           conv_ssd_b, dt_bias, a_log, d_skip, ssd_norm_w, conv_short_w, w_out, final_norm_w):
    depth = norm_w.shape[0]
    assert depth == 1, "the output projection fuses the final rmsnorm, valid for a single layer"
    bp, lp, _ = x_prompt.shape
    bs, ls, _ = x_sample.shape
    assert lp % SSD_CHUNK == 0 and ls < SSD_CHUNK
    hp = x_prompt.reshape(bp * lp, D_MODEL)
    hs = x_sample.reshape(bs * ls, D_MODEL)
    pad_h = LANES - HEADS
    expand = (jnp.arange(LANES)[:, None] == jnp.arange(D_SSM)[None, :] // HEAD_DIM).astype(BF16)
    expand_pairs = (jnp.arange(LANES)[:, None] == jnp.arange(HEADS * ls)[None, :] // ls).astype(BF16)
    outs = [[] for _ in range(6)]
    for layer in range(depth):
        p = dict(
            nw=norm_w[layer][None, :],
            w_t=w_in[layer].T,
            conv_short_w=conv_short_w[layer],
            conv_w=conv_ssd_w[layer],
            conv_b=conv_ssd_b[layer][None, :],
            dtb=jnp.pad(dt_bias[layer], (0, pad_h))[None, :],
        )
        alr = jnp.pad(a_log[layer], (0, pad_h))[None, :]
        dexp = jnp.repeat(d_skip[layer], HEAD_DIM)[None, :]
        gnw = ssd_norm_w[layer][None, :]

        zg, xc, dt, yc, a2, a3 = _projections(
            hp, jnp.zeros((bp, CONV_K - 1, CONV_DIM), F32), jnp.zeros((bp, SHORT_K - 1, D_SHORT), F32), p,
            seqlen=lp)
        ys, a1, w_o = _ssd_chunked(zg, xc, dt, jnp.zeros((bp, D_SSM, D_STATE), F32), alr, expand, dexp,
                                   gnw, w_out[layer], nb=bp, nc=lp // SSD_CHUNK)
        zg2, xc2, dt2, yc2, s2, s3 = _projections(
            hs, state_conv_ssd[layer], state_conv_short[layer], p, seqlen=ls)
        ys2, s1 = _ssd_short(zg2, xc2, dt2, state_ssm[layer].reshape(bs, D_SSM, D_STATE), alr, expand,
                             expand_pairs, dexp, gnw, nb=bs, lc=ls)
        fw = final_norm_w[None, :]
        hp = _out(ys, yc, hp, w_o, fw, tm=OUT_ROW_TILE)
        hs = _out(ys2, yc2, hs, w_o, fw, tm=OUT_ROW_TILE)
        for lst, val in zip(outs, (a1.reshape(bp, HEADS, HEAD_DIM, D_STATE), a2, a3,
                                   s1.reshape(bs, HEADS, HEAD_DIM, D_STATE), s2, s3)):
            lst.append(val)
    return (hp.reshape(bp, lp, D_MODEL), hs.reshape(bs, ls, D_MODEL),
            *(jnp.stack(v) for v in outs))
```

```python
import functools

import jax
import jax.numpy as jnp
from jax import lax
from jax.experimental import pallas as pl
from jax.experimental.pallas import tpu as pltpu

F32 = jnp.float32
BF16 = jnp.bfloat16

D_MODEL = 2048
D_SSM = 2048
D_SHORT = 2048
HEADS = 32
HEAD_DIM = 64
D_STATE = 128
GROUPS = 4
HPG = HEADS // GROUPS
GROUP_W = HPG * HEAD_DIM
GN = GROUPS * D_STATE
CONV_DIM = D_SSM + 2 * GN
CONV_K = 4
SHORT_K = 3
SSD_CHUNK = 128
EPS = 1e-6
LANES = 128
SUBLANES = 8
BF16_SUBLANES = 16
VMEM_LIMIT = 56 * 1024 * 1024

ROW_TILE = 1024
TN = 1024
SHORT_ROW_TILE = 2048
SHORT_TN = 256
OUT_ROW_TILE = 512
NORM_ROWS = 32
NORM_UNROLL = 8
Z_TILES = D_SSM // TN
XC_TILES = CONV_DIM // TN
MAIN_VALID = D_SSM + CONV_DIM + HEADS
DT_BLOCK = (D_SSM + CONV_DIM) // LANES

NT_DIMS = (((1,), (1,)), ((), ()))
LOG2E = 1.4426950408889634
CHUNKS_PER_STEP = 2
SHORT_SEQS_PER_STEP = 8


def _silu(x):
    half = 0.5 * x
    return half + half * jnp.tanh(half)


def _softplus(x):
    return jnp.maximum(x, 0.0) + jnp.log1p(jnp.exp(-jnp.abs(x)))


def _proj_z_body(x_ref, nw_ref, w_ref, wdt_ref, dtb_ref, h_ref, zg_ref, dt_ref, *wbf, tm):
    j = pl.program_id(1)
    if wbf:
        @pl.when(pl.program_id(0) == 0)
        def _():
            wbf[0][j] = w_ref[...].astype(BF16)

    @pl.when(j == 0)
    def _():
        nw = nw_ref[...]

        def body(i, carry):
            r = pl.ds(pl.multiple_of(i * NORM_ROWS, NORM_ROWS), NORM_ROWS)
            xf = x_ref[r, :]
            var = jnp.mean(xf * xf, axis=-1, keepdims=True)
            h_ref[r, :] = (xf * lax.rsqrt(var + EPS) * nw).astype(BF16)
            return carry

        lax.fori_loop(0, tm // NORM_ROWS, body, 0, unroll=NORM_UNROLL)
        dt_raw = lax.dot_general(h_ref[...], wdt_ref[...].astype(BF16), NT_DIMS,
                                 preferred_element_type=F32)
        lane = lax.broadcasted_iota(jnp.int32, dt_raw.shape, 1)
        dt_ref[...] = jnp.where(lane < HEADS, _softplus(dt_raw + dtb_ref[...]), 0.0)

    w = wbf[0][j] if wbf else w_ref[...].astype(BF16)
    acc = lax.dot_general(h_ref[...], w, NT_DIMS, preferred_element_type=F32)
    zg_ref[...] = _silu(acc).astype(BF16)


def _proj_z(x, nw, w_t, dtb, *, tm):
    t = x.shape[0]
    if t // tm > 1:
        w_spec = pl.BlockSpec((TN, D_MODEL), lambda m, j: (jnp.where(m == 0, j, Z_TILES - 1), 0),
                              pipeline_mode=pl.Buffered(1))
        scratch = [pltpu.VMEM((Z_TILES, TN, D_MODEL), BF16)]
    else:
        w_spec = pl.BlockSpec((TN, D_MODEL), lambda m, j: (j, 0))
        scratch = []
    return pl.pallas_call(
        functools.partial(_proj_z_body, tm=tm),
        grid=(t // tm, Z_TILES),
        in_specs=[
            pl.BlockSpec((tm, D_MODEL), lambda m, j: (m, 0)),
            pl.BlockSpec((1, D_MODEL), lambda m, j: (0, 0)),
            w_spec,
            pl.BlockSpec((LANES, D_MODEL), lambda m, j: (DT_BLOCK, 0)),
            pl.BlockSpec((1, LANES), lambda m, j: (0, 0)),
        ],
        out_specs=[pl.BlockSpec((tm, D_MODEL), lambda m, j: (m, 0)),
                   pl.BlockSpec((tm, TN), lambda m, j: (m, j)),
                   pl.BlockSpec((tm, LANES), lambda m, j: (m, 0))],
        out_shape=[jax.ShapeDtypeStruct((t, D_MODEL), BF16),
                   jax.ShapeDtypeStruct((t, D_SSM), BF16),
                   jax.ShapeDtypeStruct((t, LANES), F32)],
        scratch_shapes=scratch,
        compiler_params=pltpu.CompilerParams(
            dimension_semantics=("arbitrary", "arbitrary"), vmem_limit_bytes=VMEM_LIMIT),
        name="in_proj_z",
    )(x, nw, w_t, w_t, dtb)


def _load_history(pad_ref, st_ref, carry, j, *, first, hist, tps):
    def from_state():
        pad_ref[:, 0:SUBLANES, :] = jnp.zeros((pad_ref.shape[0], SUBLANES, pad_ref.shape[2]), F32)
        pad_ref[:, SUBLANES - hist:SUBLANES, :] = st_ref[...]

    if tps > 1:
        (carry_ref,) = carry
        pl.when(first)(from_state)

        @pl.when(jnp.logical_not(first))
        def _():
            pad_ref[:, 0:SUBLANES, :] = carry_ref[j]
    else:
        from_state()


def _causal_conv(pad_ref, cw_ref, taps):
    xe = pad_ref[...]
    if taps == 4:
        s1 = pltpu.roll(xe, 1, axis=1)
        near = xe * cw_ref[3:4, :] + s1 * cw_ref[2:3, :]
        far = xe * cw_ref[1:2, :] + s1 * cw_ref[0:1, :]
        return near[:, SUBLANES:, :] + pltpu.roll(far, 2, axis=1)[:, SUBLANES:, :]
    conv = xe[:, SUBLANES:, :] * cw_ref[taps - 1:taps, :]
    for s in range(1, taps):
        shifted = pltpu.roll(xe, s, axis=1)[:, SUBLANES:, :]
        conv = conv + shifted * cw_ref[taps - 1 - s:taps - s, :]
    return conv


def _proj_xc_body(h_ref, w_ref, cw_ref, cb_ref, st_ref, xc_ref, tails_ref, pad_ref, *carry,
                  tm, nseq, tps):
    j = pl.program_id(1)
    lt = tm // nseq
    _load_history(pad_ref, st_ref, carry, j, first=(pl.program_id(0) % tps) == 0, hist=CONV_K - 1,
                  tps=tps)
    acc = lax.dot_general(h_ref[...], w_ref[...].astype(BF16), NT_DIMS, preferred_element_type=F32)
    pad_ref[:, SUBLANES:SUBLANES + lt, :] = acc.reshape(nseq, lt, TN)
    conv = _causal_conv(pad_ref, cw_ref, CONV_K)
    xc_ref[...] = _silu(conv + cb_ref[...]).reshape(tm, TN)
    tails_ref[...] = pad_ref[:, lt + SUBLANES - (CONV_K - 1):lt + SUBLANES, :]
    if tps > 1:
        carry[0][j] = pad_ref[:, lt:lt + SUBLANES, :]


def _proj_xc(h, w_t, conv_w, conv_b, conv_state, *, tm, nseq, tps):
    t = h.shape[0]
    lt = tm // nseq
    scratch = [pltpu.VMEM((nseq, SUBLANES + lt, TN), F32)]
    if tps > 1:
        scratch.append(pltpu.VMEM((XC_TILES, nseq, SUBLANES, TN), F32))
    xc, tails = pl.pallas_call(
        functools.partial(_proj_xc_body, tm=tm, nseq=nseq, tps=tps),
        grid=(t // tm, XC_TILES),
        in_specs=[
            pl.BlockSpec((tm, D_MODEL), lambda m, j: (m, 0)),
            pl.BlockSpec((TN, D_MODEL), lambda m, j: (Z_TILES + j, 0)),
            pl.BlockSpec((CONV_K, TN), lambda m, j: (0, j)),
            pl.BlockSpec((1, TN), lambda m, j: (0, j)),
            pl.BlockSpec((nseq, CONV_K - 1, TN), lambda m, j: (m // tps, 0, j)),
        ],
        out_specs=[pl.BlockSpec((tm, TN), lambda m, j: (m, j)),
                   pl.BlockSpec((nseq, CONV_K - 1, TN), lambda m, j: (m, 0, j))],
        out_shape=[jax.ShapeDtypeStruct((t, CONV_DIM), F32),
                   jax.ShapeDtypeStruct(((t // tm) * nseq, CONV_K - 1, CONV_DIM), F32)],
        scratch_shapes=scratch,
        compiler_params=pltpu.CompilerParams(
            dimension_semantics=("arbitrary", "arbitrary"), vmem_limit_bytes=VMEM_LIMIT),
        name="in_proj_xc",
    )(h, w_t, conv_w, conv_b, conv_state)
    tails = tails.reshape(t // (tm * tps), tps, nseq, CONV_K - 1, CONV_DIM)[:, -1]
    return xc, tails.reshape(conv_state.shape)


def _short_body(h_ref, *refs, tm, tn, nseq, tps, shift):
    w_refs, (cw_ref, st_ref, y_ref, ns_ref, wbf_ref, pad_ref), carry = refs[:8], refs[8:14], refs[14:]
    m = pl.program_id(1)
    lt = tm // nseq

    @pl.when(m == 0)
    def _():
        for k in range(4):
            a_ref, b_ref = w_refs[2 * k], w_refs[2 * k + 1]
            wbf_ref[k] = jnp.concatenate([a_ref[shift:, :], b_ref[...]], axis=0).astype(BF16)

    _load_history(pad_ref, st_ref, carry, 0, first=(m % tps) == 0, hist=SHORT_K - 1, tps=tps)
    h = h_ref[...]
    band = lambda k: lax.dot_general(h, wbf_ref[k], NT_DIMS, preferred_element_type=F32)
    c = band(2)
    v = band(3)
    pad_ref[:, SUBLANES:SUBLANES + lt, :] = (c * v).reshape(nseq, lt, tn)
    conv = _causal_conv(pad_ref, cw_ref, SHORT_K)
    b = band(1)
    z = band(0)
    y = b * conv.reshape(tm, tn) * _silu(z)
    y_ref[...] = y.astype(y_ref.dtype)
    ns_ref[...] = pad_ref[:, lt + SUBLANES - (SHORT_K - 1):lt + SUBLANES, :]
    if tps > 1:
        carry[0][0] = pad_ref[:, lt:lt + SUBLANES, :]


def _short(h, w_t, conv_w, state, *, tm, tn, nseq, tps):
    t = h.shape[0]
    nj = D_SHORT // tn
    lt = tm // nseq
    base = (MAIN_VALID // tn) * tn
    shift = MAIN_VALID - base
    assert shift % SUBLANES == 0 and tn % shift == 0 and base % shift == 0
    scratch = [pltpu.VMEM((4, tn, D_MODEL), BF16), pltpu.VMEM((nseq, SUBLANES + lt, tn), F32)]
    if tps > 1:
        scratch.append(pltpu.VMEM((1, nseq, SUBLANES, tn), F32))
    w_specs = []
    for k in range(4):
        r0 = base + k * D_SHORT
        w_specs.append(pl.BlockSpec((tn, D_MODEL), lambda j, m, r0=r0: (r0 // tn + j, 0)))
        w_specs.append(pl.BlockSpec((shift, D_MODEL), lambda j, m, r0=r0: ((r0 + tn * (j + 1)) // shift, 0)))

    y, tails = pl.pallas_call(
        functools.partial(_short_body, tm=tm, tn=tn, nseq=nseq, tps=tps, shift=shift),
        grid=(nj, t // tm),
        in_specs=[
            pl.BlockSpec((tm, D_MODEL), lambda j, m: (m, 0)),
            *w_specs,
            pl.BlockSpec((SHORT_K, tn), lambda j, m: (0, j)),
            pl.BlockSpec((nseq, SHORT_K - 1, tn), lambda j, m: (m // tps, 0, j)),
        ],
        out_specs=[pl.BlockSpec((tm, tn), lambda j, m: (m, j)),
                   pl.BlockSpec((nseq, SHORT_K - 1, tn), lambda j, m: (m, 0, j))],
        out_shape=[jax.ShapeDtypeStruct((t, D_SHORT), BF16),
                   jax.ShapeDtypeStruct(((t // tm) * nseq, SHORT_K - 1, D_SHORT), F32)],
        scratch_shapes=scratch,
        compiler_params=pltpu.CompilerParams(
            dimension_semantics=("arbitrary", "arbitrary"), vmem_limit_bytes=VMEM_LIMIT),
        name="short_conv",
    )(h, *([w_t] * 8), conv_w, state)
    tails = tails.reshape(t // (tm * tps), tps, nseq, SHORT_K - 1, D_SHORT)[:, -1]
    return y, tails.reshape(state.shape)


def _dot_split(x, w, pieces):
    out = None
    r = x
    for _ in range(pieces):
        p = r.astype(BF16)
        d = jnp.dot(p, w, preferred_element_type=F32)
        out = d if out is None else out + d
        r = r - p.astype(F32)
    return out


def _prefix_sum_rows(x, period):
    t = lax.broadcasted_iota(jnp.int32, x.shape, 0) % period
    k = 1
    while k < period:
        x = x + jnp.where(t >= k, pltpu.roll(x, k, axis=0), 0.0)
        k *= 2
    return x


def _gated_norm_store(y, zg_ref, gnw_ref, y_ref):
    y = y * zg_ref[...].astype(F32)
    for g in range(GROUPS):
        cs = slice(g * GROUP_W, (g + 1) * GROUP_W)
        blk = y[:, cs]
        ms = jnp.mean(blk * blk, axis=-1, keepdims=True)
        y_ref[:, cs] = (blk * lax.rsqrt(ms + EPS) * gnw_ref[:, cs]).astype(y_ref.dtype)


def _ssd_chunk_body(zg_ref, xc_ref, dt_ref, alr_ref, expand_ref, dexp_ref, gnw_ref, wsrc_ref,
                    y_ref, hout_ref, wdst_ref, hst_ref, ysc_ref, *, lc, cps, nsteps):
    c = pl.program_id(1)
    wdst_ref[...] = wsrc_ref[...].astype(BF16)

    @pl.when(c == 0)
    def _():
        hst_ref[...] = jnp.zeros(hst_ref.shape, F32)

    row_i = lax.broadcasted_iota(jnp.int32, (lc, lc), 0)
    col_i = lax.broadcasted_iota(jnp.int32, (lc, lc), 1)
    causal = row_i >= col_i
    lane_lo = lax.broadcasted_iota(jnp.int32, (lc, LANES), 1) < HEAD_DIM

    a_row = -jnp.exp(alr_ref[...])

    for k in range(cps):
        tok = slice(k * lc, (k + 1) * lc)
        dt = dt_ref[tok, :]
        a_cum = _prefix_sum_rows(dt * a_row, lc)
        a_last = a_cum[lc - 1:lc, :]
        factors = jnp.concatenate([jnp.exp(a_cum), jnp.exp(a_last - a_cum) * dt], axis=0).astype(BF16)
        a2 = a_cum * LOG2E
        r_t = (a2 - jnp.log2(dt)).T

        for g in range(GROUPS):
            bg = xc_ref[tok, D_SSM + g * D_STATE:D_SSM + (g + 1) * D_STATE]
            cg = xc_ref[tok, D_SSM + GN + g * D_STATE:D_SSM + GN + (g + 1) * D_STATE].astype(BF16)
            cols = slice(g * GROUP_W, (g + 1) * GROUP_W)
            h_prev = hst_ref[:, cols]
            cb = lax.dot_general(cg, bg.astype(BF16), NT_DIMS, preferred_element_type=F32)
            y_off = jnp.dot(cg, h_prev.astype(BF16), preferred_element_type=F32)
            xs_g = xc_ref[tok, cols]
            both = jnp.dot(factors, expand_ref[:, cols], preferred_element_type=F32)
            e_exp, w_exp = both[:lc], both[lc:]
            for pr in range(HPG // 2):
                mats = []
                for hh in (g * HPG + 2 * pr, g * HPG + 2 * pr + 1):
                    seg = a2[:, hh:hh + 1] - r_t[hh:hh + 1, :]
                    dec = jnp.exp2(jnp.where(causal, seg, -jnp.inf))
                    mats.append((cb * dec).astype(BF16))
                xp = xs_g[:, pr * LANES:(pr + 1) * LANES].astype(BF16)
                rhs = jnp.concatenate([jnp.where(lane_lo, xp, 0), jnp.where(lane_lo, 0, xp)], axis=0)
                y_diag = jnp.dot(jnp.concatenate(mats, axis=1), rhs, preferred_element_type=F32)
                pc = slice(g * GROUP_W + pr * LANES, g * GROUP_W + (pr + 1) * LANES)
                ysc_ref[tok, pc] = y_diag + (y_off * e_exp)[:, pr * LANES:(pr + 1) * LANES]
            xw = (xs_g * w_exp).astype(BF16)
            s_t = jnp.dot(bg.T.astype(BF16), xw, preferred_element_type=F32)
            hst_ref[:, cols] = h_prev * e_exp[lc - 1:lc, :] + s_t

    y = ysc_ref[...] + xc_ref[:, 0:D_SSM] * dexp_ref[...]
    _gated_norm_store(y, zg_ref, gnw_ref, y_ref)

    @pl.when(c == nsteps - 1)
    def _():
        hout_ref[0] = hst_ref[...].T


def _ssd_short_body(zg_ref, xc_ref, dt_ref, h0_ref, alr_ref, expand_ref, exps_ref, dexp_ref, gnw_ref,
                    y_ref, hout_ref, ysc_ref, *, lc, spb):
    rows = spb * lc
    hl = HEADS * lc
    dt = dt_ref[...]
    a_cum = _prefix_sum_rows(dt * (-jnp.exp(alr_ref[...])), lc)
    a_last = a_cum.reshape(spb, lc, LANES)[:, lc - 1:lc, :]
    a_last_b = jnp.broadcast_to(a_last, (spb, lc, LANES)).reshape(rows, LANES)
    expand = expand_ref[...]
    e_exp = _dot_split(jnp.exp(a_cum), expand, 2)
    w_exp = _dot_split(jnp.exp(a_last_b - a_cum) * dt, expand, 2)
    cd_t = jnp.exp(a_last.reshape(spb, LANES)).T

    z2 = _dot_split(jnp.concatenate([a_cum, dt], axis=0), exps_ref[...], 3)
    zc, zd = z2[:rows], z2[rows:]
    trow = lax.broadcasted_iota(jnp.int32, (rows, hl), 0) % lc
    tsrc = lax.broadcasted_iota(jnp.int32, (rows, hl), 1) % lc
    diag = trow == tsrc

    def per_source(z):
        d = jnp.where(diag, z, 0.0).reshape(spb, lc, hl).sum(axis=1, keepdims=True)
        return jnp.broadcast_to(d, (spb, lc, hl)).reshape(rows, hl)

    dec = jnp.exp(jnp.where(trow >= tsrc, zc - per_source(zc), -jnp.inf)) * per_source(zd)

    pair_w = HPG * lc
    blockdiag = (lax.broadcasted_iota(jnp.int32, (pair_w, GROUP_W), 0) // lc
                 == lax.broadcasted_iota(jnp.int32, (pair_w, GROUP_W), 1) // HEAD_DIM)

    for q in range(spb):
        tok = slice(q * lc, (q + 1) * lc)
        for g in range(GROUPS):
            bg = xc_ref[tok, D_SSM + g * D_STATE:D_SSM + (g + 1) * D_STATE]
            cg = xc_ref[tok, D_SSM + GN + g * D_STATE:D_SSM + GN + (g + 1) * D_STATE].astype(BF16)
            cols = slice(g * GROUP_W, (g + 1) * GROUP_W)
            h_prev = h0_ref[q, cols, :]
            xs_g = xc_ref[tok, cols]
            b_tiled = jnp.concatenate([bg] * HPG, axis=0).astype(BF16)
            cb = lax.dot_general(cg, b_tiled, NT_DIMS, preferred_element_type=F32)
            m = (cb * dec[tok, g * pair_w:(g + 1) * pair_w]).astype(BF16)
            x_diag = jnp.where(blockdiag, jnp.concatenate([xs_g] * HPG, axis=0), 0.0).astype(BF16)
            y_diag = jnp.dot(m, x_diag, preferred_element_type=F32)
            y_off = lax.dot_general(cg, h_prev.astype(BF16), NT_DIMS, preferred_element_type=F32)
            ysc_ref[tok, cols] = y_diag + y_off * e_exp[tok, cols]
            xw_t = (xs_g * w_exp[tok, cols]).T.astype(BF16)
            s_g = jnp.dot(xw_t, bg.astype(BF16), preferred_element_type=F32)
            cd = jnp.concatenate(
                [jnp.broadcast_to(cd_t[g * HPG + jh:g * HPG + jh + 1, q:q + 1], (HEAD_DIM, D_STATE))
                 for jh in range(HPG)], axis=0)
            hout_ref[q, cols, :] = h_prev * cd + s_g

    y = ysc_ref[...] + xc_ref[:, 0:D_SSM] * dexp_ref[...]
    _gated_norm_store(y, zg_ref, gnw_ref, y_ref)


def _ssd_chunked(zg, xc, dt, alr, expand, dexp, gnw, w_f32, *, nb, nc):
    t = zg.shape[0]
    lc = SSD_CHUNK
    cps = CHUNKS_PER_STEP
    nsteps = nc // cps
    rows = cps * lc
    wrows = w_f32.shape[0] // (nb * nsteps)
    assert nsteps * cps == nc and wrows * nb * nsteps == w_f32.shape[0] and wrows % BF16_SUBLANES == 0
    tok = lambda r, width: pl.BlockSpec((r, width), lambda b, c: (b * nsteps + c, 0))
    par = lambda r, width: pl.BlockSpec((r, width), lambda b, c: (0, 0))
    h_spec = pl.BlockSpec((1, D_SSM, D_STATE), lambda b, c: (b, 0, 0))
    return pl.pallas_call(
        functools.partial(_ssd_chunk_body, lc=lc, cps=cps, nsteps=nsteps),
        grid=(nb, nsteps),
        in_specs=[tok(rows, D_SSM), tok(rows, CONV_DIM), tok(rows, LANES), par(1, LANES),
                  par(LANES, D_SSM), par(1, D_SSM), par(1, D_SSM), tok(wrows, w_f32.shape[1])],
        out_specs=[tok(rows, D_SSM), h_spec, tok(wrows, w_f32.shape[1])],
        out_shape=[jax.ShapeDtypeStruct((t, D_SSM), BF16),
                   jax.ShapeDtypeStruct((nb, D_SSM, D_STATE), F32),
                   jax.ShapeDtypeStruct(w_f32.shape, BF16)],
        scratch_shapes=[pltpu.VMEM((D_STATE, D_SSM), F32), pltpu.VMEM((rows, D_SSM), F32)],
        compiler_params=pltpu.CompilerParams(
            dimension_semantics=("arbitrary", "arbitrary"), vmem_limit_bytes=VMEM_LIMIT),
        name="ssd_scan",
    )(zg, xc, dt, alr, expand, dexp, gnw, w_f32)


def _ssd_short(zg, xc, dt, h0, alr, expand, exps, dexp, gnw, *, nb, lc):
    t = zg.shape[0]
    spb = SHORT_SEQS_PER_STEP
    rows = spb * lc
    tok = lambda width: pl.BlockSpec((rows, width), lambda b: (b, 0))
    par = lambda r, width: pl.BlockSpec((r, width), lambda b: (0, 0))
    h_spec = pl.BlockSpec((spb, D_SSM, D_STATE), lambda b: (b, 0, 0))
    return pl.pallas_call(
        functools.partial(_ssd_short_body, lc=lc, spb=spb),
        grid=(nb // spb,),
        in_specs=[tok(D_SSM), tok(CONV_DIM), tok(LANES), h_spec, par(1, LANES), par(LANES, D_SSM),
                  par(LANES, HEADS * lc), par(1, D_SSM), par(1, D_SSM)],
        out_specs=[tok(D_SSM), h_spec],
        out_shape=[jax.ShapeDtypeStruct((t, D_SSM), BF16),
                   jax.ShapeDtypeStruct((nb, D_SSM, D_STATE), F32)],
        scratch_shapes=[pltpu.VMEM((rows, D_SSM), F32)],
        compiler_params=pltpu.CompilerParams(
            dimension_semantics=("arbitrary",), vmem_limit_bytes=VMEM_LIMIT),
        name="ssd_scan_short",
    )(zg, xc, dt, h0, alr, expand, exps, dexp, gnw)


def _out_body(ys_ref, yc_ref, x_ref, w1_ref, w2_ref, fnw_ref, o_ref):
    acc = jnp.dot(ys_ref[...], w1_ref[...], preferred_element_type=F32)
    acc = acc + jnp.dot(yc_ref[...], w2_ref[...], preferred_element_type=F32)
    r = x_ref[...] + acc
    var = jnp.mean(r * r, axis=-1, keepdims=True)
    o_ref[...] = r * lax.rsqrt(var + EPS) * fnw_ref[...]


def _out(ys, yc, x, w_o, fnw, *, tm):
    t = x.shape[0]
    row = lambda width: pl.BlockSpec((tm, width), lambda m: (m, 0))
    return pl.pallas_call(
        _out_body,
        grid=(t // tm,),
        in_specs=[row(D_SSM), row(D_SHORT), row(D_MODEL),
                  pl.BlockSpec((D_SSM, D_MODEL), lambda m: (0, 0), pipeline_mode=pl.Buffered(1)),
                  pl.BlockSpec((D_SHORT, D_MODEL), lambda m: (1, 0), pipeline_mode=pl.Buffered(1)),
                  pl.BlockSpec((1, D_MODEL), lambda m: (0, 0))],
        out_specs=row(D_MODEL),
        out_shape=jax.ShapeDtypeStruct((t, D_MODEL), F32),
        compiler_params=pltpu.CompilerParams(
            dimension_semantics=("arbitrary",), vmem_limit_bytes=VMEM_LIMIT),
        name="out_proj",
    )(ys, yc, x, w_o, w_o, fnw)


def _projections(x2d, conv_state, short_state, p, *, seqlen):
    tm = ROW_TILE
    nseq = max(tm // seqlen, 1)
    tps = max(seqlen // tm, 1)
    assert x2d.shape[0] % tm == 0 and (seqlen % tm == 0 or tm % seqlen == 0)
    h, zg, dt = _proj_z(x2d, p["nw"], p["w_t"], p["dtb"], tm=tm)
    xc, conv_new = _proj_xc(h, p["w_t"], p["conv_w"], p["conv_b"], conv_state,
                            tm=tm, nseq=nseq, tps=tps)
    tm_s = min(SHORT_ROW_TILE, x2d.shape[0])
    assert seqlen % tm_s == 0 or tm_s % seqlen == 0
    y_c, short_new = _short(h, p["w_t"], p["conv_short_w"], short_state,
                            tm=tm_s, tn=SHORT_TN, nseq=max(tm_s // seqlen, 1), tps=max(seqlen // tm_s, 1))
    return zg, xc, dt, y_c, conv_new, short_new


def kernel(x_prompt, x_sample, state_ssm, state_conv_ssd, state_conv_short, norm_w, w_in, conv_ssd_w,
           conv_ssd_b, dt_bias, a_log, d_skip, ssd_norm_w, conv_short_w, w_out, final_norm_w):
    depth = norm_w.shape[0]
    assert depth == 1, "the output projection fuses the final rmsnorm, valid for a single layer"
    bp, lp, _ = x_prompt.shape
    bs, ls, _ = x_sample.shape
    assert lp % SSD_CHUNK == 0 and ls < SSD_CHUNK
    hp = x_prompt.reshape(bp * lp, D_MODEL)
    hs = x_sample.reshape(bs * ls, D_MODEL)
    pad_h = LANES - HEADS
    expand = (jnp.arange(LANES)[:, None] == jnp.arange(D_SSM)[None, :] // HEAD_DIM).astype(BF16)
    expand_pairs = (jnp.arange(LANES)[:, None] == jnp.arange(HEADS * ls)[None, :] // ls).astype(BF16)
    outs = [[] for _ in range(6)]
    for layer in range(depth):
        p = dict(
            nw=norm_w[layer][None, :],
            w_t=w_in[layer].T,
            conv_short_w=conv_short_w[layer],
            conv_w=conv_ssd_w[layer],
            conv_b=conv_ssd_b[layer][None, :],
            dtb=jnp.pad(dt_bias[layer], (0, pad_h))[None, :],
        )
        alr = jnp.pad(a_log[layer], (0, pad_h))[None, :]
        dexp = jnp.repeat(d_skip[layer], HEAD_DIM)[None, :]
        gnw = ssd_norm_w[layer][None, :]

        zg, xc, dt, yc, a2, a3 = _projections(
            hp, jnp.zeros((bp, CONV_K - 1, CONV_DIM), F32), jnp.zeros((bp, SHORT_K - 1, D_SHORT), F32), p,
            seqlen=lp)
        ys, a1, w_o = _ssd_chunked(zg, xc, dt, alr, expand, dexp, gnw, w_out[layer],
                                   nb=bp, nc=lp // SSD_CHUNK)
        zg2, xc2, dt2, yc2, s2, s3 = _projections(
            hs, state_conv_ssd[layer], state_conv_short[layer], p, seqlen=ls)
        ys2, s1 = _ssd_short(zg2, xc2, dt2, state_ssm[layer].reshape(bs, D_SSM, D_STATE), alr, expand,
                             expand_pairs, dexp, gnw, nb=bs, lc=ls)
        fw = final_norm_w[None, :]
        hp = _out(ys, yc, hp, w_o, fw, tm=OUT_ROW_TILE)
        hs = _out(ys2, yc2, hs, w_o, fw, tm=OUT_ROW_TILE)
        for lst, val in zip(outs, (a1.reshape(bp, HEADS, HEAD_DIM, D_STATE), a2, a3,
                                   s1.reshape(bs, HEADS, HEAD_DIM, D_STATE), s2, s3)):
            lst.append(val)
    return (hp.reshape(bp, lp, D_MODEL), hs.reshape(bs, ls, D_MODEL),
            *(jnp.stack(v) for v in outs))
```

```python
import functools

import jax
import jax.numpy as jnp
from jax import lax
from jax.experimental import pallas as pl
from jax.experimental.pallas import tpu as pltpu

F32 = jnp.float32
BF16 = jnp.bfloat16

D_MODEL = 2048
D_SSM = 2048
D_SHORT = 2048
HEADS = 32
HEAD_DIM = 64
D_STATE = 128
GROUPS = 4
HPG = HEADS // GROUPS
GROUP_W = HPG * HEAD_DIM
GN = GROUPS * D_STATE
CONV_DIM = D_SSM + 2 * GN
CONV_K = 4
SHORT_K = 3
SSD_CHUNK = 128
EPS = 1e-6
LANES = 128
SUBLANES = 8
BF16_SUBLANES = 16
VMEM_LIMIT = 56 * 1024 * 1024

ROW_TILE = 1024
TN = 1024
SHORT_ROW_TILE = 2048
SHORT_TN = 256
OUT_ROW_TILE = 512
NORM_ROWS = 32
NORM_UNROLL = 8
Z_TILES = D_SSM // TN
XC_TILES = CONV_DIM // TN
MAIN_VALID = D_SSM + CONV_DIM + HEADS
DT_BLOCK = (D_SSM + CONV_DIM) // LANES

NT_DIMS = (((1,), (1,)), ((), ()))
LOG2E = 1.4426950408889634
CHUNKS_PER_STEP = 2
SHORT_SEQS_PER_STEP = 8


def _silu(x):
    half = 0.5 * x
    return half + half * jnp.tanh(half)


def _softplus(x):
    return jnp.maximum(x, 0.0) + jnp.log1p(jnp.exp(-jnp.abs(x)))


def _proj_z_body(x_ref, nw_ref, w_ref, wdt_ref, dtb_ref, h_ref, zg_ref, dt_ref, *wbf, tm):
    j = pl.program_id(1)
    if wbf:
        @pl.when(pl.program_id(0) == 0)
        def _():
            wbf[0][j] = w_ref[...].astype(BF16)

    @pl.when(j == 0)
    def _():
        nw = nw_ref[...]

        def body(i, carry):
            r = pl.ds(pl.multiple_of(i * NORM_ROWS, NORM_ROWS), NORM_ROWS)
            xf = x_ref[r, :]
            var = jnp.mean(xf * xf, axis=-1, keepdims=True)
            h_ref[r, :] = (xf * lax.rsqrt(var + EPS) * nw).astype(BF16)
            return carry

        lax.fori_loop(0, tm // NORM_ROWS, body, 0, unroll=NORM_UNROLL)
        dt_raw = lax.dot_general(h_ref[...], wdt_ref[...].astype(BF16), NT_DIMS,
                                 preferred_element_type=F32)
        lane = lax.broadcasted_iota(jnp.int32, dt_raw.shape, 1)
        dt_ref[...] = jnp.where(lane < HEADS, _softplus(dt_raw + dtb_ref[...]), 0.0)

    w = wbf[0][j] if wbf else w_ref[...].astype(BF16)
    acc = lax.dot_general(h_ref[...], w, NT_DIMS, preferred_element_type=F32)
    zg_ref[...] = _silu(acc).astype(BF16)


def _proj_z(x, nw, w_t, dtb, *, tm):
    t = x.shape[0]
    if t // tm > 1:
        w_spec = pl.BlockSpec((TN, D_MODEL), lambda m, j: (jnp.where(m == 0, j, Z_TILES - 1), 0),
                              pipeline_mode=pl.Buffered(1))
        scratch = [pltpu.VMEM((Z_TILES, TN, D_MODEL), BF16)]
    else:
        w_spec = pl.BlockSpec((TN, D_MODEL), lambda m, j: (j, 0))
        scratch = []
    return pl.pallas_call(
        functools.partial(_proj_z_body, tm=tm),
        grid=(t // tm, Z_TILES),
        in_specs=[
            pl.BlockSpec((tm, D_MODEL), lambda m, j: (m, 0)),
            pl.BlockSpec((1, D_MODEL), lambda m, j: (0, 0)),
            w_spec,
            pl.BlockSpec((LANES, D_MODEL), lambda m, j: (DT_BLOCK, 0)),
            pl.BlockSpec((1, LANES), lambda m, j: (0, 0)),
        ],
        out_specs=[pl.BlockSpec((tm, D_MODEL), lambda m, j: (m, 0)),
                   pl.BlockSpec((tm, TN), lambda m, j: (m, j)),
                   pl.BlockSpec((tm, LANES), lambda m, j: (m, 0))],
        out_shape=[jax.ShapeDtypeStruct((t, D_MODEL), BF16),
                   jax.ShapeDtypeStruct((t, D_SSM), BF16),
                   jax.ShapeDtypeStruct((t, LANES), F32)],
        scratch_shapes=scratch,
        compiler_params=pltpu.CompilerParams(
            dimension_semantics=("arbitrary", "arbitrary"), vmem_limit_bytes=VMEM_LIMIT),
        name="in_proj_z",
    )(x, nw, w_t, w_t, dtb)


def _load_history(pad_ref, st_ref, carry, j, *, first, hist, tps, row_major=False):
    def from_state():
        pad_ref[:, 0:SUBLANES, :] = jnp.zeros((pad_ref.shape[0], SUBLANES, pad_ref.shape[2]), F32)
        if row_major:
            for k in range(hist):
                pad_ref[:, SUBLANES - hist + k, :] = st_ref[k]
        else:
            pad_ref[:, SUBLANES - hist:SUBLANES, :] = st_ref[...]

    if tps > 1:
        (carry_ref,) = carry
        pl.when(first)(from_state)

        @pl.when(jnp.logical_not(first))
        def _():
            pad_ref[:, 0:SUBLANES, :] = carry_ref[j]
    else:
        from_state()


def _causal_conv(pad_ref, cw_ref, taps):
    xe = pad_ref[...]
    if taps == 4:
        s1 = pltpu.roll(xe, 1, axis=1)
        near = xe * cw_ref[3:4, :] + s1 * cw_ref[2:3, :]
        far = xe * cw_ref[1:2, :] + s1 * cw_ref[0:1, :]
        return near[:, SUBLANES:, :] + pltpu.roll(far, 2, axis=1)[:, SUBLANES:, :]
    conv = xe[:, SUBLANES:, :] * cw_ref[taps - 1:taps, :]
    for s in range(1, taps):
        shifted = pltpu.roll(xe, s, axis=1)[:, SUBLANES:, :]
        conv = conv + shifted * cw_ref[taps - 1 - s:taps - s, :]
    return conv


def _proj_xc_body(h_ref, w_ref, cw_ref, cb_ref, st_ref, xc_ref, tails_ref, pad_ref, *carry,
                  tm, nseq, tps, row_major):
    j = pl.program_id(1)
    lt = tm // nseq
    _load_history(pad_ref, st_ref, carry, j, first=(pl.program_id(0) % tps) == 0, hist=CONV_K - 1,
                  tps=tps, row_major=row_major)
    acc = lax.dot_general(h_ref[...], w_ref[...].astype(BF16), NT_DIMS, preferred_element_type=F32)
    pad_ref[:, SUBLANES:SUBLANES + lt, :] = acc.reshape(nseq, lt, TN)
    conv = _causal_conv(pad_ref, cw_ref, CONV_K)
    xc_ref[...] = _silu(conv + cb_ref[...]).reshape(tm, TN)
    if row_major:
        for k in range(CONV_K - 1):
            tails_ref[k] = pad_ref[:, lt + SUBLANES - (CONV_K - 1) + k, :]
    else:
        tails_ref[...] = pad_ref[:, lt + SUBLANES - (CONV_K - 1):lt + SUBLANES, :]
    if tps > 1:
        carry[0][j] = pad_ref[:, lt:lt + SUBLANES, :]


def _proj_xc(h, w_t, conv_w, conv_b, conv_state, *, tm, nseq, tps):
    t = h.shape[0]
    lt = tm // nseq
    hist = CONV_K - 1
    row_major = nseq % SUBLANES == 0
    scratch = [pltpu.VMEM((nseq, SUBLANES + lt, TN), F32)]
    if tps > 1:
        scratch.append(pltpu.VMEM((XC_TILES, nseq, SUBLANES, TN), F32))
    ntail = (t // tm) * nseq
    if row_major:
        state = jnp.transpose(conv_state, (1, 0, 2))
        st_spec = pl.BlockSpec((hist, nseq, TN), lambda m, j: (0, m // tps, j))
        tail_spec = pl.BlockSpec((hist, nseq, TN), lambda m, j: (0, m, j))
        tail_shape = (hist, ntail, CONV_DIM)
    else:
        state = conv_state
        st_spec = pl.BlockSpec((nseq, hist, TN), lambda m, j: (m // tps, 0, j))
        tail_spec = pl.BlockSpec((nseq, hist, TN), lambda m, j: (m, 0, j))
        tail_shape = (ntail, hist, CONV_DIM)
    xc, tails = pl.pallas_call(
        functools.partial(_proj_xc_body, tm=tm, nseq=nseq, tps=tps, row_major=row_major),
        grid=(t // tm, XC_TILES),
        in_specs=[
            pl.BlockSpec((tm, D_MODEL), lambda m, j: (m, 0)),
            pl.BlockSpec((TN, D_MODEL), lambda m, j: (Z_TILES + j, 0)),
            pl.BlockSpec((CONV_K, TN), lambda m, j: (0, j)),
            pl.BlockSpec((1, TN), lambda m, j: (0, j)),
            st_spec,
        ],
        out_specs=[pl.BlockSpec((tm, TN), lambda m, j: (m, j)), tail_spec],
        out_shape=[jax.ShapeDtypeStruct((t, CONV_DIM), F32), jax.ShapeDtypeStruct(tail_shape, F32)],
        scratch_shapes=scratch,
        compiler_params=pltpu.CompilerParams(
            dimension_semantics=("arbitrary", "arbitrary"), vmem_limit_bytes=VMEM_LIMIT),
        name="in_proj_xc",
    )(h, w_t, conv_w, conv_b, state)
    if row_major:
        tails = tails.reshape(hist, t // (tm * tps), tps, nseq, CONV_DIM)[:, :, -1]
        return xc, jnp.transpose(tails.reshape(hist, -1, CONV_DIM), (1, 0, 2))
    tails = tails.reshape(t // (tm * tps), tps, nseq, hist, CONV_DIM)[:, -1]
    return xc, tails.reshape(conv_state.shape)


def _short_body(h_ref, *refs, tm, tn, nseq, tps, shift):
    w_refs, (cw_ref, st_ref, y_ref, ns_ref, wbf_ref, pad_ref), carry = refs[:8], refs[8:14], refs[14:]
    m = pl.program_id(1)
    lt = tm // nseq

    @pl.when(m == 0)
    def _():
        for k in range(4):
            a_ref, b_ref = w_refs[2 * k], w_refs[2 * k + 1]
            wbf_ref[k] = jnp.concatenate([a_ref[shift:, :], b_ref[...]], axis=0).astype(BF16)

    _load_history(pad_ref, st_ref, carry, 0, first=(m % tps) == 0, hist=SHORT_K - 1, tps=tps)
    h = h_ref[...]
    band = lambda k: lax.dot_general(h, wbf_ref[k], NT_DIMS, preferred_element_type=F32)
    c = band(2)
    v = band(3)
    pad_ref[:, SUBLANES:SUBLANES + lt, :] = (c * v).reshape(nseq, lt, tn)
    conv = _causal_conv(pad_ref, cw_ref, SHORT_K)
    b = band(1)
    z = band(0)
    y = b * conv.reshape(tm, tn) * _silu(z)
    y_ref[...] = y.astype(y_ref.dtype)
    ns_ref[...] = pad_ref[:, lt + SUBLANES - (SHORT_K - 1):lt + SUBLANES, :]
    if tps > 1:
        carry[0][0] = pad_ref[:, lt:lt + SUBLANES, :]


def _short(h, w_t, conv_w, state, *, tm, tn, nseq, tps):
    t = h.shape[0]
    nj = D_SHORT // tn
    lt = tm // nseq
    base = (MAIN_VALID // tn) * tn
    shift = MAIN_VALID - base
    assert shift % SUBLANES == 0 and tn % shift == 0 and base % shift == 0
    scratch = [pltpu.VMEM((4, tn, D_MODEL), BF16), pltpu.VMEM((nseq, SUBLANES + lt, tn), F32)]
    if tps > 1:
        scratch.append(pltpu.VMEM((1, nseq, SUBLANES, tn), F32))
    w_specs = []
    for k in range(4):
        r0 = base + k * D_SHORT
        w_specs.append(pl.BlockSpec((tn, D_MODEL), lambda j, m, r0=r0: (r0 // tn + j, 0)))
        w_specs.append(pl.BlockSpec((shift, D_MODEL), lambda j, m, r0=r0: ((r0 + tn * (j + 1)) // shift, 0)))

    y, tails = pl.pallas_call(
        functools.partial(_short_body, tm=tm, tn=tn, nseq=nseq, tps=tps, shift=shift),
        grid=(nj, t // tm),
        in_specs=[
            pl.BlockSpec((tm, D_MODEL), lambda j, m: (m, 0)),
            *w_specs,
            pl.BlockSpec((SHORT_K, tn), lambda j, m: (0, j)),
            pl.BlockSpec((nseq, SHORT_K - 1, tn), lambda j, m: (m // tps, 0, j)),
        ],
        out_specs=[pl.BlockSpec((tm, tn), lambda j, m: (m, j)),
                   pl.BlockSpec((nseq, SHORT_K - 1, tn), lambda j, m: (m, 0, j))],
        out_shape=[jax.ShapeDtypeStruct((t, D_SHORT), BF16),
                   jax.ShapeDtypeStruct(((t // tm) * nseq, SHORT_K - 1, D_SHORT), F32)],
        scratch_shapes=scratch,
        compiler_params=pltpu.CompilerParams(
            dimension_semantics=("arbitrary", "arbitrary"), vmem_limit_bytes=VMEM_LIMIT),
        name="short_conv",
    )(h, *([w_t] * 8), conv_w, state)
    tails = tails.reshape(t // (tm * tps), tps, nseq, SHORT_K - 1, D_SHORT)[:, -1]
    return y, tails.reshape(state.shape)


def _dot_split(x, w, pieces):
    out = None
    r = x
    for _ in range(pieces):
        p = r.astype(BF16)
        d = jnp.dot(p, w, preferred_element_type=F32)
        out = d if out is None else out + d
        r = r - p.astype(F32)
    return out


def _prefix_sum_rows(x, period):
    t = lax.broadcasted_iota(jnp.int32, x.shape, 0) % period
    k = 1
    while k < period:
        x = x + jnp.where(t >= k, pltpu.roll(x, k, axis=0), 0.0)
        k *= 2
    return x


def _gated_norm_store(y, zg_ref, gnw_ref, y_ref):
    y = y * zg_ref[...].astype(F32)
    for g in range(GROUPS):
        cs = slice(g * GROUP_W, (g + 1) * GROUP_W)
        blk = y[:, cs]
        ms = jnp.mean(blk * blk, axis=-1, keepdims=True)
        y_ref[:, cs] = (blk * lax.rsqrt(ms + EPS) * gnw_ref[:, cs]).astype(y_ref.dtype)


def _ssd_chunk_body(zg_ref, xc_ref, dt_ref, alr_ref, expand_ref, dexp_ref, gnw_ref, wsrc_ref,
                    y_ref, hout_ref, wdst_ref, hst_ref, ysc_ref, *, lc, cps, nsteps):
    c = pl.program_id(1)
    wdst_ref[...] = wsrc_ref[...].astype(BF16)

    @pl.when(c == 0)
    def _():
        hst_ref[...] = jnp.zeros(hst_ref.shape, F32)

    row_i = lax.broadcasted_iota(jnp.int32, (lc, lc), 0)
    col_i = lax.broadcasted_iota(jnp.int32, (lc, lc), 1)
    causal = row_i >= col_i
    lane_lo = lax.broadcasted_iota(jnp.int32, (lc, LANES), 1) < HEAD_DIM

    a_row = -jnp.exp(alr_ref[...])

    for k in range(cps):
        tok = slice(k * lc, (k + 1) * lc)
        dt = dt_ref[tok, :]
        a_cum = _prefix_sum_rows(dt * a_row, lc)
        a_last = a_cum[lc - 1:lc, :]
        factors = jnp.concatenate([jnp.exp(a_cum), jnp.exp(a_last - a_cum) * dt], axis=0).astype(BF16)
        a2 = a_cum * LOG2E
        r_t = (a2 - jnp.log2(dt)).T

        for g in range(GROUPS):
            bg = xc_ref[tok, D_SSM + g * D_STATE:D_SSM + (g + 1) * D_STATE]
            cg = xc_ref[tok, D_SSM + GN + g * D_STATE:D_SSM + GN + (g + 1) * D_STATE].astype(BF16)
            cols = slice(g * GROUP_W, (g + 1) * GROUP_W)
            h_prev = hst_ref[:, cols]
            cb = lax.dot_general(cg, bg.astype(BF16), NT_DIMS, preferred_element_type=F32)
            y_off = jnp.dot(cg, h_prev.astype(BF16), preferred_element_type=F32)
            xs_g = xc_ref[tok, cols]
            both = jnp.dot(factors, expand_ref[:, cols], preferred_element_type=F32)
            e_exp, w_exp = both[:lc], both[lc:]
            for pr in range(HPG // 2):
                mats = []
                for hh in (g * HPG + 2 * pr, g * HPG + 2 * pr + 1):
                    seg = a2[:, hh:hh + 1] - r_t[hh:hh + 1, :]
                    dec = jnp.exp2(jnp.where(causal, seg, -jnp.inf))
                    mats.append((cb * dec).astype(BF16))
                xp = xs_g[:, pr * LANES:(pr + 1) * LANES].astype(BF16)
                rhs = jnp.concatenate([jnp.where(lane_lo, xp, 0), jnp.where(lane_lo, 0, xp)], axis=0)
                y_diag = jnp.dot(jnp.concatenate(mats, axis=1), rhs, preferred_element_type=F32)
                pc = slice(g * GROUP_W + pr * LANES, g * GROUP_W + (pr + 1) * LANES)
                ysc_ref[tok, pc] = y_diag + (y_off * e_exp)[:, pr * LANES:(pr + 1) * LANES]
            xw = (xs_g * w_exp).astype(BF16)
            s_t = jnp.dot(bg.T.astype(BF16), xw, preferred_element_type=F32)
            hst_ref[:, cols] = h_prev * e_exp[lc - 1:lc, :] + s_t

    y = ysc_ref[...] + xc_ref[:, 0:D_SSM] * dexp_ref[...]
    _gated_norm_store(y, zg_ref, gnw_ref, y_ref)

    @pl.when(c == nsteps - 1)
    def _():
        hout_ref[0] = hst_ref[...].T


def _ssd_short_body(zg_ref, xc_ref, dt_ref, h0_ref, alr_ref, expand_ref, exps_ref, dexp_ref, gnw_ref,
                    y_ref, hout_ref, ysc_ref, *, lc, spb):
    rows = spb * lc
    hl = HEADS * lc
    dt = dt_ref[...]
    a_cum = _prefix_sum_rows(dt * (-jnp.exp(alr_ref[...])), lc)
    a_last = a_cum.reshape(spb, lc, LANES)[:, lc - 1:lc, :]
    a_last_b = jnp.broadcast_to(a_last, (spb, lc, LANES)).reshape(rows, LANES)
    expand = expand_ref[...]
    e_exp = _dot_split(jnp.exp(a_cum), expand, 2)
    w_exp = _dot_split(jnp.exp(a_last_b - a_cum) * dt, expand, 2)
    cd_t = jnp.exp(a_last.reshape(spb, LANES)).T

    z2 = _dot_split(jnp.concatenate([a_cum, dt], axis=0), exps_ref[...], 3)
    zc, zd = z2[:rows], z2[rows:]
    trow = lax.broadcasted_iota(jnp.int32, (rows, hl), 0) % lc
    tsrc = lax.broadcasted_iota(jnp.int32, (rows, hl), 1) % lc
    diag = trow == tsrc

    def per_source(z):
        d = jnp.where(diag, z, 0.0).reshape(spb, lc, hl).sum(axis=1, keepdims=True)
        return jnp.broadcast_to(d, (spb, lc, hl)).reshape(rows, hl)

    dec = jnp.exp(jnp.where(trow >= tsrc, zc - per_source(zc), -jnp.inf)) * per_source(zd)

    pair_w = HPG * lc
    blockdiag = (lax.broadcasted_iota(jnp.int32, (pair_w, GROUP_W), 0) // lc
                 == lax.broadcasted_iota(jnp.int32, (pair_w, GROUP_W), 1) // HEAD_DIM)

    for q in range(spb):
        tok = slice(q * lc, (q + 1) * lc)
        for g in range(GROUPS):
            bg = xc_ref[tok, D_SSM + g * D_STATE:D_SSM + (g + 1) * D_STATE]
            cg = xc_ref[tok, D_SSM + GN + g * D_STATE:D_SSM + GN + (g + 1) * D_STATE].astype(BF16)
            cols = slice(g * GROUP_W, (g + 1) * GROUP_W)
            h_prev = h0_ref[q, cols, :]
            xs_g = xc_ref[tok, cols]
            b_tiled = jnp.concatenate([bg] * HPG, axis=0).astype(BF16)
            cb = lax.dot_general(cg, b_tiled, NT_DIMS, preferred_element_type=F32)
            m = (cb * dec[tok, g * pair_w:(g + 1) * pair_w]).astype(BF16)
            x_diag = jnp.where(blockdiag, jnp.concatenate([xs_g] * HPG, axis=0), 0.0).astype(BF16)
            y_diag = jnp.dot(m, x_diag, preferred_element_type=F32)
            y_off = lax.dot_general(cg, h_prev.astype(BF16), NT_DIMS, preferred_element_type=F32)
            ysc_ref[tok, cols] = y_diag + y_off * e_exp[tok, cols]
            xw_t = (xs_g * w_exp[tok, cols]).T.astype(BF16)
            s_g = jnp.dot(xw_t, bg.astype(BF16), preferred_element_type=F32)
            cd = jnp.concatenate(
                [jnp.broadcast_to(cd_t[g * HPG + jh:g * HPG + jh + 1, q:q + 1], (HEAD_DIM, D_STATE))
                 for jh in range(HPG)], axis=0)
            hout_ref[q, cols, :] = h_prev * cd + s_g

    y = ysc_ref[...] + xc_ref[:, 0:D_SSM] * dexp_ref[...]
    _gated_norm_store(y, zg_ref, gnw_ref, y_ref)


def _ssd_chunked(zg, xc, dt, alr, expand, dexp, gnw, w_f32, *, nb, nc):
    t = zg.shape[0]
    lc = SSD_CHUNK
    cps = CHUNKS_PER_STEP
    nsteps = nc // cps
    rows = cps * lc
    wrows = w_f32.shape[0] // (nb * nsteps)
    assert nsteps * cps == nc and wrows * nb * nsteps == w_f32.shape[0] and wrows % BF16_SUBLANES == 0
    tok = lambda r, width: pl.BlockSpec((r, width), lambda b, c: (b * nsteps + c, 0))
    par = lambda r, width: pl.BlockSpec((r, width), lambda b, c: (0, 0))
    h_spec = pl.BlockSpec((1, D_SSM, D_STATE), lambda b, c: (b, 0, 0))
    return pl.pallas_call(
        functools.partial(_ssd_chunk_body, lc=lc, cps=cps, nsteps=nsteps),
        grid=(nb, nsteps),
        in_specs=[tok(rows, D_SSM), tok(rows, CONV_DIM), tok(rows, LANES), par(1, LANES),
                  par(LANES, D_SSM), par(1, D_SSM), par(1, D_SSM), tok(wrows, w_f32.shape[1])],
        out_specs=[tok(rows, D_SSM), h_spec, tok(wrows, w_f32.shape[1])],
        out_shape=[jax.ShapeDtypeStruct((t, D_SSM), BF16),
                   jax.ShapeDtypeStruct((nb, D_SSM, D_STATE), F32),
                   jax.ShapeDtypeStruct(w_f32.shape, BF16)],
        scratch_shapes=[pltpu.VMEM((D_STATE, D_SSM), F32), pltpu.VMEM((rows, D_SSM), F32)],
        compiler_params=pltpu.CompilerParams(
            dimension_semantics=("arbitrary", "arbitrary"), vmem_limit_bytes=VMEM_LIMIT),
        name="ssd_scan",
    )(zg, xc, dt, alr, expand, dexp, gnw, w_f32)


def _ssd_short(zg, xc, dt, h0, alr, expand, exps, dexp, gnw, *, nb, lc):
    t = zg.shape[0]
    spb = SHORT_SEQS_PER_STEP
    rows = spb * lc
    tok = lambda width: pl.BlockSpec((rows, width), lambda b: (b, 0))
    par = lambda r, width: pl.BlockSpec((r, width), lambda b: (0, 0))
    h_spec = pl.BlockSpec((spb, D_SSM, D_STATE), lambda b: (b, 0, 0))
    return pl.pallas_call(
        functools.partial(_ssd_short_body, lc=lc, spb=spb),
        grid=(nb // spb,),
        in_specs=[tok(D_SSM), tok(CONV_DIM), tok(LANES), h_spec, par(1, LANES), par(LANES, D_SSM),
                  par(LANES, HEADS * lc), par(1, D_SSM), par(1, D_SSM)],
        out_specs=[tok(D_SSM), h_spec],
        out_shape=[jax.ShapeDtypeStruct((t, D_SSM), BF16),
                   jax.ShapeDtypeStruct((nb, D_SSM, D_STATE), F32)],
        scratch_shapes=[pltpu.VMEM((rows, D_SSM), F32)],
        compiler_params=pltpu.CompilerParams(
            dimension_semantics=("arbitrary",), vmem_limit_bytes=VMEM_LIMIT),
        name="ssd_scan_short",
    )(zg, xc, dt, h0, alr, expand, exps, dexp, gnw)


def _out_body(ys_ref, yc_ref, x_ref, w1_ref, w2_ref, fnw_ref, o_ref):
    acc = jnp.dot(ys_ref[...], w1_ref[...], preferred_element_type=F32)
    acc = acc + jnp.dot(yc_ref[...], w2_ref[...], preferred_element_type=F32)
    r = x_ref[...] + acc
    var = jnp.mean(r * r, axis=-1, keepdims=True)
    o_ref[...] = r * lax.rsqrt(var + EPS) * fnw_ref[...]


def _out(ys, yc, x, w_o, fnw, *, tm):
    t = x.shape[0]
    row = lambda width: pl.BlockSpec((tm, width), lambda m: (m, 0))
    return pl.pallas_call(
        _out_body,
        grid=(t // tm,),
        in_specs=[row(D_SSM), row(D_SHORT), row(D_MODEL),
                  pl.BlockSpec((D_SSM, D_MODEL), lambda m: (0, 0), pipeline_mode=pl.Buffered(1)),
                  pl.BlockSpec((D_SHORT, D_MODEL), lambda m: (1, 0), pipeline_mode=pl.Buffered(1)),
                  pl.BlockSpec((1, D_MODEL), lambda m: (0, 0))],
        out_specs=row(D_MODEL),
        out_shape=jax.ShapeDtypeStruct((t, D_MODEL), F32),
        compiler_params=pltpu.CompilerParams(
            dimension_semantics=("arbitrary",), vmem_limit_bytes=VMEM_LIMIT),
        name="out_proj",
    )(ys, yc, x, w_o, w_o, fnw)


def _projections(x2d, conv_state, short_state, p, *, seqlen):
    tm = ROW_TILE
    nseq = max(tm // seqlen, 1)
    tps = max(seqlen // tm, 1)
    assert x2d.shape[0] % tm == 0 and (seqlen % tm == 0 or tm % seqlen == 0)
    h, zg, dt = _proj_z(x2d, p["nw"], p["w_t"], p["dtb"], tm=tm)
    xc, conv_new = _proj_xc(h, p["w_t"], p["conv_w"], p["conv_b"], conv_state,
                            tm=tm, nseq=nseq, tps=tps)
    tm_s = min(SHORT_ROW_TILE, x2d.shape[0])
    assert seqlen % tm_s == 0 or tm_s % seqlen == 0
    y_c, short_new = _short(h, p["w_t"], p["conv_short_w"], short_state,
                            tm=tm_s, tn=SHORT_TN, nseq=max(tm_s // seqlen, 1), tps=max(seqlen // tm_s, 1))
    return zg, xc, dt, y_c, conv_new, short_new


def kernel(x_prompt, x_sample, state_ssm, state_conv_ssd, state_conv_short, norm_w, w_in, conv_ssd_w,
           conv_ssd_b, dt_bias, a_log, d_skip, ssd_norm_w, conv_short_w, w_out, final_norm_w):
    depth = norm_w.shape[0]
    assert depth == 1, "the output projection fuses the final rmsnorm, valid for a single layer"
    bp, lp, _ = x_prompt.shape
    bs, ls, _ = x_sample.shape
    assert lp % SSD_CHUNK == 0 and ls < SSD_CHUNK
    hp = x_prompt.reshape(bp * lp, D_MODEL)
    hs = x_sample.reshape(bs * ls, D_MODEL)
    pad_h = LANES - HEADS
    expand = (jnp.arange(LANES)[:, None] == jnp.arange(D_SSM)[None, :] // HEAD_DIM).astype(BF16)
    expand_pairs = (jnp.arange(LANES)[:, None] == jnp.arange(HEADS * ls)[None, :] // ls).astype(BF16)
    outs = [[] for _ in range(6)]
    for layer in range(depth):
        p = dict(
            nw=norm_w[layer][None, :],
            w_t=w_in[layer].T,
            conv_short_w=conv_short_w[layer],
            conv_w=conv_ssd_w[layer],
            conv_b=conv_ssd_b[layer][None, :],
            dtb=jnp.pad(dt_bias[layer], (0, pad_h))[None, :],
        )
        alr = jnp.pad(a_log[layer], (0, pad_h))[None, :]
        dexp = jnp.repeat(d_skip[layer], HEAD_DIM)[None, :]
        gnw = ssd_norm_w[layer][None, :]

        zg, xc, dt, yc, a2, a3 = _projections(
            hp, jnp.zeros((bp, CONV_K - 1, CONV_DIM), F32), jnp.zeros((bp, SHORT_K - 1, D_SHORT), F32), p,
            seqlen=lp)
        ys, a1, w_o = _ssd_chunked(zg, xc, dt, alr, expand, dexp, gnw, w_out[layer],
                                   nb=bp, nc=lp // SSD_CHUNK)
        zg2, xc2, dt2, yc2, s2, s3 = _projections(
            hs, state_conv_ssd[layer], state_conv_short[layer], p, seqlen=ls)
        ys2, s1 = _ssd_short(zg2, xc2, dt2, state_ssm[layer].reshape(bs, D_SSM, D_STATE), alr, expand,
                             expand_pairs, dexp, gnw, nb=bs, lc=ls)
        fw = final_norm_w[None, :]
        hp = _out(ys, yc, hp, w_o, fw, tm=OUT_ROW_TILE)
        hs = _out(ys2, yc2, hs, w_o, fw, tm=OUT_ROW_TILE)
        for lst, val in zip(outs, (a1.reshape(bp, HEADS, HEAD_DIM, D_STATE), a2, a3,
                                   s1.reshape(bs, HEADS, HEAD_DIM, D_STATE), s2, s3)):
            lst.append(val)
    return (hp.reshape(bp, lp, D_MODEL), hs.reshape(bs, ls, D_MODEL),
            *(jnp.stack(v) for v in outs))
```

```python
import functools

import jax
import jax.numpy as jnp
from jax import lax
from jax.experimental import pallas as pl
from jax.experimental.pallas import tpu as pltpu

F32 = jnp.float32
BF16 = jnp.bfloat16

D_MODEL = 2048
D_SSM = 2048
D_SHORT = 2048
HEADS = 32
HEAD_DIM = 64
D_STATE = 128
GROUPS = 4
HPG = HEADS // GROUPS
GROUP_W = HPG * HEAD_DIM
GN = GROUPS * D_STATE
CONV_DIM = D_SSM + 2 * GN
CONV_K = 4
SHORT_K = 3
SSD_CHUNK = 128
EPS = 1e-6
LANES = 128
SUBLANES = 8
BF16_SUBLANES = 16
VMEM_LIMIT = 56 * 1024 * 1024

ROW_TILE = 1024
TN = 1024
SHORT_ROW_TILE = 2048
SHORT_TN = 256
OUT_ROW_TILE = 512
NORM_ROWS = 32
NORM_UNROLL = 8
Z_TILES = D_SSM // TN
XC_TILES = CONV_DIM // TN
MAIN_VALID = D_SSM + CONV_DIM + HEADS
DT_BLOCK = (D_SSM + CONV_DIM) // LANES

NT_DIMS = (((1,), (1,)), ((), ()))
LOG2E = 1.4426950408889634
CHUNKS_PER_STEP = 2
SHORT_SEQS_PER_STEP = 8


def _silu(x):
    half = 0.5 * x
    return half + half * jnp.tanh(half)


def _softplus(x):
    return jnp.maximum(x, 0.0) + jnp.log1p(jnp.exp(-jnp.abs(x)))


def _proj_z_body(x_ref, nw_ref, w_ref, wdt_ref, dtb_ref, h_ref, zg_ref, dt_ref, *wbf, tm):
    j = pl.program_id(1)
    if wbf:
        @pl.when(pl.program_id(0) == 0)
        def _():
            wbf[0][j] = w_ref[...].astype(BF16)

    @pl.when(j == 0)
    def _():
        nw = nw_ref[...]

        def body(i, carry):
            r = pl.ds(pl.multiple_of(i * NORM_ROWS, NORM_ROWS), NORM_ROWS)
            xf = x_ref[r, :]
            var = jnp.mean(xf * xf, axis=-1, keepdims=True)
            h_ref[r, :] = (xf * lax.rsqrt(var + EPS) * nw).astype(BF16)
            return carry

        lax.fori_loop(0, tm // NORM_ROWS, body, 0, unroll=NORM_UNROLL)
        dt_raw = lax.dot_general(h_ref[...], wdt_ref[...].astype(BF16), NT_DIMS,
                                 preferred_element_type=F32)
        lane = lax.broadcasted_iota(jnp.int32, dt_raw.shape, 1)
        dt_ref[...] = jnp.where(lane < HEADS, _softplus(dt_raw + dtb_ref[...]), 0.0)

    w = wbf[0][j] if wbf else w_ref[...].astype(BF16)
    acc = lax.dot_general(h_ref[...], w, NT_DIMS, preferred_element_type=F32)
    zg_ref[...] = _silu(acc).astype(BF16)


def _proj_z(x, nw, w_t, dtb, *, tm):
    t = x.shape[0]
    if t // tm > 1:
        w_spec = pl.BlockSpec((TN, D_MODEL), lambda m, j: (jnp.where(m == 0, j, Z_TILES - 1), 0),
                              pipeline_mode=pl.Buffered(1))
        scratch = [pltpu.VMEM((Z_TILES, TN, D_MODEL), BF16)]
    else:
        w_spec = pl.BlockSpec((TN, D_MODEL), lambda m, j: (j, 0))
        scratch = []
    return pl.pallas_call(
        functools.partial(_proj_z_body, tm=tm),
        grid=(t // tm, Z_TILES),
        in_specs=[
            pl.BlockSpec((tm, D_MODEL), lambda m, j: (m, 0)),
            pl.BlockSpec((1, D_MODEL), lambda m, j: (0, 0)),
            w_spec,
            pl.BlockSpec((LANES, D_MODEL), lambda m, j: (DT_BLOCK, 0)),
            pl.BlockSpec((1, LANES), lambda m, j: (0, 0)),
        ],
        out_specs=[pl.BlockSpec((tm, D_MODEL), lambda m, j: (m, 0)),
                   pl.BlockSpec((tm, TN), lambda m, j: (m, j)),
                   pl.BlockSpec((tm, LANES), lambda m, j: (m, 0))],
        out_shape=[jax.ShapeDtypeStruct((t, D_MODEL), BF16),
                   jax.ShapeDtypeStruct((t, D_SSM), BF16),
                   jax.ShapeDtypeStruct((t, LANES), F32)],
        scratch_shapes=scratch,
        compiler_params=pltpu.CompilerParams(
            dimension_semantics=("arbitrary", "arbitrary"), vmem_limit_bytes=VMEM_LIMIT),
        name="in_proj_z",
    )(x, nw, w_t, w_t, dtb)


def _load_history(pad_ref, st_ref, carry, j, *, first, hist, tps, row_major=False):
    def from_state():
        pad_ref[:, 0:SUBLANES, :] = jnp.zeros((pad_ref.shape[0], SUBLANES, pad_ref.shape[2]), F32)
        if row_major:
            for k in range(hist):
                pad_ref[:, SUBLANES - hist + k, :] = st_ref[k]
        else:
            pad_ref[:, SUBLANES - hist:SUBLANES, :] = st_ref[...]

    if tps > 1:
        (carry_ref,) = carry
        pl.when(first)(from_state)

        @pl.when(jnp.logical_not(first))
        def _():
            pad_ref[:, 0:SUBLANES, :] = carry_ref[j]
    else:
        from_state()


def _causal_conv(pad_ref, cw_ref, taps):
    xe = pad_ref[...]
    if taps == 4:
        s1 = pltpu.roll(xe, 1, axis=1)
        near = xe * cw_ref[3:4, :] + s1 * cw_ref[2:3, :]
        far = xe * cw_ref[1:2, :] + s1 * cw_ref[0:1, :]
        return near[:, SUBLANES:, :] + pltpu.roll(far, 2, axis=1)[:, SUBLANES:, :]
    conv = xe[:, SUBLANES:, :] * cw_ref[taps - 1:taps, :]
    for s in range(1, taps):
        shifted = pltpu.roll(xe, s, axis=1)[:, SUBLANES:, :]
        conv = conv + shifted * cw_ref[taps - 1 - s:taps - s, :]
    return conv


def _proj_xc_body(h_ref, w_ref, cw_ref, cb_ref, st_ref, xc_ref, tails_ref, pad_ref, *carry,
                  tm, nseq, tps, row_major):
    j = pl.program_id(1)
    lt = tm // nseq
    _load_history(pad_ref, st_ref, carry, j, first=(pl.program_id(0) % tps) == 0, hist=CONV_K - 1,
                  tps=tps, row_major=row_major)
    acc = lax.dot_general(h_ref[...], w_ref[...].astype(BF16), NT_DIMS, preferred_element_type=F32)
    pad_ref[:, SUBLANES:SUBLANES + lt, :] = acc.reshape(nseq, lt, TN)
    conv = _causal_conv(pad_ref, cw_ref, CONV_K)
    xc_ref[...] = _silu(conv + cb_ref[...]).reshape(tm, TN)
    if row_major:
        for k in range(CONV_K - 1):
            tails_ref[k] = pad_ref[:, lt + SUBLANES - (CONV_K - 1) + k, :]
    else:
        tails_ref[...] = pad_ref[:, lt + SUBLANES - (CONV_K - 1):lt + SUBLANES, :]
    if tps > 1:
        carry[0][j] = pad_ref[:, lt:lt + SUBLANES, :]


def _proj_xc(h, w_t, conv_w, conv_b, conv_state, *, tm, nseq, tps):
    t = h.shape[0]
    lt = tm // nseq
    hist = CONV_K - 1
    row_major = nseq % SUBLANES == 0
    scratch = [pltpu.VMEM((nseq, SUBLANES + lt, TN), F32)]
    if tps > 1:
        scratch.append(pltpu.VMEM((XC_TILES, nseq, SUBLANES, TN), F32))
    ntail = (t // tm) * nseq
    if row_major:
        state = jnp.transpose(conv_state, (1, 0, 2))
        st_spec = pl.BlockSpec((hist, nseq, TN), lambda m, j: (0, m // tps, j))
        tail_spec = pl.BlockSpec((hist, nseq, TN), lambda m, j: (0, m, j))
        tail_shape = (hist, ntail, CONV_DIM)
    else:
        state = conv_state
        st_spec = pl.BlockSpec((nseq, hist, TN), lambda m, j: (m // tps, 0, j))
        tail_spec = pl.BlockSpec((nseq, hist, TN), lambda m, j: (m, 0, j))
        tail_shape = (ntail, hist, CONV_DIM)
    xc, tails = pl.pallas_call(
        functools.partial(_proj_xc_body, tm=tm, nseq=nseq, tps=tps, row_major=row_major),
        grid=(t // tm, XC_TILES),
        in_specs=[
            pl.BlockSpec((tm, D_MODEL), lambda m, j: (m, 0)),
            pl.BlockSpec((TN, D_MODEL), lambda m, j: (Z_TILES + j, 0)),
            pl.BlockSpec((CONV_K, TN), lambda m, j: (0, j)),
            pl.BlockSpec((1, TN), lambda m, j: (0, j)),
            st_spec,
        ],
        out_specs=[pl.BlockSpec((tm, TN), lambda m, j: (m, j)), tail_spec],
        out_shape=[jax.ShapeDtypeStruct((t, CONV_DIM), F32), jax.ShapeDtypeStruct(tail_shape, F32)],
        scratch_shapes=scratch,
        compiler_params=pltpu.CompilerParams(
            dimension_semantics=("arbitrary", "arbitrary"), vmem_limit_bytes=VMEM_LIMIT),
        name="in_proj_xc",
    )(h, w_t, conv_w, conv_b, state)
    if row_major:
        tails = tails.reshape(hist, t // (tm * tps), tps, nseq, CONV_DIM)[:, :, -1]
        return xc, jnp.transpose(tails.reshape(hist, -1, CONV_DIM), (1, 0, 2))
    tails = tails.reshape(t // (tm * tps), tps, nseq, hist, CONV_DIM)[:, -1]
    return xc, tails.reshape(conv_state.shape)


def _short_body(h_ref, *refs, tm, tn, nseq, tps, shift):
    w_refs, (cw_ref, st_ref, y_ref, ns_ref, wbf_ref, pad_ref), carry = refs[:8], refs[8:14], refs[14:]
    m = pl.program_id(1)
    lt = tm // nseq

    @pl.when(m == 0)
    def _():
        for k in range(4):
            a_ref, b_ref = w_refs[2 * k], w_refs[2 * k + 1]
            wbf_ref[k] = jnp.concatenate([a_ref[shift:, :], b_ref[...]], axis=0).astype(BF16)

    _load_history(pad_ref, st_ref, carry, 0, first=(m % tps) == 0, hist=SHORT_K - 1, tps=tps)
    h = h_ref[...]
    band = lambda k: lax.dot_general(h, wbf_ref[k], NT_DIMS, preferred_element_type=F32)
    c = band(2)
    v = band(3)
    pad_ref[:, SUBLANES:SUBLANES + lt, :] = (c * v).reshape(nseq, lt, tn)
    conv = _causal_conv(pad_ref, cw_ref, SHORT_K)
    b = band(1)
    z = band(0)
    y = b * conv.reshape(tm, tn) * _silu(z)
    y_ref[...] = y.astype(y_ref.dtype)
    ns_ref[...] = pad_ref[:, lt + SUBLANES - (SHORT_K - 1):lt + SUBLANES, :]
    if tps > 1:
        carry[0][0] = pad_ref[:, lt:lt + SUBLANES, :]


def _short(h, w_t, conv_w, state, *, tm, tn, nseq, tps):
    t = h.shape[0]
    nj = D_SHORT // tn
    lt = tm // nseq
    base = (MAIN_VALID // tn) * tn
    shift = MAIN_VALID - base
    assert shift % SUBLANES == 0 and tn % shift == 0 and base % shift == 0
    scratch = [pltpu.VMEM((4, tn, D_MODEL), BF16), pltpu.VMEM((nseq, SUBLANES + lt, tn), F32)]
    if tps > 1:
        scratch.append(pltpu.VMEM((1, nseq, SUBLANES, tn), F32))
    w_specs = []
    for k in range(4):
        r0 = base + k * D_SHORT
        w_specs.append(pl.BlockSpec((tn, D_MODEL), lambda j, m, r0=r0: (r0 // tn + j, 0)))
        w_specs.append(pl.BlockSpec((shift, D_MODEL), lambda j, m, r0=r0: ((r0 + tn * (j + 1)) // shift, 0)))

    y, tails = pl.pallas_call(
        functools.partial(_short_body, tm=tm, tn=tn, nseq=nseq, tps=tps, shift=shift),
        grid=(nj, t // tm),
        in_specs=[
            pl.BlockSpec((tm, D_MODEL), lambda j, m: (m, 0)),
            *w_specs,
            pl.BlockSpec((SHORT_K, tn), lambda j, m: (0, j)),
            pl.BlockSpec((nseq, SHORT_K - 1, tn), lambda j, m: (m // tps, 0, j)),
        ],
        out_specs=[pl.BlockSpec((tm, tn), lambda j, m: (m, j)),
                   pl.BlockSpec((nseq, SHORT_K - 1, tn), lambda j, m: (m, 0, j))],
        out_shape=[jax.ShapeDtypeStruct((t, D_SHORT), BF16),
                   jax.ShapeDtypeStruct(((t // tm) * nseq, SHORT_K - 1, D_SHORT), F32)],
        scratch_shapes=scratch,
        compiler_params=pltpu.CompilerParams(
            dimension_semantics=("arbitrary", "arbitrary"), vmem_limit_bytes=VMEM_LIMIT),
        name="short_conv",
    )(h, *([w_t] * 8), conv_w, state)
    tails = tails.reshape(t // (tm * tps), tps, nseq, SHORT_K - 1, D_SHORT)[:, -1]
    return y, tails.reshape(state.shape)


def _dot_split(x, w, pieces):
    out = None
    r = x
    for _ in range(pieces):
        p = r.astype(BF16)
        d = jnp.dot(p, w, preferred_element_type=F32)
        out = d if out is None else out + d
        r = r - p.astype(F32)
    return out


def _prefix_sum_rows(x, period):
    t = lax.broadcasted_iota(jnp.int32, x.shape, 0) % period
    k = 1
    while k < period:
        x = x + jnp.where(t >= k, pltpu.roll(x, k, axis=0), 0.0)
        k *= 2
    return x


def _gated_norm_store(y, zg_ref, gnw_ref, y_ref):
    y = y * zg_ref[...].astype(F32)
    for g in range(GROUPS):
        cs = slice(g * GROUP_W, (g + 1) * GROUP_W)
        blk = y[:, cs]
        ms = jnp.mean(blk * blk, axis=-1, keepdims=True)
        y_ref[:, cs] = (blk * lax.rsqrt(ms + EPS) * gnw_ref[:, cs]).astype(y_ref.dtype)


def _ssd_chunk_body(zg_ref, xc_ref, dt_ref, alr_ref, expand_ref, dexp_ref, gnw_ref, wsrc_ref,
                    y_ref, hout_ref, wdst_ref, hst_ref, *, lc, cps, nsteps):
    c = pl.program_id(1)
    wdst_ref[...] = wsrc_ref[...].astype(BF16)

    @pl.when(c == 0)
    def _():
        hst_ref[...] = jnp.zeros(hst_ref.shape, F32)

    row_i = lax.broadcasted_iota(jnp.int32, (lc, lc), 0)
    col_i = lax.broadcasted_iota(jnp.int32, (lc, lc), 1)
    causal = row_i >= col_i
    lane_lo = lax.broadcasted_iota(jnp.int32, (lc, LANES), 1) < HEAD_DIM

    a_row = -jnp.exp(alr_ref[...])

    for k in range(cps):
        tok = slice(k * lc, (k + 1) * lc)
        dt = dt_ref[tok, :]
        a_cum = _prefix_sum_rows(dt * a_row, lc)
        a_last = a_cum[lc - 1:lc, :]
        factors = jnp.concatenate([jnp.exp(a_cum), jnp.exp(a_last - a_cum) * dt], axis=0).astype(BF16)
        a2 = a_cum * LOG2E
        r_t = (a2 - jnp.log2(dt)).T

        for g in range(GROUPS):
            bg = xc_ref[tok, D_SSM + g * D_STATE:D_SSM + (g + 1) * D_STATE]
            cg = xc_ref[tok, D_SSM + GN + g * D_STATE:D_SSM + GN + (g + 1) * D_STATE].astype(BF16)
            cols = slice(g * GROUP_W, (g + 1) * GROUP_W)
            h_prev = hst_ref[:, cols]
            cb = lax.dot_general(cg, bg.astype(BF16), NT_DIMS, preferred_element_type=F32)
            y_off = jnp.dot(cg, h_prev.astype(BF16), preferred_element_type=F32)
            xs_g = xc_ref[tok, cols]
            both = jnp.dot(factors, expand_ref[:, cols], preferred_element_type=F32)
            e_exp, w_exp = both[:lc], both[lc:]
            y_pairs = []
            for pr in range(HPG // 2):
                mats = []
                for hh in (g * HPG + 2 * pr, g * HPG + 2 * pr + 1):
                    seg = a2[:, hh:hh + 1] - r_t[hh:hh + 1, :]
                    dec = jnp.exp2(jnp.where(causal, seg, -jnp.inf))
                    mats.append((cb * dec).astype(BF16))
                xp = xs_g[:, pr * LANES:(pr + 1) * LANES].astype(BF16)
                rhs = jnp.concatenate([jnp.where(lane_lo, xp, 0), jnp.where(lane_lo, 0, xp)], axis=0)
                y_pairs.append(jnp.dot(jnp.concatenate(mats, axis=1), rhs, preferred_element_type=F32))
            y = jnp.concatenate(y_pairs, axis=1) + y_off * e_exp + xs_g * dexp_ref[:, cols]
            y = y * zg_ref[tok, cols].astype(F32)
            ms = jnp.mean(y * y, axis=-1, keepdims=True)
            y_ref[tok, cols] = (y * lax.rsqrt(ms + EPS) * gnw_ref[:, cols]).astype(y_ref.dtype)
            xw = (xs_g * w_exp).astype(BF16)
            s_t = jnp.dot(bg.T.astype(BF16), xw, preferred_element_type=F32)
            hst_ref[:, cols] = h_prev * e_exp[lc - 1:lc, :] + s_t

    @pl.when(c == nsteps - 1)
    def _():
        hout_ref[0] = hst_ref[...].T


def _ssd_short_body(zg_ref, xc_ref, dt_ref, h0_ref, alr_ref, expand_ref, exps_ref, dexp_ref, gnw_ref,
                    y_ref, hout_ref, ysc_ref, *, lc, spb):
    rows = spb * lc
    hl = HEADS * lc
    dt = dt_ref[...]
    a_cum = _prefix_sum_rows(dt * (-jnp.exp(alr_ref[...])), lc)
    a_last = a_cum.reshape(spb, lc, LANES)[:, lc - 1:lc, :]
    a_last_b = jnp.broadcast_to(a_last, (spb, lc, LANES)).reshape(rows, LANES)
    expand = expand_ref[...]
    e_exp = _dot_split(jnp.exp(a_cum), expand, 2)
    w_exp = _dot_split(jnp.exp(a_last_b - a_cum) * dt, expand, 2)
    cd_t = jnp.exp(a_last.reshape(spb, LANES)).T

    z2 = _dot_split(jnp.concatenate([a_cum, dt], axis=0), exps_ref[...], 3)
    zc, zd = z2[:rows], z2[rows:]
    trow = lax.broadcasted_iota(jnp.int32, (rows, hl), 0) % lc
    tsrc = lax.broadcasted_iota(jnp.int32, (rows, hl), 1) % lc
    diag = trow == tsrc

    def per_source(z):
        d = jnp.where(diag, z, 0.0).reshape(spb, lc, hl).sum(axis=1, keepdims=True)
        return jnp.broadcast_to(d, (spb, lc, hl)).reshape(rows, hl)

    dec = jnp.exp(jnp.where(trow >= tsrc, zc - per_source(zc), -jnp.inf)) * per_source(zd)

    pair_w = HPG * lc
    blockdiag = (lax.broadcasted_iota(jnp.int32, (pair_w, GROUP_W), 0) // lc
                 == lax.broadcasted_iota(jnp.int32, (pair_w, GROUP_W), 1) // HEAD_DIM)

    for q in range(spb):
        tok = slice(q * lc, (q + 1) * lc)
        for g in range(GROUPS):
            bg = xc_ref[tok, D_SSM + g * D_STATE:D_SSM + (g + 1) * D_STATE]
            cg = xc_ref[tok, D_SSM + GN + g * D_STATE:D_SSM + GN + (g + 1) * D_STATE].astype(BF16)
            cols = slice(g * GROUP_W, (g + 1) * GROUP_W)
            h_prev = h0_ref[q, cols, :]
            xs_g = xc_ref[tok, cols]
            b_tiled = jnp.concatenate([bg] * HPG, axis=0).astype(BF16)
            cb = lax.dot_general(cg, b_tiled, NT_DIMS, preferred_element_type=F32)
            m = (cb * dec[tok, g * pair_w:(g + 1) * pair_w]).astype(BF16)
            x_diag = jnp.where(blockdiag, jnp.concatenate([xs_g] * HPG, axis=0), 0.0).astype(BF16)
            y_diag = jnp.dot(m, x_diag, preferred_element_type=F32)
            y_off = lax.dot_general(cg, h_prev.astype(BF16), NT_DIMS, preferred_element_type=F32)
            ysc_ref[tok, cols] = y_diag + y_off * e_exp[tok, cols]
            xw_t = (xs_g * w_exp[tok, cols]).T.astype(BF16)
            s_g = jnp.dot(xw_t, bg.astype(BF16), preferred_element_type=F32)
            cd = jnp.concatenate(
                [jnp.broadcast_to(cd_t[g * HPG + jh:g * HPG + jh + 1, q:q + 1], (HEAD_DIM, D_STATE))
                 for jh in range(HPG)], axis=0)
            hout_ref[q, cols, :] = h_prev * cd + s_g

    y = ysc_ref[...] + xc_ref[:, 0:D_SSM] * dexp_ref[...]
    _gated_norm_store(y, zg_ref, gnw_ref, y_ref)


def _ssd_chunked(zg, xc, dt, alr, expand, dexp, gnw, w_f32, *, nb, nc):
    t = zg.shape[0]
    lc = SSD_CHUNK
    cps = CHUNKS_PER_STEP
    nsteps = nc // cps
    rows = cps * lc
    wrows = w_f32.shape[0] // (nb * nsteps)
    assert nsteps * cps == nc and wrows * nb * nsteps == w_f32.shape[0] and wrows % BF16_SUBLANES == 0
    tok = lambda r, width: pl.BlockSpec((r, width), lambda b, c: (b * nsteps + c, 0))
    par = lambda r, width: pl.BlockSpec((r, width), lambda b, c: (0, 0))
    h_spec = pl.BlockSpec((1, D_SSM, D_STATE), lambda b, c: (b, 0, 0))
    return pl.pallas_call(
        functools.partial(_ssd_chunk_body, lc=lc, cps=cps, nsteps=nsteps),
        grid=(nb, nsteps),
        in_specs=[tok(rows, D_SSM), tok(rows, CONV_DIM), tok(rows, LANES), par(1, LANES),
                  par(LANES, D_SSM), par(1, D_SSM), par(1, D_SSM), tok(wrows, w_f32.shape[1])],
        out_specs=[tok(rows, D_SSM), h_spec, tok(wrows, w_f32.shape[1])],
        out_shape=[jax.ShapeDtypeStruct((t, D_SSM), BF16),
                   jax.ShapeDtypeStruct((nb, D_SSM, D_STATE), F32),
                   jax.ShapeDtypeStruct(w_f32.shape, BF16)],
        scratch_shapes=[pltpu.VMEM((D_STATE, D_SSM), F32)],
        compiler_params=pltpu.CompilerParams(
            dimension_semantics=("arbitrary", "arbitrary"), vmem_limit_bytes=VMEM_LIMIT),
        name="ssd_scan",
    )(zg, xc, dt, alr, expand, dexp, gnw, w_f32)


def _ssd_short(zg, xc, dt, h0, alr, expand, exps, dexp, gnw, *, nb, lc):
    t = zg.shape[0]
    spb = SHORT_SEQS_PER_STEP
    rows = spb * lc
    tok = lambda width: pl.BlockSpec((rows, width), lambda b: (b, 0))
    par = lambda r, width: pl.BlockSpec((r, width), lambda b: (0, 0))
    h_spec = pl.BlockSpec((spb, D_SSM, D_STATE), lambda b: (b, 0, 0))
    return pl.pallas_call(
        functools.partial(_ssd_short_body, lc=lc, spb=spb),
        grid=(nb // spb,),
        in_specs=[tok(D_SSM), tok(CONV_DIM), tok(LANES), h_spec, par(1, LANES), par(LANES, D_SSM),
                  par(LANES, HEADS * lc), par(1, D_SSM), par(1, D_SSM)],
        out_specs=[tok(D_SSM), h_spec],
        out_shape=[jax.ShapeDtypeStruct((t, D_SSM), BF16),
                   jax.ShapeDtypeStruct((nb, D_SSM, D_STATE), F32)],
        scratch_shapes=[pltpu.VMEM((rows, D_SSM), F32)],
        compiler_params=pltpu.CompilerParams(
            dimension_semantics=("arbitrary",), vmem_limit_bytes=VMEM_LIMIT),
        name="ssd_scan_short",
    )(zg, xc, dt, h0, alr, expand, exps, dexp, gnw)


def _out_body(ys_ref, yc_ref, x_ref, w1_ref, w2_ref, fnw_ref, o_ref):
    acc = jnp.dot(ys_ref[...], w1_ref[...], preferred_element_type=F32)
    acc = acc + jnp.dot(yc_ref[...], w2_ref[...], preferred_element_type=F32)
    r = x_ref[...] + acc
    var = jnp.mean(r * r, axis=-1, keepdims=True)
    o_ref[...] = r * lax.rsqrt(var + EPS) * fnw_ref[...]


def _out(ys, yc, x, w_o, fnw, *, tm):
    t = x.shape[0]
    row = lambda width: pl.BlockSpec((tm, width), lambda m: (m, 0))
    return pl.pallas_call(
        _out_body,
        grid=(t // tm,),
        in_specs=[row(D_SSM), row(D_SHORT), row(D_MODEL),
                  pl.BlockSpec((D_SSM, D_MODEL), lambda m: (0, 0), pipeline_mode=pl.Buffered(1)),
                  pl.BlockSpec((D_SHORT, D_MODEL), lambda m: (1, 0), pipeline_mode=pl.Buffered(1)),
                  pl.BlockSpec((1, D_MODEL), lambda m: (0, 0))],
        out_specs=row(D_MODEL),
        out_shape=jax.ShapeDtypeStruct((t, D_MODEL), F32),
        compiler_params=pltpu.CompilerParams(
            dimension_semantics=("arbitrary",), vmem_limit_bytes=VMEM_LIMIT),
        name="out_proj",
    )(ys, yc, x, w_o, w_o, fnw)


def _projections(x2d, conv_state, short_state, p, *, seqlen):
    tm = ROW_TILE
    nseq = max(tm // seqlen, 1)
    tps = max(seqlen // tm, 1)
    assert x2d.shape[0] % tm == 0 and (seqlen % tm == 0 or tm % seqlen == 0)
    h, zg, dt = _proj_z(x2d, p["nw"], p["w_t"], p["dtb"], tm=tm)
    xc, conv_new = _proj_xc(h, p["w_t"], p["conv_w"], p["conv_b"], conv_state,
                            tm=tm, nseq=nseq, tps=tps)
    tm_s = min(SHORT_ROW_TILE, x2d.shape[0])
    assert seqlen % tm_s == 0 or tm_s % seqlen == 0
    y_c, short_new = _short(h, p["w_t"], p["conv_short_w"], short_state,
                            tm=tm_s, tn=SHORT_TN, nseq=max(tm_s // seqlen, 1), tps=max(seqlen // tm_s, 1))
    return zg, xc, dt, y_c, conv_new, short_new


def kernel(x_prompt, x_sample, state_ssm, state_conv_ssd, state_conv_short, norm_w, w_in, conv_ssd_w,
           conv_ssd_b, dt_bias, a_log, d_skip, ssd_norm_w, conv_short_w, w_out, final_norm_w):
    depth = norm_w.shape[0]
    assert depth == 1, "the output projection fuses the final rmsnorm, valid for a single layer"
    bp, lp, _ = x_prompt.shape
    bs, ls, _ = x_sample.shape
    assert lp % SSD_CHUNK == 0 and ls < SSD_CHUNK
    hp = x_prompt.reshape(bp * lp, D_MODEL)
    hs = x_sample.reshape(bs * ls, D_MODEL)
    pad_h = LANES - HEADS
    expand = (jnp.arange(LANES)[:, None] == jnp.arange(D_SSM)[None, :] // HEAD_DIM).astype(BF16)
    expand_pairs = (jnp.arange(LANES)[:, None] == jnp.arange(HEADS * ls)[None, :] // ls).astype(BF16)
    outs = [[] for _ in range(6)]
    for layer in range(depth):
        p = dict(
            nw=norm_w[layer][None, :],
            w_t=w_in[layer].T,
            conv_short_w=conv_short_w[layer],
            conv_w=conv_ssd_w[layer],
            conv_b=conv_ssd_b[layer][None, :],
            dtb=jnp.pad(dt_bias[layer], (0, pad_h))[None, :],
        )
        alr = jnp.pad(a_log[layer], (0, pad_h))[None, :]
        dexp = jnp.repeat(d_skip[layer], HEAD_DIM)[None, :]
        gnw = ssd_norm_w[layer][None, :]

        zg, xc, dt, yc, a2, a3 = _projections(
            hp, jnp.zeros((bp, CONV_K - 1, CONV_DIM), F32), jnp.zeros((bp, SHORT_K - 1, D_SHORT), F32), p,
            seqlen=lp)
        ys, a1, w_o = _ssd_chunked(zg, xc, dt, alr, expand, dexp, gnw, w_out[layer],
                                   nb=bp, nc=lp // SSD_CHUNK)
        zg2, xc2, dt2, yc2, s2, s3 = _projections(
            hs, state_conv_ssd[layer], state_conv_short[layer], p, seqlen=ls)
        ys2, s1 = _ssd_short(zg2, xc2, dt2, state_ssm[layer].reshape(bs, D_SSM, D_STATE), alr, expand,
                             expand_pairs, dexp, gnw, nb=bs, lc=ls)
        fw = final_norm_w[None, :]
        hp = _out(ys, yc, hp, w_o, fw, tm=OUT_ROW_TILE)
        hs = _out(ys2, yc2, hs, w_o, fw, tm=OUT_ROW_TILE)
        for lst, val in zip(outs, (a1.reshape(bp, HEADS, HEAD_DIM, D_STATE), a2, a3,
                                   s1.reshape(bs, HEADS, HEAD_DIM, D_STATE), s2, s3)):
            lst.append(val)
    return (hp.reshape(bp, lp, D_MODEL), hs.reshape(bs, ls, D_MODEL),
            *(jnp.stack(v) for v in outs))
```

```python
import functools

import jax
import jax.numpy as jnp
from jax import lax
from jax.experimental import pallas as pl
from jax.experimental.pallas import tpu as pltpu

F32 = jnp.float32
BF16 = jnp.bfloat16

D_MODEL = 2048
D_SSM = 2048
D_SHORT = 2048
HEADS = 32
HEAD_DIM = 64
D_STATE = 128
GROUPS = 4
HPG = HEADS // GROUPS
GROUP_W = HPG * HEAD_DIM
GN = GROUPS * D_STATE
CONV_DIM = D_SSM + 2 * GN
CONV_K = 4
SHORT_K = 3
SSD_CHUNK = 128
EPS = 1e-6
LANES = 128
SUBLANES = 8
BF16_SUBLANES = 16
VMEM_LIMIT = 56 * 1024 * 1024

ROW_TILE = 1024
TN = 1024
SHORT_ROW_TILE = 2048
SHORT_TN = 256
OUT_ROW_TILE = 512
NORM_ROWS = 32
NORM_UNROLL = 8
Z_TILES = D_SSM // TN
XC_TILES = CONV_DIM // TN
MAIN_VALID = D_SSM + CONV_DIM + HEADS
DT_BLOCK = (D_SSM + CONV_DIM) // LANES

NT_DIMS = (((1,), (1,)), ((), ()))
LOG2E = 1.4426950408889634
CHUNKS_PER_STEP = 2
SHORT_SEQS_PER_STEP = 8


def _silu_of_twice(half):
    return half + half * jnp.tanh(half)


def _silu(x):
    return _silu_of_twice(0.5 * x)


def _softplus(x):
    return jnp.maximum(x, 0.0) + jnp.log1p(jnp.exp(-jnp.abs(x)))


def _proj_z_body(x_ref, nw_ref, w_ref, wdt_ref, dtb_ref, h_ref, zg_ref, dt_ref, *wbf, tm):
    j = pl.program_id(1)
    if wbf:
        @pl.when(pl.program_id(0) == 0)
        def _():
            wbf[0][j] = (0.5 * w_ref[...]).astype(BF16)

    @pl.when(j == 0)
    def _():
        nw = nw_ref[...]

        def body(i, carry):
            r = pl.ds(pl.multiple_of(i * NORM_ROWS, NORM_ROWS), NORM_ROWS)
            xf = x_ref[r, :]
            var = jnp.mean(xf * xf, axis=-1, keepdims=True)
            h_ref[r, :] = (xf * lax.rsqrt(var + EPS) * nw).astype(BF16)
            return carry

        lax.fori_loop(0, tm // NORM_ROWS, body, 0, unroll=NORM_UNROLL)
        dt_raw = lax.dot_general(h_ref[...], wdt_ref[...].astype(BF16), NT_DIMS,
                                 preferred_element_type=F32)
        lane = lax.broadcasted_iota(jnp.int32, dt_raw.shape, 1)
        dt_ref[...] = jnp.where(lane < HEADS, _softplus(dt_raw + dtb_ref[...]), 0.0)

    if wbf:
        half = lax.dot_general(h_ref[...], wbf[0][j], NT_DIMS, preferred_element_type=F32)
        zg_ref[...] = _silu_of_twice(half).astype(BF16)
    else:
        acc = lax.dot_general(h_ref[...], w_ref[...].astype(BF16), NT_DIMS, preferred_element_type=F32)
        zg_ref[...] = _silu(acc).astype(BF16)


def _proj_z(x, nw, w_t, dtb, *, tm):
    t = x.shape[0]
    if t // tm > 1:
        w_spec = pl.BlockSpec((TN, D_MODEL), lambda m, j: (jnp.where(m == 0, j, Z_TILES - 1), 0),
                              pipeline_mode=pl.Buffered(1))
        scratch = [pltpu.VMEM((Z_TILES, TN, D_MODEL), BF16)]
    else:
        w_spec = pl.BlockSpec((TN, D_MODEL), lambda m, j: (j, 0))
        scratch = []
    return pl.pallas_call(
        functools.partial(_proj_z_body, tm=tm),
        grid=(t // tm, Z_TILES),
        in_specs=[
            pl.BlockSpec((tm, D_MODEL), lambda m, j: (m, 0)),
            pl.BlockSpec((1, D_MODEL), lambda m, j: (0, 0)),
            w_spec,
            pl.BlockSpec((LANES, D_MODEL), lambda m, j: (DT_BLOCK, 0)),
            pl.BlockSpec((1, LANES), lambda m, j: (0, 0)),
        ],
        out_specs=[pl.BlockSpec((tm, D_MODEL), lambda m, j: (m, 0)),
                   pl.BlockSpec((tm, TN), lambda m, j: (m, j)),
                   pl.BlockSpec((tm, LANES), lambda m, j: (m, 0))],
        out_shape=[jax.ShapeDtypeStruct((t, D_MODEL), BF16),
                   jax.ShapeDtypeStruct((t, D_SSM), BF16),
                   jax.ShapeDtypeStruct((t, LANES), F32)],
        scratch_shapes=scratch,
        compiler_params=pltpu.CompilerParams(
            dimension_semantics=("arbitrary", "arbitrary"), vmem_limit_bytes=VMEM_LIMIT),
        name="in_proj_z",
    )(x, nw, w_t, w_t, dtb)


def _load_history(pad_ref, st_ref, carry, j, *, first, hist, tps, row_major=False):
    def from_state():
        pad_ref[:, 0:SUBLANES, :] = jnp.zeros((pad_ref.shape[0], SUBLANES, pad_ref.shape[2]), F32)
        if row_major:
            for k in range(hist):
                pad_ref[:, SUBLANES - hist + k, :] = st_ref[k]
        else:
            pad_ref[:, SUBLANES - hist:SUBLANES, :] = st_ref[...]

    if tps > 1:
        (carry_ref,) = carry
        pl.when(first)(from_state)

        @pl.when(jnp.logical_not(first))
        def _():
            pad_ref[:, 0:SUBLANES, :] = carry_ref[j]
    else:
        from_state()


def _causal_conv(pad_ref, cw_ref, taps):
    xe = pad_ref[...]
    if taps == 4:
        s1 = pltpu.roll(xe, 1, axis=1)
        near = xe * cw_ref[3:4, :] + s1 * cw_ref[2:3, :]
        far = xe * cw_ref[1:2, :] + s1 * cw_ref[0:1, :]
        return near[:, SUBLANES:, :] + pltpu.roll(far, 2, axis=1)[:, SUBLANES:, :]
    conv = xe[:, SUBLANES:, :] * cw_ref[taps - 1:taps, :]
    for s in range(1, taps):
        shifted = pltpu.roll(xe, s, axis=1)[:, SUBLANES:, :]
        conv = conv + shifted * cw_ref[taps - 1 - s:taps - s, :]
    return conv


def _proj_xc_body(h_ref, w_ref, cw_ref, cb_ref, st_ref, xc_ref, tails_ref, pad_ref, *carry,
                  tm, nseq, tps, row_major):
    j = pl.program_id(1)
    lt = tm // nseq
    _load_history(pad_ref, st_ref, carry, j, first=(pl.program_id(0) % tps) == 0, hist=CONV_K - 1,
                  tps=tps, row_major=row_major)
    acc = lax.dot_general(h_ref[...], w_ref[...].astype(BF16), NT_DIMS, preferred_element_type=F32)
    pad_ref[:, SUBLANES:SUBLANES + lt, :] = acc.reshape(nseq, lt, TN)
    half = _causal_conv(pad_ref, 0.5 * cw_ref[...], CONV_K) + 0.5 * cb_ref[...]
    xc_ref[...] = _silu_of_twice(half).reshape(tm, TN)
    if row_major:
        for k in range(CONV_K - 1):
            tails_ref[k] = pad_ref[:, lt + SUBLANES - (CONV_K - 1) + k, :]
    else:
        tails_ref[...] = pad_ref[:, lt + SUBLANES - (CONV_K - 1):lt + SUBLANES, :]
    if tps > 1:
        carry[0][j] = pad_ref[:, lt:lt + SUBLANES, :]


def _proj_xc(h, w_t, conv_w, conv_b, conv_state, *, tm, nseq, tps):
    t = h.shape[0]
    lt = tm // nseq
    hist = CONV_K - 1
    row_major = nseq % SUBLANES == 0
    scratch = [pltpu.VMEM((nseq, SUBLANES + lt, TN), F32)]
    if tps > 1:
        scratch.append(pltpu.VMEM((XC_TILES, nseq, SUBLANES, TN), F32))
    ntail = (t // tm) * nseq
    if row_major:
        state = jnp.transpose(conv_state, (1, 0, 2))
        st_spec = pl.BlockSpec((hist, nseq, TN), lambda m, j: (0, m // tps, j))
        tail_spec = pl.BlockSpec((hist, nseq, TN), lambda m, j: (0, m, j))
        tail_shape = (hist, ntail, CONV_DIM)
    else:
        state = conv_state
        st_spec = pl.BlockSpec((nseq, hist, TN), lambda m, j: (m // tps, 0, j))
        tail_spec = pl.BlockSpec((nseq, hist, TN), lambda m, j: (m, 0, j))
        tail_shape = (ntail, hist, CONV_DIM)
    xc, tails = pl.pallas_call(
        functools.partial(_proj_xc_body, tm=tm, nseq=nseq, tps=tps, row_major=row_major),
        grid=(t // tm, XC_TILES),
        in_specs=[
            pl.BlockSpec((tm, D_MODEL), lambda m, j: (m, 0)),
            pl.BlockSpec((TN, D_MODEL), lambda m, j: (Z_TILES + j, 0)),
            pl.BlockSpec((CONV_K, TN), lambda m, j: (0, j)),
            pl.BlockSpec((1, TN), lambda m, j: (0, j)),
            st_spec,
        ],
        out_specs=[pl.BlockSpec((tm, TN), lambda m, j: (m, j)), tail_spec],
        out_shape=[jax.ShapeDtypeStruct((t, CONV_DIM), F32), jax.ShapeDtypeStruct(tail_shape, F32)],
        scratch_shapes=scratch,
        compiler_params=pltpu.CompilerParams(
            dimension_semantics=("arbitrary", "arbitrary"), vmem_limit_bytes=VMEM_LIMIT),
        name="in_proj_xc",
    )(h, w_t, conv_w, conv_b, state)
    if row_major:
        tails = tails.reshape(hist, t // (tm * tps), tps, nseq, CONV_DIM)[:, :, -1]
        return xc, jnp.transpose(tails.reshape(hist, -1, CONV_DIM), (1, 0, 2))
    tails = tails.reshape(t // (tm * tps), tps, nseq, hist, CONV_DIM)[:, -1]
    return xc, tails.reshape(conv_state.shape)


def _short_body(h_ref, *refs, tm, tn, nseq, tps, shift):
    w_refs, (cw_ref, st_ref, y_ref, ns_ref, wbf_ref, pad_ref), carry = refs[:8], refs[8:14], refs[14:]
    m = pl.program_id(1)
    lt = tm // nseq

    @pl.when(m == 0)
    def _():
        for k in range(4):
            a_ref, b_ref = w_refs[2 * k], w_refs[2 * k + 1]
            w = jnp.concatenate([a_ref[shift:, :], b_ref[...]], axis=0)
            wbf_ref[k] = ((0.5 * w) if k == 0 else w).astype(BF16)

    _load_history(pad_ref, st_ref, carry, 0, first=(m % tps) == 0, hist=SHORT_K - 1, tps=tps)
    h = h_ref[...]
    band = lambda k: lax.dot_general(h, wbf_ref[k], NT_DIMS, preferred_element_type=F32)
    c = band(2)
    v = band(3)
    pad_ref[:, SUBLANES:SUBLANES + lt, :] = (c * v).reshape(nseq, lt, tn)
    conv = _causal_conv(pad_ref, cw_ref, SHORT_K)
    b = band(1)
    y = b * conv.reshape(tm, tn) * _silu_of_twice(band(0))
    y_ref[...] = y.astype(y_ref.dtype)
    ns_ref[...] = pad_ref[:, lt + SUBLANES - (SHORT_K - 1):lt + SUBLANES, :]
    if tps > 1:
        carry[0][0] = pad_ref[:, lt:lt + SUBLANES, :]


def _short(h, w_t, conv_w, state, *, tm, tn, nseq, tps):
    t = h.shape[0]
    nj = D_SHORT // tn
    lt = tm // nseq
    base = (MAIN_VALID // tn) * tn
    shift = MAIN_VALID - base
    assert shift % SUBLANES == 0 and tn % shift == 0 and base % shift == 0
    scratch = [pltpu.VMEM((4, tn, D_MODEL), BF16), pltpu.VMEM((nseq, SUBLANES + lt, tn), F32)]
    if tps > 1:
        scratch.append(pltpu.VMEM((1, nseq, SUBLANES, tn), F32))
    w_specs = []
    for k in range(4):
        r0 = base + k * D_SHORT
        w_specs.append(pl.BlockSpec((tn, D_MODEL), lambda j, m, r0=r0: (r0 // tn + j, 0)))
        w_specs.append(pl.BlockSpec((shift, D_MODEL), lambda j, m, r0=r0: ((r0 + tn * (j + 1)) // shift, 0)))

    y, tails = pl.pallas_call(
        functools.partial(_short_body, tm=tm, tn=tn, nseq=nseq, tps=tps, shift=shift),
        grid=(nj, t // tm),
        in_specs=[
            pl.BlockSpec((tm, D_MODEL), lambda j, m: (m, 0)),
            *w_specs,
            pl.BlockSpec((SHORT_K, tn), lambda j, m: (0, j)),
            pl.BlockSpec((nseq, SHORT_K - 1, tn), lambda j, m: (m // tps, 0, j)),
        ],
        out_specs=[pl.BlockSpec((tm, tn), lambda j, m: (m, j)),
                   pl.BlockSpec((nseq, SHORT_K - 1, tn), lambda j, m: (m, 0, j))],
        out_shape=[jax.ShapeDtypeStruct((t, D_SHORT), BF16),
                   jax.ShapeDtypeStruct(((t // tm) * nseq, SHORT_K - 1, D_SHORT), F32)],
        scratch_shapes=scratch,
        compiler_params=pltpu.CompilerParams(
            dimension_semantics=("arbitrary", "arbitrary"), vmem_limit_bytes=VMEM_LIMIT),
        name="short_conv",
    )(h, *([w_t] * 8), conv_w, state)
    tails = tails.reshape(t // (tm * tps), tps, nseq, SHORT_K - 1, D_SHORT)[:, -1]
    return y, tails.reshape(state.shape)


def _dot_split(x, w, pieces):
    out = None
    r = x
    for _ in range(pieces):
        p = r.astype(BF16)
        d = jnp.dot(p, w, preferred_element_type=F32)
        out = d if out is None else out + d
        r = r - p.astype(F32)
    return out


def _prefix_sum_rows(x, period):
    t = lax.broadcasted_iota(jnp.int32, x.shape, 0) % period
    k = 1
    while k < period:
        x = x + jnp.where(t >= k, pltpu.roll(x, k, axis=0), 0.0)
        k *= 2
    return x


def _gated_norm_store(y, zg_ref, gnw_ref, y_ref):
    y = y * zg_ref[...].astype(F32)
    for g in range(GROUPS):
        cs = slice(g * GROUP_W, (g + 1) * GROUP_W)
        blk = y[:, cs]
        ms = jnp.mean(blk * blk, axis=-1, keepdims=True)
        y_ref[:, cs] = (blk * lax.rsqrt(ms + EPS) * gnw_ref[:, cs]).astype(y_ref.dtype)


def _ssd_chunk_body(zg_ref, xc_ref, dt_ref, alr_ref, expand_ref, dexp_ref, gnw_ref, wsrc_ref,
                    y_ref, hout_ref, wdst_ref, hst_ref, ysc_ref, *, lc, cps, nsteps):
    c = pl.program_id(1)
    wdst_ref[...] = wsrc_ref[...].astype(BF16)

    @pl.when(c == 0)
    def _():
        hst_ref[...] = jnp.zeros(hst_ref.shape, F32)

    row_i = lax.broadcasted_iota(jnp.int32, (lc, lc), 0)
    col_i = lax.broadcasted_iota(jnp.int32, (lc, lc), 1)
    causal = row_i >= col_i
    lane_lo = lax.broadcasted_iota(jnp.int32, (lc, LANES), 1) < HEAD_DIM

    a_row = -jnp.exp(alr_ref[...])

    for k in range(cps):
        tok = slice(k * lc, (k + 1) * lc)
        dt = dt_ref[tok, :]
        a_cum = _prefix_sum_rows(dt * a_row, lc)
        a_last = a_cum[lc - 1:lc, :]
        factors = jnp.concatenate([jnp.exp(a_cum), jnp.exp(a_last - a_cum) * dt], axis=0).astype(BF16)
        a2 = a_cum * LOG2E
        r_t = (a2 - jnp.log2(dt)).T

        for g in range(GROUPS):
            bg = xc_ref[tok, D_SSM + g * D_STATE:D_SSM + (g + 1) * D_STATE]
            cg = xc_ref[tok, D_SSM + GN + g * D_STATE:D_SSM + GN + (g + 1) * D_STATE].astype(BF16)
            cols = slice(g * GROUP_W, (g + 1) * GROUP_W)
            h_prev = hst_ref[:, cols]
            cb = lax.dot_general(cg, bg.astype(BF16), NT_DIMS, preferred_element_type=F32)
            y_off = jnp.dot(cg, h_prev.astype(BF16), preferred_element_type=F32)
            xs_g = xc_ref[tok, cols]
            both = jnp.dot(factors, expand_ref[:, cols], preferred_element_type=F32)
            e_exp, w_exp = both[:lc], both[lc:]
            for pr in range(HPG // 2):
                mats = []
                for hh in (g * HPG + 2 * pr, g * HPG + 2 * pr + 1):
                    seg = a2[:, hh:hh + 1] - r_t[hh:hh + 1, :]
                    dec = jnp.exp2(jnp.where(causal, seg, -jnp.inf))
                    mats.append((cb * dec).astype(BF16))
                xp = xs_g[:, pr * LANES:(pr + 1) * LANES].astype(BF16)
                rhs = jnp.concatenate([jnp.where(lane_lo, xp, 0), jnp.where(lane_lo, 0, xp)], axis=0)
                y_diag = jnp.dot(jnp.concatenate(mats, axis=1), rhs, preferred_element_type=F32)
                pc = slice(g * GROUP_W + pr * LANES, g * GROUP_W + (pr + 1) * LANES)
                ysc_ref[tok, pc] = y_diag + (y_off * e_exp)[:, pr * LANES:(pr + 1) * LANES]
            xw = (xs_g * w_exp).astype(BF16)
            s_t = jnp.dot(bg.T.astype(BF16), xw, preferred_element_type=F32)
            hst_ref[:, cols] = h_prev * e_exp[lc - 1:lc, :] + s_t

    y = ysc_ref[...] + xc_ref[:, 0:D_SSM] * dexp_ref[...]
    _gated_norm_store(y, zg_ref, gnw_ref, y_ref)

    @pl.when(c == nsteps - 1)
    def _():
        hout_ref[0] = hst_ref[...].T


def _ssd_short_body(zg_ref, xc_ref, dt_ref, h0_ref, alr_ref, expand_ref, exps_ref, dexp_ref, gnw_ref,
                    y_ref, hout_ref, ysc_ref, *, lc, spb):
    rows = spb * lc
    hl = HEADS * lc
    dt = dt_ref[...]
    a_cum = _prefix_sum_rows(dt * (-jnp.exp(alr_ref[...])), lc)
    a_last = a_cum.reshape(spb, lc, LANES)[:, lc - 1:lc, :]
    a_last_b = jnp.broadcast_to(a_last, (spb, lc, LANES)).reshape(rows, LANES)
    expand = expand_ref[...]
    e_exp = _dot_split(jnp.exp(a_cum), expand, 2)
    w_exp = _dot_split(jnp.exp(a_last_b - a_cum) * dt, expand, 2)
    cd_t = jnp.exp(a_last.reshape(spb, LANES)).T

    z2 = _dot_split(jnp.concatenate([a_cum, dt], axis=0), exps_ref[...], 3)
    zc, zd = z2[:rows], z2[rows:]
    trow = lax.broadcasted_iota(jnp.int32, (rows, hl), 0) % lc
    tsrc = lax.broadcasted_iota(jnp.int32, (rows, hl), 1) % lc
    diag = trow == tsrc

    def per_source(z):
        d = jnp.where(diag, z, 0.0).reshape(spb, lc, hl).sum(axis=1, keepdims=True)
        return jnp.broadcast_to(d, (spb, lc, hl)).reshape(rows, hl)

    dec = jnp.exp(jnp.where(trow >= tsrc, zc - per_source(zc), -jnp.inf)) * per_source(zd)

    pair_w = HPG * lc
    blockdiag = (lax.broadcasted_iota(jnp.int32, (pair_w, GROUP_W), 0) // lc
                 == lax.broadcasted_iota(jnp.int32, (pair_w, GROUP_W), 1) // HEAD_DIM)

    for q in range(spb):
        tok = slice(q * lc, (q + 1) * lc)
        for g in range(GROUPS):
            bg = xc_ref[tok, D_SSM + g * D_STATE:D_SSM + (g + 1) * D_STATE]
            cg = xc_ref[tok, D_SSM + GN + g * D_STATE:D_SSM + GN + (g + 1) * D_STATE].astype(BF16)
            cols = slice(g * GROUP_W, (g + 1) * GROUP_W)
            h_prev = h0_ref[q, cols, :]
            xs_g = xc_ref[tok, cols]
            b_tiled = jnp.concatenate([bg] * HPG, axis=0).astype(BF16)
            cb = lax.dot_general(cg, b_tiled, NT_DIMS, preferred_element_type=F32)
            m = (cb * dec[tok, g * pair_w:(g + 1) * pair_w]).astype(BF16)
            x_diag = jnp.where(blockdiag, jnp.concatenate([xs_g] * HPG, axis=0), 0.0).astype(BF16)
            y_diag = jnp.dot(m, x_diag, preferred_element_type=F32)
            y_off = lax.dot_general(cg, h_prev.astype(BF16), NT_DIMS, preferred_element_type=F32)
            ysc_ref[tok, cols] = y_diag + y_off * e_exp[tok, cols]
            xw_t = (xs_g * w_exp[tok, cols]).T.astype(BF16)
            s_g = jnp.dot(xw_t, bg.astype(BF16), preferred_element_type=F32)
            cd = jnp.concatenate(
                [jnp.broadcast_to(cd_t[g * HPG + jh:g * HPG + jh + 1, q:q + 1], (HEAD_DIM, D_STATE))
                 for jh in range(HPG)], axis=0)
            hout_ref[q, cols, :] = h_prev * cd + s_g

    y = ysc_ref[...] + xc_ref[:, 0:D_SSM] * dexp_ref[...]
    _gated_norm_store(y, zg_ref, gnw_ref, y_ref)


def _ssd_chunked(zg, xc, dt, alr, expand, dexp, gnw, w_f32, *, nb, nc):
    t = zg.shape[0]
    lc = SSD_CHUNK
    cps = CHUNKS_PER_STEP
    nsteps = nc // cps
    rows = cps * lc
    wrows = w_f32.shape[0] // (nb * nsteps)
    assert nsteps * cps == nc and wrows * nb * nsteps == w_f32.shape[0] and wrows % BF16_SUBLANES == 0
    tok = lambda r, width: pl.BlockSpec((r, width), lambda b, c: (b * nsteps + c, 0))
    par = lambda r, width: pl.BlockSpec((r, width), lambda b, c: (0, 0))
    h_spec = pl.BlockSpec((1, D_SSM, D_STATE), lambda b, c: (b, 0, 0))
    return pl.pallas_call(
        functools.partial(_ssd_chunk_body, lc=lc, cps=cps, nsteps=nsteps),
        grid=(nb, nsteps),
        in_specs=[tok(rows, D_SSM), tok(rows, CONV_DIM), tok(rows, LANES), par(1, LANES),
                  par(LANES, D_SSM), par(1, D_SSM), par(1, D_SSM), tok(wrows, w_f32.shape[1])],
        out_specs=[tok(rows, D_SSM), h_spec, tok(wrows, w_f32.shape[1])],
        out_shape=[jax.ShapeDtypeStruct((t, D_SSM), BF16),
                   jax.ShapeDtypeStruct((nb, D_SSM, D_STATE), F32),
                   jax.ShapeDtypeStruct(w_f32.shape, BF16)],
        scratch_shapes=[pltpu.VMEM((D_STATE, D_SSM), F32), pltpu.VMEM((rows, D_SSM), F32)],
        compiler_params=pltpu.CompilerParams(
            dimension_semantics=("arbitrary", "arbitrary"), vmem_limit_bytes=VMEM_LIMIT),
        name="ssd_scan",
    )(zg, xc, dt, alr, expand, dexp, gnw, w_f32)


def _ssd_short(zg, xc, dt, h0, alr, expand, exps, dexp, gnw, *, nb, lc):
    t = zg.shape[0]
    spb = SHORT_SEQS_PER_STEP
    rows = spb * lc
    tok = lambda width: pl.BlockSpec((rows, width), lambda b: (b, 0))
    par = lambda r, width: pl.BlockSpec((r, width), lambda b: (0, 0))
    h_spec = pl.BlockSpec((spb, D_SSM, D_STATE), lambda b: (b, 0, 0))
    return pl.pallas_call(
        functools.partial(_ssd_short_body, lc=lc, spb=spb),
        grid=(nb // spb,),
        in_specs=[tok(D_SSM), tok(CONV_DIM), tok(LANES), h_spec, par(1, LANES), par(LANES, D_SSM),
                  par(LANES, HEADS * lc), par(1, D_SSM), par(1, D_SSM)],
        out_specs=[tok(D_SSM), h_spec],
        out_shape=[jax.ShapeDtypeStruct((t, D_SSM), BF16),
                   jax.ShapeDtypeStruct((nb, D_SSM, D_STATE), F32)],
        scratch_shapes=[pltpu.VMEM((rows, D_SSM), F32)],
        compiler_params=pltpu.CompilerParams(
            dimension_semantics=("arbitrary",), vmem_limit_bytes=VMEM_LIMIT),
        name="ssd_scan_short",
    )(zg, xc, dt, h0, alr, expand, exps, dexp, gnw)


def _out_body(ys_ref, yc_ref, x_ref, w1_ref, w2_ref, fnw_ref, o_ref):
    acc = jnp.dot(ys_ref[...], w1_ref[...], preferred_element_type=F32)
    acc = acc + jnp.dot(yc_ref[...], w2_ref[...], preferred_element_type=F32)
    r = x_ref[...] + acc
    var = jnp.mean(r * r, axis=-1, keepdims=True)
    o_ref[...] = r * lax.rsqrt(var + EPS) * fnw_ref[...]


def _out(ys, yc, x, w_o, fnw, *, tm):
    t = x.shape[0]
    row = lambda width: pl.BlockSpec((tm, width), lambda m: (m, 0))
    return pl.pallas_call(
        _out_body,
        grid=(t // tm,),
        in_specs=[row(D_SSM), row(D_SHORT), row(D_MODEL),
                  pl.BlockSpec((D_SSM, D_MODEL), lambda m: (0, 0), pipeline_mode=pl.Buffered(1)),
                  pl.BlockSpec((D_SHORT, D_MODEL), lambda m: (1, 0), pipeline_mode=pl.Buffered(1)),
                  pl.BlockSpec((1, D_MODEL), lambda m: (0, 0))],
        out_specs=row(D_MODEL),
        out_shape=jax.ShapeDtypeStruct((t, D_MODEL), F32),
        compiler_params=pltpu.CompilerParams(
            dimension_semantics=("arbitrary",), vmem_limit_bytes=VMEM_LIMIT),
        name="out_proj",
    )(ys, yc, x, w_o, w_o, fnw)


def _projections(x2d, conv_state, short_state, p, *, seqlen):
    tm = ROW_TILE
    nseq = max(tm // seqlen, 1)
    tps = max(seqlen // tm, 1)
    assert x2d.shape[0] % tm == 0 and (seqlen % tm == 0 or tm % seqlen == 0)
    h, zg, dt = _proj_z(x2d, p["nw"], p["w_t"], p["dtb"], tm=tm)
    xc, conv_new = _proj_xc(h, p["w_t"], p["conv_w"], p["conv_b"], conv_state,
                            tm=tm, nseq=nseq, tps=tps)
    tm_s = min(SHORT_ROW_TILE, x2d.shape[0])
    assert seqlen % tm_s == 0 or tm_s % seqlen == 0
    y_c, short_new = _short(h, p["w_t"], p["conv_short_w"], short_state,
                            tm=tm_s, tn=SHORT_TN, nseq=max(tm_s // seqlen, 1), tps=max(seqlen // tm_s, 1))
    return zg, xc, dt, y_c, conv_new, short_new


def kernel(x_prompt, x_sample, state_ssm, state_conv_ssd, state_conv_short, norm_w, w_in, conv_ssd_w,
           conv_ssd_b, dt_bias, a_log, d_skip, ssd_norm_w, conv_short_w, w_out, final_norm_w):
    depth = norm_w.shape[0]
    assert depth == 1, "the output projection fuses the final rmsnorm, valid for a single layer"
    bp, lp, _ = x_prompt.shape
    bs, ls, _ = x_sample.shape
    assert lp % SSD_CHUNK == 0 and ls < SSD_CHUNK
    hp = x_prompt.reshape(bp * lp, D_MODEL)
    hs = x_sample.reshape(bs * ls, D_MODEL)
    pad_h = LANES - HEADS
    expand = (jnp.arange(LANES)[:, None] == jnp.arange(D_SSM)[None, :] // HEAD_DIM).astype(BF16)
    expand_pairs = (jnp.arange(LANES)[:, None] == jnp.arange(HEADS * ls)[None, :] // ls).astype(BF16)
    outs = [[] for _ in range(6)]
    for layer in range(depth):
        p = dict(
            nw=norm_w[layer][None, :],
            w_t=w_in[layer].T,
            conv_short_w=conv_short_w[layer],
            conv_w=conv_ssd_w[layer],
            conv_b=conv_ssd_b[layer][None, :],
            dtb=jnp.pad(dt_bias[layer], (0, pad_h))[None, :],
        )
        alr = jnp.pad(a_log[layer], (0, pad_h))[None, :]
        dexp = jnp.repeat(d_skip[layer], HEAD_DIM)[None, :]
        gnw = ssd_norm_w[layer][None, :]

        zg, xc, dt, yc, a2, a3 = _projections(
            hp, jnp.zeros((bp, CONV_K - 1, CONV_DIM), F32), jnp.zeros((bp, SHORT_K - 1, D_SHORT), F32), p,
            seqlen=lp)
        ys, a1, w_o = _ssd_chunked(zg, xc, dt, alr, expand, dexp, gnw, w_out[layer],
                                   nb=bp, nc=lp // SSD_CHUNK)
        zg2, xc2, dt2, yc2, s2, s3 = _projections(
            hs, state_conv_ssd[layer], state_conv_short[layer], p, seqlen=ls)
        ys2, s1 = _ssd_short(zg2, xc2, dt2, state_ssm[layer].reshape(bs, D_SSM, D_STATE), alr, expand,
                             expand_pairs, dexp, gnw, nb=bs, lc=ls)
        fw = final_norm_w[None, :]
        hp = _out(ys, yc, hp, w_o, fw, tm=OUT_ROW_TILE)
        hs = _out(ys2, yc2, hs, w_o, fw, tm=OUT_ROW_TILE)
        for lst, val in zip(outs, (a1.reshape(bp, HEADS, HEAD_DIM, D_STATE), a2, a3,
                                   s1.reshape(bs, HEADS, HEAD_DIM, D_STATE), s2, s3)):
            lst.append(val)
    return (hp.reshape(bp, lp, D_MODEL), hs.reshape(bs, ls, D_MODEL),
            *(jnp.stack(v) for v in outs))
```
